```python
import math
import jax
import jax.numpy as jnp
from jax import lax
import numpy as np

D_MODEL = 1024
BATCH = 32
SEQ = 256
DEPTH = 4
DEC_BATCH = 2
DEC_SEQ = 1024
PAST_LEN = 512

GRID_W = 64
N_EVEN = (DEPTH + 1) // 2
N_ODD = DEPTH // 2
NORM_EPS = 1e-6
ROPE_THETA = 10000.0
Q_BLOCK = 128

A_HEADS = 8
A_KV_HEADS = 2
A_DIM = 64
A_GROUP = A_HEADS // A_KV_HEADS
B_HEADS = 8
B_DIM = 64
B_WIDTH = B_HEADS * B_DIM
DECAY_LORA = 64
ICLR_LORA = 64
GATE_LORA = 128
B_COLS = 3 * B_WIDTH + DECAY_LORA + ICLR_LORA + GATE_LORA
LNX_EPS = 64e-5
C_HEADS = 4
C_DIM = 128
C_WIDTH = C_HEADS * C_DIM
CONV_W = 3
CHUNK = 64
D_HEADS = 4
D_DIM = 64
D_VDIM = 128
D_QK = D_HEADS * 2 * D_DIM
D_V = D_HEADS * D_VDIM
EVEN_COLS = A_HEADS * A_DIM + 2 * A_KV_HEADS * A_DIM + B_COLS
EVEN_OUT = A_HEADS * A_DIM + B_WIDTH
ODD_COLS = 4 * C_WIDTH + 4 * C_HEADS + 2 * D_QK + D_V
ODD_OUT = C_WIDTH + D_V
N_EXPERTS = 64
TOP_K = 6
N_GROUPS = 8
TOPK_GROUPS = 4
EXPERT_FF = 256
SHARED_FF = 256
ROUTED_SCALE = 1.0
MOE_BLOCK = 128

kernel_name = 'hybrid_prefix_diffusion_step'


def rms_norm(x, g, eps=NORM_EPS):
    xf = x.astype(jnp.float32)
    y = xf * lax.rsqrt(jnp.mean(xf * xf, axis=-1, keepdims=True) + eps)
    return (y * g.astype(jnp.float32)).astype(x.dtype)


def l2_normalize(x, eps=1e-6):
    xf = x.astype(jnp.float32)
    return xf * lax.rsqrt(jnp.sum(xf * xf, axis=-1, keepdims=True) + eps)


def group_norm_heads(y, w, b, eps=LNX_EPS):
    mu = jnp.mean(y, axis=-1, keepdims=True)
    var = jnp.mean(jnp.square(y - mu), axis=-1, keepdims=True)
    return (y - mu) * lax.rsqrt(var + eps) * w + b


def adaln(cond, w, b):
    m = jax.nn.silu(cond) @ w + b
    return m.reshape(cond.shape[0], 6, D_MODEL)


def modulate(x, shift, scale):
    return x * (1 + scale) + shift


def axial_rope_tables(n_tok, dim):
    rows = n_tok // GRID_W
    row = jnp.repeat(jnp.arange(rows, dtype=jnp.float32), GRID_W)
    col = jnp.tile(jnp.arange(GRID_W, dtype=jnp.float32), rows)
    n_freq = dim // 4
    inv = ROPE_THETA ** (-jnp.arange(n_freq, dtype=jnp.float32) / n_freq)
    ang = jnp.concatenate([row[:, None] * inv, col[:, None] * inv], axis=-1)
    return jnp.cos(ang), jnp.sin(ang)


def apply_rope(x, cos, sin):
    xf = x.astype(jnp.float32)
    x1, x2 = xf[..., 0::2], xf[..., 1::2]
    out = jnp.stack([x1 * cos - x2 * sin, x1 * sin + x2 * cos], axis=-1)
    return out.reshape(x.shape).astype(x.dtype)


def block_attention(q, k, v):
    bsz, hk, grp, s, d = q.shape
    nb = s // Q_BLOCK
    qb = jnp.moveaxis(q.reshape(bsz, hk, grp, nb, Q_BLOCK, d), 3, 0)
    scale = d ** -0.5

    def one_block(qblk):
        sc = jnp.einsum('bhgqd,bhkd->bhgqk', qblk, k, preferred_element_type=jnp.float32) * scale
        pr = jax.nn.softmax(sc, axis=-1).astype(v.dtype)
        return jnp.einsum('bhgqk,bhkd->bhgqd', pr, v)

    ob = lax.map(one_block, qb)
    return jnp.moveaxis(ob, 0, 3).reshape(bsz, hk, grp, s, v.shape[-1])


def centred_token_shift(x, mu):
    xp = jnp.pad(x, ((0, 0), (1, 1), (0, 0)))
    neighbours = 0.5 * (xp[:, :-2] + xp[:, 2:])
    return x + mu * (neighbours - x)


def centred_depthwise_conv(x, w):
    ch = x.shape[-1]
    return lax.conv_general_dilated(x, w[:, None, :].astype(x.dtype), window_strides=(1,),
                                    padding=[(CONV_W // 2, CONV_W // 2)],
                                    dimension_numbers=('NWC', 'WIO', 'NWC'), feature_group_count=ch)


def rwkv7_scan(S0, r, w, k, v, a, b):
    def step(S, inp):
        r_t, w_t, k_t, v_t, a_t, b_t = inp
        sa = jnp.einsum('bhij,bhj->bhi', S, a_t)
        S = S * w_t[:, :, None, :] + sa[..., None] * b_t[:, :, None, :] + v_t[..., None] * k_t[:, :, None, :]
        return S, jnp.einsum('bhij,bhj->bhi', S, r_t)

    xs = tuple(jnp.moveaxis(t, 1, 0) for t in (r, w, k, v, a, b))
    S, ys = lax.scan(step, S0, xs)
    return jnp.moveaxis(ys, 0, 1), S


def gated_delta_chunked(q, k, v, g, beta, S0):
    bsz, t_len, nh, dk = k.shape
    n = t_len // CHUNK

    def chunks(t):
        return jnp.moveaxis(t.reshape((bsz, n, CHUNK, nh) + t.shape[3:]), 3, 1)

    q = chunks(q * dk ** -0.5)
    k = chunks(k)
    v = chunks(v)
    g = jnp.cumsum(chunks(g), axis=-1)
    beta = chunks(beta)
    incl = jnp.tril(jnp.ones((CHUNK, CHUNK), bool))
    strict = jnp.tril(jnp.ones((CHUNK, CHUNK), bool), -1)
    gdiff = g[..., :, None] - g[..., None, :]
    decay = jnp.where(incl, jnp.exp(jnp.where(incl, gdiff, 0.0)), 0.0)
    kb = k * beta[..., None]
    L = jnp.where(strict, jnp.einsum('bhnid,bhnjd->bhnij', kb, k) * decay, 0.0)
    eye = jnp.eye(CHUNK, dtype=jnp.float32)
    t_inv = lax.linalg.triangular_solve(eye + L, jnp.broadcast_to(eye, L.shape), left_side=True,
                                        lower=True, unit_diagonal=True)
    u = t_inv @ (v * beta[..., None])
    wk = t_inv @ (kb * jnp.exp(g)[..., None])
    attn = jnp.einsum('bhnid,bhnjd->bhnij', q, k) * decay
    qg = q * jnp.exp(g)[..., None]
    kg = k * jnp.exp(g[..., -1:] - g)[..., None]
    glast = jnp.exp(g[..., -1])

    def step(S, inp):
        u_i, w_i, a_i, qg_i, kg_i, gl_i = inp
        v_new = u_i - jnp.einsum('bhcd,bhde->bhce', w_i, S)
        o = jnp.einsum('bhcd,bhde->bhce', qg_i, S) + jnp.einsum('bhcs,bhse->bhce', a_i, v_new)
        S = S * gl_i[..., None, None] + jnp.einsum('bhcd,bhce->bhde', kg_i, v_new)
        return S, o

    xs = tuple(jnp.moveaxis(t, 2, 0) for t in (u, wk, attn, qg, kg, glast))
    S, o = lax.scan(step, S0, xs)
    o = jnp.moveaxis(jnp.moveaxis(o, 0, 2), 1, 3)
    return o.reshape(bsz, t_len, nh, o.shape[-1]), S


def even_mixer(h, p, ctx=None):
    bsz, t_len, _ = h.shape
    f32 = jnp.float32
    proj = h @ p['w_in']
    a_q = A_HEADS * A_DIM
    a_kv = A_KV_HEADS * A_DIM
    aq, ak, av, xb = jnp.split(proj, [a_q, a_q + a_kv, a_q + 2 * a_kv], axis=-1)
    aq = jnp.transpose(rms_norm(aq.reshape(bsz, t_len, A_HEADS, A_DIM), p['a_qn']), (0, 2, 1, 3))
    ak = jnp.transpose(rms_norm(ak.reshape(bsz, t_len, A_KV_HEADS, A_DIM), p['a_kn']), (0, 2, 1, 3))
    av = jnp.transpose(av.reshape(bsz, t_len, A_KV_HEADS, A_DIM), (0, 2, 1, 3))
    if ctx is None:
        k_all, v_all = ak, av
        S0 = jnp.zeros((bsz, 2, B_HEADS, B_DIM, B_DIM), f32)
    else:
        cache_k, cache_v, S0 = ctx
        cos, sin = axial_rope_tables(t_len, A_DIM)
        aq = apply_rope(aq, cos, sin)
        k_all = jnp.concatenate([cache_k.astype(ak.dtype), apply_rope(ak, cos, sin)], axis=2)
        v_all = jnp.concatenate([cache_v.astype(av.dtype), av], axis=2)
        S0 = S0.astype(f32)
    oa = block_attention(aq.reshape(bsz, A_KV_HEADS, A_GROUP, t_len, A_DIM), k_all, v_all)
    oa = jnp.transpose(oa, (0, 3, 1, 2, 4)).reshape(bsz, t_len, a_q)
    xb = centred_token_shift(xb, p['b_mu'])
    splits = [B_WIDTH, 2 * B_WIDTH, 3 * B_WIDTH, 3 * B_WIDTH + DECAY_LORA, 3 * B_WIDTH + DECAY_LORA + ICLR_LORA]
    r, kb, vb, wd, ad, gd = jnp.split(xb.astype(f32), splits, axis=-1)

    def heads(t):
        return t.astype(f32).reshape(bsz, t_len, B_HEADS, B_DIM)

    r_h, k_h, v_h = heads(r), heads(kb), heads(vb)
    kk = l2_normalize(heads(kb * p['b_kk']))
    y = jnp.zeros_like(v_h)
    finals = []
    for d in range(2):
        w_log = -jax.nn.softplus(-(p['b_w0'][d] + jnp.tanh(wd) @ p['b_w2'][d])) - 0.5
        a = jax.nn.sigmoid(p['b_a0'][d] + ad @ p['b_a2'][d])
        k_d = kb * (1 + (a - 1) * p['b_ka'])
        seqs = (r_h, heads(jnp.exp(-jnp.exp(w_log))), heads(k_d), v_h, -kk, kk * heads(a))
        if d == 1:
            seqs = tuple(jnp.flip(t, axis=1) for t in seqs)
        y_d, S_d = rwkv7_scan(S0[:, d], *seqs)
        y = y + (jnp.flip(y_d, axis=1) if d == 1 else y_d)
        finals.append(S_d)
    y = group_norm_heads(y, p['b_lnx_w'].reshape(B_HEADS, B_DIM), p['b_lnx_b'].reshape(B_HEADS, B_DIM))
    y = y + jnp.sum(r_h * k_h * p['b_rk'], axis=-1, keepdims=True) * v_h
    ob = y.reshape(bsz, t_len, B_WIDTH) * (jax.nn.sigmoid(gd) @ p['b_g2'])
    out = jnp.concatenate([oa, ob.astype(h.dtype)], axis=-1) @ p['w_out']
    new_ctx = (ak, av, jnp.stack(finals, axis=1)) if ctx is None else None
    return out, new_ctx


def odd_mixer(h, p, lam_init, ctx=None):
    bsz, t_len, _ = h.shape
    f32 = jnp.float32
    proj = h @ p['w_in']
    s0 = 4 * C_WIDTH + 4 * C_HEADS
    cqkv, cz, cab, dq, dk, dv = jnp.split(
        proj, [3 * C_WIDTH, 4 * C_WIDTH, s0, s0 + D_QK, s0 + 2 * D_QK], axis=-1)
    cqkv = jax.nn.silu(centred_depthwise_conv(cqkv, p['c_conv']))

    def heads(t):
        return t.astype(f32).reshape(bsz, t_len, C_HEADS, C_DIM)

    cq, ck, cv = (heads(t) for t in jnp.split(cqkv, 3, axis=-1))
    cq, ck = l2_normalize(cq), l2_normalize(ck)
    cab = cab.astype(f32).reshape(bsz, t_len, 4, C_HEADS)
    if ctx is None:
        S0 = jnp.zeros((bsz, 2, C_HEADS, C_DIM, C_DIM), f32)
    else:
        cache_k, cache_v, S0 = ctx
        S0 = S0.astype(f32)
    oc = jnp.zeros_like(cv)
    finals = []
    for d in range(2):
        g = -jnp.exp(p['c_A_log'][d]) * jax.nn.softplus(cab[:, :, d] + p['c_dt_bias'][d])
        beta = jax.nn.sigmoid(cab[:, :, 2 + d])
        seqs = (cq, ck, cv, g, beta)
        if d == 1:
            seqs = tuple(jnp.flip(t, axis=1) for t in seqs)
        o_d, S_d = gated_delta_chunked(*seqs, S0[:, d])
        oc = oc + (jnp.flip(o_d, axis=1) if d == 1 else o_d)
        finals.append(S_d)
    oc = rms_norm(oc, p['c_onorm']) * jax.nn.silu(heads(cz))
    oc = oc.reshape(bsz, t_len, C_WIDTH).astype(h.dtype)
    dq = jnp.transpose(rms_norm(dq.reshape(bsz, t_len, D_HEADS, 2, D_DIM), p['d_qn']), (0, 2, 3, 1, 4))
    dk = jnp.transpose(rms_norm(dk.reshape(bsz, t_len, D_HEADS, 2, D_DIM), p['d_kn']), (0, 2, 3, 1, 4))
    dv = jnp.transpose(dv.reshape(bsz, t_len, D_HEADS, D_VDIM), (0, 2, 1, 3))
    if ctx is None:
        k_all, v_all = dk, dv
    else:
        cos, sin = axial_rope_tables(t_len, D_DIM)
        dq = apply_rope(dq, cos, sin)
        k_all = jnp.concatenate([cache_k.astype(dk.dtype), apply_rope(dk, cos, sin)], axis=3)
        v_all = jnp.concatenate([cache_v.astype(dv.dtype), dv], axis=2)
    o1 = block_attention(dq[:, :, 0:1], k_all[:, :, 0], v_all)
    o2 = block_attention(dq[:, :, 1:2], k_all[:, :, 1], v_all)
    lq1, lk1, lq2, lk2 = p['d_lambda'][0], p['d_lambda'][1], p['d_lambda'][2], p['d_lambda'][3]
    lam = jnp.exp(jnp.sum(lq1 * lk1)) - jnp.exp(jnp.sum(lq2 * lk2)) + lam_init
    od = (o1.astype(f32) - lam * o2.astype(f32))[:, :, 0]
    od = rms_norm(od, p['d_subln']) * (1 - lam_init)
    od = jnp.transpose(od, (0, 2, 1, 3)).reshape(bsz, t_len, D_V).astype(h.dtype)
    out = jnp.concatenate([oc, od], axis=-1) @ p['w_out']
    new_ctx = (dk, dv, jnp.stack(finals, axis=1)) if ctx is None else None
    return out, new_ctx


def swiglu(x, wg, wu, wd):
    return (jax.nn.silu(x @ wg) * (x @ wu)) @ wd


def routed_experts(x, idx, wts, w_gate, w_up, w_down):
    n_tok, dm = x.shape
    n_asg = n_tok * TOP_K
    flat_e = idx.reshape(-1)
    order = jnp.argsort(flat_e)
    e_sorted = flat_e[order]
    tok_sorted = order // TOP_K
    counts = jnp.bincount(flat_e, length=N_EXPERTS)
    padded = (counts + MOE_BLOCK - 1) // MOE_BLOCK * MOE_BLOCK
    pad_end = jnp.cumsum(padded)
    pad_start = pad_end - padded
    seg_start = jnp.cumsum(counts) - counts
    dest = pad_start[e_sorted] + jnp.arange(n_asg) - seg_start[e_sorted]
    n_blocks = -(-n_asg // MOE_BLOCK) + N_EXPERTS
    slot_tok = jnp.full((n_blocks * MOE_BLOCK,), n_tok, jnp.int32).at[dest].set(tok_sorted)
    block_expert = jnp.minimum(
        jnp.searchsorted(pad_end, jnp.arange(n_blocks) * MOE_BLOCK, side='right'), N_EXPERTS - 1)
    x_pad = jnp.concatenate([x, jnp.zeros((1, dm), x.dtype)], axis=0)
    xb = x_pad[slot_tok].reshape(n_blocks, MOE_BLOCK, dm)

    def expert_block(args):
        xblk, e = args
        return swiglu(xblk, w_gate[e], w_up[e], w_down[e])

    yb = lax.map(expert_block, (xb, block_expert)).reshape(-1, dm)
    y_asg = yb[dest].astype(jnp.float32) * wts.reshape(-1)[order][:, None]
    return jnp.zeros((n_tok, dm), jnp.float32).at[tok_sorted].add(y_asg).astype(x.dtype)


def moe(x, router_w, router_bias, w_gate, w_up, w_down, s_gate, s_up, s_down):
    n_tok = x.shape[0]
    scores = jax.nn.sigmoid((x @ router_w).astype(jnp.float32))
    biased = scores + router_bias.astype(jnp.float32)
    per_group = N_EXPERTS // N_GROUPS
    grp_score = jnp.sum(lax.top_k(biased.reshape(n_tok, N_GROUPS, per_group), 2)[0], axis=-1)
    _, grp_idx = lax.top_k(grp_score, TOPK_GROUPS)
    grp_sel = jnp.any(grp_idx[..., None] == jnp.arange(N_GROUPS), axis=1)
    masked = jnp.where(jnp.repeat(grp_sel, per_group, axis=1), biased, -jnp.inf)
    _, idx = lax.top_k(masked, TOP_K)
    wts = jnp.take_along_axis(scores, idx, axis=1)
    wts = wts / (jnp.sum(wts, axis=-1, keepdims=True) + 1e-20) * ROUTED_SCALE
    return routed_experts(x, idx, wts, w_gate, w_up, w_down) + swiglu(x, s_gate, s_up, s_down)


def setup_inputs(seed: int = 0) -> dict:
    key = jax.random.key(seed)
    keys = jax.random.split(key, 64)
    counter = [0]
    f32 = jnp.float32

    def nxt():
        k = keys[counter[0]]
        counter[0] += 1
        return k

    def nrm(shape, scale=1.0):
        return jax.random.normal(nxt(), shape, f32) * scale

    def unif(shape, lo, hi):
        return jax.random.uniform(nxt(), shape, f32, lo, hi)

    def gain(shape):
        return 1.0 + nrm(shape, 0.02)

    D = D_MODEL
    inp = {}
    inp['x_prompt'] = nrm((BATCH, SEQ, D))
    inp['x_sample'] = nrm((DEC_BATCH, DEC_SEQ, D))
    inp['cache_attn_k'] = nrm((DEC_BATCH, N_EVEN, A_KV_HEADS, PAST_LEN, A_DIM))
    inp['cache_attn_v'] = nrm((DEC_BATCH, N_EVEN, A_KV_HEADS, PAST_LEN, A_DIM))
    inp['state_rwkv'] = nrm((DEC_BATCH, N_EVEN, 2, B_HEADS, B_DIM, B_DIM), 0.5)
    inp['state_delta'] = nrm((DEC_BATCH, N_ODD, 2, C_HEADS, C_DIM, C_DIM), 0.1)
    inp['cache_diff_k'] = nrm((DEC_BATCH, N_ODD, D_HEADS, 2, PAST_LEN, D_DIM))
    inp['cache_diff_v'] = nrm((DEC_BATCH, N_ODD, D_HEADS, PAST_LEN, D_VDIM))
    inp['c'] = nrm((DEC_BATCH, D))
    inp['c_ctx'] = nrm((D,))
    inp['mod_w'] = nrm((DEPTH, D, 6 * D), 0.5 * D ** -0.5)
    inp['mod_b'] = nrm((DEPTH, 6 * D), 0.01)
    inp['norm1_g'] = gain((DEPTH, D))
    inp['norm2_g'] = gain((DEPTH, D))
    inp['ev_w_in'] = nrm((N_EVEN, D, EVEN_COLS), D ** -0.5)
    inp['ev_w_out'] = nrm((N_EVEN, EVEN_OUT, D), EVEN_OUT ** -0.5)
    inp['a_qn'] = gain((N_EVEN, A_DIM))
    inp['a_kn'] = gain((N_EVEN, A_DIM))
    inp['b_mu'] = unif((N_EVEN, B_COLS), 0.0, 1.0)
    inp['b_w0'] = unif((N_EVEN, 2, B_WIDTH), -6.0, 1.0)
    inp['b_w2'] = nrm((N_EVEN, 2, DECAY_LORA, B_WIDTH), 0.5 * DECAY_LORA ** -0.5)
    inp['b_a0'] = nrm((N_EVEN, 2, B_WIDTH), 0.1)
    inp['b_a2'] = nrm((N_EVEN, 2, ICLR_LORA, B_WIDTH), 0.5 * ICLR_LORA ** -0.5)
    inp['b_g2'] = nrm((N_EVEN, GATE_LORA, B_WIDTH), GATE_LORA ** -0.5)
    inp['b_kk'] = 0.85 + nrm((N_EVEN, B_WIDTH), 0.02)
    inp['b_ka'] = gain((N_EVEN, B_WIDTH))
    inp['b_rk'] = nrm((N_EVEN, B_HEADS, B_DIM), 0.1)
    inp['b_lnx_w'] = gain((N_EVEN, B_WIDTH))
    inp['b_lnx_b'] = nrm((N_EVEN, B_WIDTH), 0.01)
    inp['od_w_in'] = nrm((N_ODD, D, ODD_COLS), D ** -0.5)
    inp['od_w_out'] = nrm((N_ODD, ODD_OUT, D), ODD_OUT ** -0.5)
    inp['c_conv'] = nrm((N_ODD, CONV_W, 3 * C_WIDTH), CONV_W ** -0.5)
    inp['c_A_log'] = jnp.log(unif((N_ODD, 2, C_HEADS), 1.0, 16.0))
    dt = jnp.exp(unif((N_ODD, 2, C_HEADS), math.log(1e-3), math.log(1e-1)))
    inp['c_dt_bias'] = jnp.log(jnp.expm1(dt))
    inp['c_onorm'] = gain((N_ODD, C_DIM))
    inp['d_qn'] = gain((N_ODD, D_DIM))
    inp['d_kn'] = gain((N_ODD, D_DIM))
    inp['d_lambda'] = nrm((N_ODD, 4, D_DIM), 0.1)
    inp['d_subln'] = gain((N_ODD, D_VDIM))
    inp['router_w'] = nrm((DEPTH, D, N_EXPERTS), D ** -0.5)
    inp['router_bias'] = nrm((DEPTH, N_EXPERTS), 0.01)
    inp['exp_w_gate'] = nrm((DEPTH, N_EXPERTS, D, EXPERT_FF), D ** -0.5)
    inp['exp_w_up'] = nrm((DEPTH, N_EXPERTS, D, EXPERT_FF), D ** -0.5)
    inp['exp_w_down'] = nrm((DEPTH, N_EXPERTS, EXPERT_FF, D), EXPERT_FF ** -0.5)
    inp['sh_w_gate'] = nrm((DEPTH, D, SHARED_FF), D ** -0.5)
    inp['sh_w_up'] = nrm((DEPTH, D, SHARED_FF), D ** -0.5)
    inp['sh_w_down'] = nrm((DEPTH, SHARED_FF, D), SHARED_FF ** -0.5)
    return inp


def reference(x_prompt, x_sample, cache_attn_k, cache_attn_v, state_rwkv, state_delta, cache_diff_k,
              cache_diff_v, c, c_ctx, mod_w, mod_b, norm1_g, norm2_g, ev_w_in, ev_w_out, a_qn, a_kn, b_mu,
              b_w0, b_w2, b_a0, b_a2, b_g2, b_kk, b_ka, b_rk, b_lnx_w, b_lnx_b, od_w_in, od_w_out, c_conv,
              c_A_log, c_dt_bias, c_onorm, d_qn, d_kn, d_lambda, d_subln, router_w, router_bias, exp_w_gate,
              exp_w_up, exp_w_down, sh_w_gate, sh_w_up, sh_w_down):
    xc, xl = x_prompt, x_sample
    n_ctx_tok = xc.shape[0] * xc.shape[1]
    new_ak, new_av, new_sr, new_sd, new_dk, new_dv = [], [], [], [], [], []
    for l in range(DEPTH):
        mc = adaln(c_ctx[None], mod_w[l], mod_b[l])
        ml = adaln(c, mod_w[l], mod_b[l])
        hc = modulate(rms_norm(xc, norm1_g[l]), mc[:, None, 0], mc[:, None, 1]).astype(xc.dtype)
        hl = modulate(rms_norm(xl, norm1_g[l]), ml[:, None, 0], ml[:, None, 1]).astype(xl.dtype)
        j = l // 2
        if l % 2 == 0:
            p = {'w_in': ev_w_in[j], 'w_out': ev_w_out[j], 'a_qn': a_qn[j], 'a_kn': a_kn[j], 'b_mu': b_mu[j],
                 'b_w0': b_w0[j], 'b_w2': b_w2[j], 'b_a0': b_a0[j], 'b_a2': b_a2[j], 'b_g2': b_g2[j],
                 'b_kk': b_kk[j], 'b_ka': b_ka[j], 'b_rk': b_rk[j], 'b_lnx_w': b_lnx_w[j], 'b_lnx_b': b_lnx_b[j]}
            oc, (ak, av, sr) = even_mixer(hc, p)
            ol, _ = even_mixer(hl, p, (cache_attn_k[:, j], cache_attn_v[:, j], state_rwkv[:, j]))
            new_ak.append(ak)
            new_av.append(av)
            new_sr.append(sr)
        else:
            lam_init = 0.8 - 0.6 * math.exp(-0.3 * l)
            p = {'w_in': od_w_in[j], 'w_out': od_w_out[j], 'c_conv': c_conv[j], 'c_A_log': c_A_log[j],
                 'c_dt_bias': c_dt_bias[j], 'c_onorm': c_onorm[j], 'd_qn': d_qn[j], 'd_kn': d_kn[j],
                 'd_lambda': d_lambda[j], 'd_subln': d_subln[j]}
            oc, (dk, dv, sd) = odd_mixer(hc, p, lam_init)
            ol, _ = odd_mixer(hl, p, lam_init, (cache_diff_k[:, j], cache_diff_v[:, j], state_delta[:, j]))
            new_dk.append(dk)
            new_dv.append(dv)
            new_sd.append(sd)
        xc = (xc + mc[:, None, 2] * oc).astype(x_prompt.dtype)
        xl = (xl + ml[:, None, 2] * ol).astype(x_sample.dtype)
        hc2 = modulate(rms_norm(xc, norm2_g[l]), mc[:, None, 3], mc[:, None, 4]).astype(xc.dtype)
        hl2 = modulate(rms_norm(xl, norm2_g[l]), ml[:, None, 3], ml[:, None, 4]).astype(xl.dtype)
        tok = jnp.concatenate([hc2.reshape(-1, D_MODEL), hl2.reshape(-1, D_MODEL)], axis=0)
        f = moe(tok, router_w[l], router_bias[l], exp_w_gate[l], exp_w_up[l], exp_w_down[l],
                sh_w_gate[l], sh_w_up[l], sh_w_down[l])
        xc = (xc + mc[:, None, 5] * f[:n_ctx_tok].reshape(xc.shape)).astype(x_prompt.dtype)
        xl = (xl + ml[:, None, 5] * f[n_ctx_tok:].reshape(xl.shape)).astype(x_sample.dtype)
    st_dtype = x_prompt.dtype
    return (xc, xl, jnp.stack(new_ak, axis=1), jnp.stack(new_av, axis=1),
            jnp.stack(new_sr, axis=1).astype(st_dtype), jnp.stack(new_sd, axis=1).astype(st_dtype),
            jnp.stack(new_dk, axis=1), jnp.stack(new_dv, axis=1))
```

```python
import functools
import math
import jax
import jax.numpy as jnp
from jax import lax
from jax.experimental import pallas as pl
from jax.experimental.pallas import tpu as pltpu

D_MODEL = 1024
BATCH = 32
SEQ = 256
DEPTH = 4
DEC_BATCH = 2
DEC_SEQ = 1024
PAST_LEN = 512
GRID_W = 64
NORM_EPS = 1e-6
ROPE_THETA = 10000.0
A_HEADS = 8
A_KV_HEADS = 2
A_GROUP = A_HEADS // A_KV_HEADS
B_HEADS = 8
B_DIM = 64
B_WIDTH = B_HEADS * B_DIM
DECAY_LORA = 64
ICLR_LORA = 64
GATE_LORA = 128
B_COLS = 3 * B_WIDTH + DECAY_LORA + ICLR_LORA + GATE_LORA
LNX_EPS = 64e-5
C_HEADS = 4
C_DIM = 128
C_WIDTH = C_HEADS * C_DIM
D_HEADS = 4
D_VDIM = 128
N_EXPERTS = 64
TOP_K = 6
N_GROUPS = 8
TOPK_GROUPS = 4
PER_GROUP = N_EXPERTS // N_GROUPS
EXPERT_FF = 256
SHARED_FF = 256
ROUTED_SCALE = 1.0

HD = 64
GW = 384
CH = 64
LANE = 128
BM = 128
TM = 512
N_CTX = BATCH * SEQ
N_LAT = DEC_BATCH * DEC_SEQ
M_TOK = N_CTX + N_LAT
N_MOD = 1 + DEC_BATCH
VMEM_LIMIT = 56 * 1024 * 1024

F32 = jnp.float32
BF16 = jnp.bfloat16
HI = lax.Precision.HIGHEST
NEG = -jnp.inf


def _dot(a, b):
    return jnp.dot(a.astype(BF16), b.astype(BF16), preferred_element_type=F32)


def _dot_nt(a, b):
    return lax.dot_general(a.astype(BF16), b.astype(BF16), (((1,), (1,)), ((), ())), preferred_element_type=F32)


def _dot_tn(a, b):
    return lax.dot_general(a.astype(BF16), b.astype(BF16), (((0,), (0,)), ((), ())), preferred_element_type=F32)


def _dot_hi(a, b):
    return jnp.dot(a, b, preferred_element_type=F32, precision=HI)


def _silu(x):
    return x * jax.nn.sigmoid(x)


def _mod_group(i):
    n_ctx_tiles = N_CTX // TM
    return jnp.where(i < n_ctx_tiles, 0, 1 + (i - n_ctx_tiles) // (DEC_SEQ // TM))


def _full(shape):
    return pl.BlockSpec(shape, lambda *_: (0,) * len(shape))


def _adaln_kernel(c_ref, w_ref, b_ref, o_ref):
    o_ref[0] = _dot(_silu(c_ref[...]), w_ref[0]) + b_ref[0]


def _adaln_call(cond, mod_w, mod_b):
    n = 6
    return pl.pallas_call(
        _adaln_kernel,
        grid=(DEPTH, n),
        in_specs=[_full((8, D_MODEL)),
                  pl.BlockSpec((1, D_MODEL, D_MODEL), lambda l, j: (l, 0, j)),
                  pl.BlockSpec((1, 1, D_MODEL), lambda l, j: (l, 0, j))],
        out_specs=pl.BlockSpec((1, 8, D_MODEL), lambda l, j: (l, 0, j)),
        out_shape=jax.ShapeDtypeStruct((DEPTH, 8, n * D_MODEL), F32),
    )(cond, mod_w, mod_b.reshape(DEPTH, 1, n * D_MODEL))


def _proj_in_kernel(n_out, x_ref, mod_ref, g_ref, *rest):
    x = x_ref[...]
    y = x * lax.rsqrt(jnp.mean(x * x, axis=-1, keepdims=True) + NORM_EPS) * g_ref[...]
    h = (y * (1.0 + mod_ref[0, 0, 1:2, :]) + mod_ref[0, 0, 0:1, :]).astype(BF16)
    for w_ref, o_ref in zip(rest[:n_out], rest[n_out:]):
        o_ref[...] = jnp.dot(h, w_ref[...], preferred_element_type=F32)


def _proj_in_call(x, mod, layer, g, weights):
    n_out = len(weights)
    return pl.pallas_call(
        functools.partial(_proj_in_kernel, n_out),
        grid=(M_TOK // TM,),
        in_specs=[pl.BlockSpec((TM, D_MODEL), lambda i: (i, 0)),
                  pl.BlockSpec((1, 1, 6, D_MODEL), lambda i: (layer, _mod_group(i), 0, 0)),
                  _full((1, D_MODEL))] + [_full(w.shape) for w in weights],
        out_specs=[pl.BlockSpec((TM, w.shape[1]), lambda i: (i, 0)) for w in weights],
        out_shape=[jax.ShapeDtypeStruct((M_TOK, w.shape[1]), F32) for w in weights],
        compiler_params=pltpu.CompilerParams(vmem_limit_bytes=VMEM_LIMIT),
    )(x, mod, g.reshape(1, D_MODEL), *weights)


def _proj_out_kernel(o1_ref, o2_ref, x_ref, mod_ref, g_ref, wo_ref, rwt_ref, sg_ref, su_ref, sd_ref,
                     xn_ref, tok_ref, sh_ref, lg_ref):
    o = jnp.concatenate([o1_ref[...], o2_ref[...]], axis=1).astype(BF16)
    xn = x_ref[...] + mod_ref[0, 0, 2:3, :] * jnp.dot(o, wo_ref[...], preferred_element_type=F32)
    xn_ref[...] = xn
    y = xn * lax.rsqrt(jnp.mean(xn * xn, axis=-1, keepdims=True) + NORM_EPS) * g_ref[...]
    h2 = y * (1.0 + mod_ref[0, 0, 4:5, :]) + mod_ref[0, 0, 3:4, :]
    tok_ref[...] = h2
    hb = h2.astype(BF16)
    lg_ref[...] = _dot_nt(rwt_ref[...], hb)
    hid = _silu(jnp.dot(hb, sg_ref[...], preferred_element_type=F32)) * jnp.dot(hb, su_ref[...], preferred_element_type=F32)
    sh_ref[...] = jnp.dot(hid.astype(BF16), sd_ref[...], preferred_element_type=F32)


def _proj_out_call(o1, o2, x, mod, layer, g2, w_out, rw_t, sg, su, sd):
    row = lambda n: pl.BlockSpec((TM, n), lambda i: (i, 0))
    return pl.pallas_call(
        _proj_out_kernel,
        grid=(M_TOK // TM,),
        in_specs=[row(o1.shape[1]), row(o2.shape[1]), row(D_MODEL),
                  pl.BlockSpec((1, 1, 6, D_MODEL), lambda i: (layer, _mod_group(i), 0, 0)),
                  _full((1, D_MODEL)), _full(w_out.shape), _full(rw_t.shape), _full(sg.shape), _full(su.shape),
                  _full(sd.shape)],
        out_specs=[row(D_MODEL), row(D_MODEL), row(D_MODEL), pl.BlockSpec((N_EXPERTS, TM), lambda i: (0, i))],
        out_shape=[jax.ShapeDtypeStruct((M_TOK, D_MODEL), F32)] * 3 + [jax.ShapeDtypeStruct((N_EXPERTS, M_TOK), F32)],
        compiler_params=pltpu.CompilerParams(vmem_limit_bytes=VMEM_LIMIT),
    )(o1, o2, x, mod, g2.reshape(1, D_MODEL), w_out, rw_t, sg, su, sd)


def _rope_lane_tables(n_tok):
    rows = n_tok // GRID_W
    row = jnp.repeat(jnp.arange(rows, dtype=F32), GRID_W)
    col = jnp.tile(jnp.arange(GRID_W, dtype=F32), rows)
    n_freq = HD // 4
    inv = ROPE_THETA ** (-jnp.arange(n_freq, dtype=F32) / n_freq)
    ang = jnp.concatenate([row[:, None] * inv, col[:, None] * inv], axis=-1)
    cos = jnp.repeat(jnp.cos(ang), 2, axis=-1)
    sin = jnp.repeat(jnp.sin(ang), 2, axis=-1) * jnp.tile(jnp.array([-1.0, 1.0], F32), HD // 2)
    return cos, sin


def _seg_matrix(width, seg, value):
    li = lax.broadcasted_iota(jnp.int32, (width, width), 0) // seg
    lj = lax.broadcasted_iota(jnp.int32, (width, width), 1) // seg
    return jnp.where(li == lj, value, 0.0).astype(F32)


def _norm_rope(x, gain, n_norm, cos, sin):
    lane = lax.broadcasted_iota(jnp.int32, x.shape, 1)
    ms = _dot_hi(x * x, _seg_matrix(x.shape[1], HD, 1.0 / HD))
    xn = jnp.where(lane < n_norm, x * lax.rsqrt(ms + NORM_EPS) * gain, x)
    if cos is None:
        return xn, xn
    w = x.shape[1]
    swapped = jnp.where(lane % 2 == 0, pltpu.roll(xn, w - 1, 1), pltpu.roll(xn, 1, 1))
    return xn, xn * cos + swapped * sin


def _softmax_pv(s, v):
    m = jnp.max(s, axis=-1, keepdims=True)
    p = jnp.exp(s - m)
    l = jnp.sum(p, axis=-1, keepdims=True)
    return _dot(p, v) / l


def _attn_a_kernel(T, P, TQ, *refs):
    if P:
        x_ref, gain_ref, cos_ref, sin_ref, ck_ref, cv_ref, o_ref, xr_s = refs
        xn, xr = _norm_rope(x_ref[...], gain_ref[...], 5 * HD, cos_ref[...], sin_ref[...])
    else:
        x_ref, gain_ref, o_ref, nk_ref, nv_ref, xr_s = refs
        xn, xr = _norm_rope(x_ref[...], gain_ref[...], 5 * HD, None, None)
        nk_ref[0, 0] = xn[:, 4 * HD:5 * HD]
        nv_ref[0, 0] = xn[:, 5 * HD:6 * HD]
    xr_s[...] = xr
    k_new = xr[:, 4 * HD:5 * HD]
    v_new = xr[:, 5 * HD:6 * HD]
    if P:
        k_all = jnp.concatenate([ck_ref[0, 0, 0], k_new], axis=0).astype(BF16)
        v_all = jnp.concatenate([cv_ref[0, 0, 0], v_new], axis=0).astype(BF16)
    else:
        k_all, v_all = k_new.astype(BF16), v_new.astype(BF16)

    def q_block(qb, carry):
        rows = pl.ds(pl.multiple_of(qb * TQ, TQ), TQ)
        qx = xr_s[rows, 0:A_GROUP * HD]
        qs = jnp.concatenate([qx[:, i * HD:(i + 1) * HD] for i in range(A_GROUP)], axis=0)
        o = _softmax_pv(_dot_nt(qs, k_all) * (HD ** -0.5), v_all)
        o_ref[rows, :] = jnp.concatenate([o[i * TQ:(i + 1) * TQ, :] for i in range(A_GROUP)], axis=1)
        return carry

    lax.fori_loop(0, T // TQ, q_block, 0)


def _attn_a_call(amat, row_off_blocks, n_seq, T, gain, cache=None):
    P = 0 if cache is None else cache[0].shape[3]
    TQ = T if P == 0 else 128
    in_specs = [pl.BlockSpec((T, GW), lambda i, g: (row_off_blocks + i, g)), _full((1, GW))]
    args = [amat, gain]
    out_specs = [pl.BlockSpec((T, A_GROUP * HD), lambda i, g: (i, g))]
    out_shape = [jax.ShapeDtypeStruct((n_seq * T, A_HEADS * HD), F32)]
    if P:
        ck, cv, j = cache
        cos, sin = _rope_lane_tables(T)
        cos_g = jnp.concatenate([cos] * 5 + [jnp.ones((T, HD), F32)], axis=1)
        sin_g = jnp.concatenate([sin] * 5 + [jnp.zeros((T, HD), F32)], axis=1)
        in_specs += [_full((T, GW)), _full((T, GW)),
                     pl.BlockSpec((1, 1, 1, P, HD), lambda i, g: (i, j, g, 0, 0)),
                     pl.BlockSpec((1, 1, 1, P, HD), lambda i, g: (i, j, g, 0, 0))]
        args += [cos_g, sin_g, ck, cv]
    else:
        out_specs += [pl.BlockSpec((1, 1, T, HD), lambda i, g: (i, g, 0, 0))] * 2
        out_shape += [jax.ShapeDtypeStruct((n_seq, A_KV_HEADS, T, HD), F32)] * 2
    return pl.pallas_call(
        functools.partial(_attn_a_kernel, T, P, TQ),
        grid=(n_seq, A_KV_HEADS),
        in_specs=in_specs, out_specs=out_specs, out_shape=out_shape,
        scratch_shapes=[pltpu.VMEM((T, GW), F32)],
        compiler_params=pltpu.CompilerParams(vmem_limit_bytes=VMEM_LIMIT),
    )(*args)


def _attn_d_kernel(T, P, TQ, lam_init, *refs):
    if P:
        x_ref, gain_ref, lam_ref, sub_ref, cos_ref, sin_ref, ck_ref, cv_ref, o_ref, xr_s = refs
        xn, xr = _norm_rope(x_ref[...], gain_ref[...], 4 * HD, cos_ref[...], sin_ref[...])
    else:
        x_ref, gain_ref, lam_ref, sub_ref, o_ref, nk_ref, nv_ref, xr_s = refs
        xn, xr = _norm_rope(x_ref[...], gain_ref[...], 4 * HD, None, None)
        nk_ref[0, 0, 0] = xn[:, 2 * HD:3 * HD]
        nk_ref[0, 0, 1] = xn[:, 3 * HD:4 * HD]
        nv_ref[0, 0] = xn[:, 4 * HD:]
    xr_s[...] = xr
    v_new = xr[:, 4 * HD:]
    ks = []
    for m in range(2):
        k_new = xr[:, (2 + m) * HD:(3 + m) * HD]
        if P:
            ks.append(jnp.concatenate([ck_ref[0, 0, 0, m], k_new], axis=0).astype(BF16))
        else:
            ks.append(k_new.astype(BF16))
    v_all = (jnp.concatenate([cv_ref[0, 0, 0], v_new], axis=0) if P else v_new).astype(BF16)
    lm = lam_ref[...]
    lam = (jnp.exp(jnp.sum(lm[0:1, :] * lm[1:2, :], axis=-1, keepdims=True))
           - jnp.exp(jnp.sum(lm[2:3, :] * lm[3:4, :], axis=-1, keepdims=True)) + lam_init)

    def q_block(qb, carry):
        rows = pl.ds(pl.multiple_of(qb * TQ, TQ), TQ)
        qx = xr_s[rows, 0:2 * HD]
        o1 = _softmax_pv(_dot_nt(qx[:, 0:HD], ks[0]) * (HD ** -0.5), v_all)
        o2 = _softmax_pv(_dot_nt(qx[:, HD:], ks[1]) * (HD ** -0.5), v_all)
        od = o1 - lam * o2
        od = od * lax.rsqrt(jnp.mean(od * od, axis=-1, keepdims=True) + NORM_EPS) * sub_ref[...]
        o_ref[rows, :] = od * (1.0 - lam_init)
        return carry

    lax.fori_loop(0, T // TQ, q_block, 0)


def _attn_d_call(dmat, row_off_blocks, n_seq, T, gain, lam, subln, lam_init, cache=None):
    P = 0 if cache is None else cache[0].shape[4]
    TQ = T if P == 0 else 256
    in_specs = [pl.BlockSpec((T, GW), lambda i, h: (row_off_blocks + i, h)),
                _full((1, GW)), _full((4, HD)), _full((1, D_VDIM))]
    args = [dmat, gain, lam, subln.reshape(1, -1)]
    out_specs = [pl.BlockSpec((T, D_VDIM), lambda i, h: (i, h))]
    out_shape = [jax.ShapeDtypeStruct((n_seq * T, D_HEADS * D_VDIM), F32)]
    if P:
        ck, cv, j = cache
        cos, sin = _rope_lane_tables(T)
        cos_g = jnp.concatenate([cos] * 4 + [jnp.ones((T, D_VDIM), F32)], axis=1)
        sin_g = jnp.concatenate([sin] * 4 + [jnp.zeros((T, D_VDIM), F32)], axis=1)
        in_specs += [_full((T, GW)), _full((T, GW)),
                     pl.BlockSpec((1, 1, 1, 2, P, HD), lambda i, h: (i, j, h, 0, 0, 0)),
                     pl.BlockSpec((1, 1, 1, P, D_VDIM), lambda i, h: (i, j, h, 0, 0))]
        args += [cos_g, sin_g, ck, cv]
    else:
        out_specs += [pl.BlockSpec((1, 1, 2, T, HD), lambda i, h: (i, h, 0, 0, 0)),
                      pl.BlockSpec((1, 1, T, D_VDIM), lambda i, h: (i, h, 0, 0))]
        out_shape += [jax.ShapeDtypeStruct((n_seq, D_HEADS, 2, T, HD), F32),
                      jax.ShapeDtypeStruct((n_seq, D_HEADS, T, D_VDIM), F32)]
    return pl.pallas_call(
        functools.partial(_attn_d_kernel, T, P, TQ, lam_init),
        grid=(n_seq, D_HEADS),
        in_specs=in_specs, out_specs=out_specs, out_shape=out_shape,
        scratch_shapes=[pltpu.VMEM((T, GW), F32)],
        compiler_params=pltpu.CompilerParams(vmem_limit_bytes=VMEM_LIMIT),
    )(*args)


def _chunk_masks(d):
    ci = lax.broadcasted_iota(jnp.int32, (CH, CH), 0)
    cj = lax.broadcasted_iota(jnp.int32, (CH, CH), 1)
    return (ci >= cj, ci > cj) if d == 0 else (ci <= cj, ci < cj)


def _unit_lower_solve(am, xm):
    n_stage = CH.bit_length() - 1
    for s in range(n_stage):
        xm = xm + _dot_hi(am, xm)
        if s < n_stage - 1:
            am = _dot_hi(am, am)
    return xm


def _shifted_rows(x, T):
    row = lax.broadcasted_iota(jnp.int32, (T, 1), 0)
    prev = jnp.where(row == 0, 0.0, pltpu.roll(x, 1, 0))
    nxt = jnp.where(row == T - 1, 0.0, pltpu.roll(x, T - 1, 0))
    return prev, nxt


def _rwkv_kernel(T, has_state, *refs):
    if has_state:
        (xb_ref, mu_ref, w0_ref, w2_ref, a0_ref, a2_ref, g2_ref, kk_ref, ka_ref, rk_ref, lnw_ref, lnb_ref, s0_ref,
         ob_ref, sf_ref, r_s, v_s, av_s, lw_s, kd_s, bv_s, y_s, st_s) = refs
    else:
        (xb_ref, mu_ref, w0_ref, w2_ref, a0_ref, a2_ref, g2_ref, kk_ref, ka_ref, rk_ref, lnw_ref, lnb_ref,
         ob_ref, sf_ref, r_s, v_s, av_s, lw_s, kd_s, bv_s, y_s, st_s) = refs
    n_chunks = T // CH
    x = xb_ref[...]
    prev, nxt = _shifted_rows(x, T)
    xs = x + mu_ref[...] * (0.5 * (prev + nxt) - x)
    r = xs[:, 0:B_WIDTH]
    kb = xs[:, B_WIDTH:2 * B_WIDTH]
    vb = xs[:, 2 * B_WIDTH:3 * B_WIDTH]
    wd = xs[:, 3 * B_WIDTH:3 * B_WIDTH + DECAY_LORA]
    ad = xs[:, 3 * B_WIDTH + DECAY_LORA:3 * B_WIDTH + DECAY_LORA + ICLR_LORA]
    gd = xs[:, 3 * B_WIDTH + DECAY_LORA + ICLR_LORA:]
    seg = _seg_matrix(B_WIDTH, B_DIM, 1.0)
    kk = kb * kk_ref[...]
    kkn = kk * lax.rsqrt(_dot_hi(kk * kk, seg) + 1e-6)
    r_s[...] = r
    v_s[...] = vb
    av_s[...] = -kkn
    twd = jnp.tanh(wd)
    for d in range(2):
        wl = w0_ref[d:d + 1, :] + _dot(twd, w2_ref[d])
        w_log = -jax.nn.softplus(-wl) - 0.5
        lw_s[d] = -jnp.exp(w_log)
        a = jax.nn.sigmoid(a0_ref[d:d + 1, :] + _dot(ad, a2_ref[d]))
        kd_s[d] = kb * (1.0 + (a - 1.0) * ka_ref[...])
        bv_s[d] = kkn * a
    if has_state:
        st_s[...] = s0_ref[0]
    else:
        st_s[...] = jnp.zeros_like(st_s)
    ci2 = lax.broadcasted_iota(jnp.int32, (CH, 2 * CH), 0)
    cj2 = lax.broadcasted_iota(jnp.int32, (CH, 2 * CH), 1) % CH
    for d in range(2):
        incl, strict = _chunk_masks(d)
        incl2 = ci2 >= cj2 if d == 0 else ci2 <= cj2
        tri = jnp.where(incl, 1.0, 0.0).astype(F32)
        last = CH - 1 if d == 0 else 0

        def chunk_body(it, carry, d=d, incl2=incl2, strict=strict, tri=tri, last=last):
            c = it if d == 0 else n_chunks - 1 - it
            rows = pl.ds(pl.multiple_of(c * CH, CH), CH)
            lwc = lw_s[d, rows, :]
            cum = _dot_hi(tri, lwc)
            e_pos = jnp.exp(cum)
            e_neg = jnp.exp(-cum)
            rt = r_s[rows, :] * e_pos
            at = av_s[rows, :] * jnp.exp(cum - lwc)
            bh = bv_s[d, rows, :] * e_neg
            kh = kd_s[d, rows, :] * e_neg
            vc = v_s[rows, :]
            pc = e_pos[last:last + 1, :]
            ys = []
            for h in range(B_HEADS):
                sl = slice(h * B_DIM, (h + 1) * B_DIM)
                at_h, rt_h, v_h = at[:, sl], rt[:, sl], vc[:, sl]
                bk_h = jnp.concatenate([bh[:, sl], kh[:, sl]], axis=0)
                g = _dot_nt(jnp.concatenate([at_h, rt_h], axis=0), bk_h)
                a_ab = jnp.where(strict, g[0:CH, 0:CH], 0.0)
                a_ak = jnp.where(strict, g[0:CH, CH:], 0.0)
                a_r = jnp.where(incl2, g[CH:, :], 0.0)
                xm = _unit_lower_solve(a_ab, jnp.concatenate([at_h, _dot(a_ak, v_h)], axis=1))
                s_old = st_s[d, h]
                ws = _dot_nt(jnp.concatenate([xm[:, 0:B_DIM], rt_h], axis=0), s_old)
                uv = jnp.concatenate([ws[0:CH, :] + xm[:, B_DIM:], v_h], axis=0)
                ys.append(ws[CH:, :] + _dot(a_r, uv))
                st_s[d, h] = (s_old + _dot_tn(uv, bk_h)) * pc[:, sl]
            yc = jnp.concatenate(ys, axis=1)
            if d == 0:
                y_s[rows, :] = yc
            else:
                y_s[rows, :] = y_s[rows, :] + yc
            return carry

        lax.fori_loop(0, n_chunks, chunk_body, 0)
    y = y_s[...]
    yc = y - _dot_hi(y, seg) * (1.0 / B_DIM)
    var = _dot_hi(yc * yc, seg) * (1.0 / B_DIM)
    yn = yc * lax.rsqrt(var + LNX_EPS) * lnw_ref[...] + lnb_ref[...]
    yn = yn + _dot_hi(r * kb * rk_ref[...], seg) * vb
    ob_ref[...] = yn * _dot(jax.nn.sigmoid(gd), g2_ref[...])
    sf_ref[0] = st_s[...]


def _rwkv_call(xb, row_off_blocks, n_seq, T, p, s0=None):
    has_state = s0 is not None
    st_shape = (2, B_HEADS, B_DIM, B_DIM)
    in_specs = [pl.BlockSpec((T, B_COLS), lambda i: (row_off_blocks + i, 0)),
                _full((1, B_COLS)), _full((2, B_WIDTH)), _full((2, DECAY_LORA, B_WIDTH)), _full((2, B_WIDTH)),
                _full((2, ICLR_LORA, B_WIDTH)), _full((GATE_LORA, B_WIDTH))] + [_full((1, B_WIDTH))] * 5
    args = [xb, p['b_mu'].reshape(1, -1), p['b_w0'], p['b_w2'], p['b_a0'], p['b_a2'], p['b_g2'],
            p['b_kk'].reshape(1, -1), p['b_ka'].reshape(1, -1), p['b_rk'].reshape(1, -1),
            p['b_lnx_w'].reshape(1, -1), p['b_lnx_b'].reshape(1, -1)]
    if has_state:
        in_specs.append(pl.BlockSpec((1,) + st_shape, lambda i: (i, 0, 0, 0, 0)))
        args.append(s0)
    scr = [pltpu.VMEM((T, B_WIDTH), F32)] * 3 + [pltpu.VMEM((2, T, B_WIDTH), F32)] * 3 + \
          [pltpu.VMEM((T, B_WIDTH), F32), pltpu.VMEM(st_shape, F32)]
    return pl.pallas_call(
        functools.partial(_rwkv_kernel, T, has_state),
        grid=(n_seq,),
        in_specs=in_specs,
        out_specs=[pl.BlockSpec((T, B_WIDTH), lambda i: (i, 0)),
                   pl.BlockSpec((1,) + st_shape, lambda i: (i, 0, 0, 0, 0))],
        out_shape=[jax.ShapeDtypeStruct((n_seq * T, B_WIDTH), F32),
                   jax.ShapeDtypeStruct((n_seq,) + st_shape, F32)],
        scratch_shapes=scr,
        compiler_params=pltpu.CompilerParams(vmem_limit_bytes=VMEM_LIMIT),
    )(*args)


def _delta_kernel(T, has_state, *refs):
    if has_state:
        (c_ref, ab_ref, conv_ref, arow_ref, dtrow_ref, on_ref, s0_ref, oc_ref, sf_ref,
         q_s, k_s, v_s, g_s, b_s, o_s, st_s) = refs
    else:
        (c_ref, ab_ref, conv_ref, arow_ref, dtrow_ref, on_ref, oc_ref, sf_ref,
         q_s, k_s, v_s, g_s, b_s, o_s, st_s) = refs
    n_chunks = T // CH
    x = c_ref[:, 0:3 * C_WIDTH]
    prev, nxt = _shifted_rows(x, T)
    xc = _silu(conv_ref[0:1, :] * prev + conv_ref[1:2, :] * x + conv_ref[2:3, :] * nxt)
    for h in range(C_HEADS):
        sl = slice(h * C_DIM, (h + 1) * C_DIM)
        qh = xc[:, h * C_DIM:(h + 1) * C_DIM]
        kh = xc[:, C_WIDTH + h * C_DIM:C_WIDTH + (h + 1) * C_DIM]
        q_s[:, sl] = qh * lax.rsqrt(jnp.sum(qh * qh, axis=-1, keepdims=True) + 1e-6) * (C_DIM ** -0.5)
        k_s[:, sl] = kh * lax.rsqrt(jnp.sum(kh * kh, axis=-1, keepdims=True) + 1e-6)
    v_s[...] = xc[:, 2 * C_WIDTH:]
    ab = ab_ref[...]
    g_s[...] = arow_ref[...] * jax.nn.softplus(ab + dtrow_ref[...])
    b_s[...] = jax.nn.sigmoid(ab)
    if has_state:
        st_s[...] = s0_ref[0]
    else:
        st_s[...] = jnp.zeros_like(st_s)
    ci = lax.broadcasted_iota(jnp.int32, (CH, CH), 0)
    cj = lax.broadcasted_iota(jnp.int32, (CH, CH), 1)
    eye = jnp.where(ci == cj, 1.0, 0.0).astype(F32)
    for d in range(2):
        incl, strict = _chunk_masks(d)
        tri = jnp.where(incl, 1.0, 0.0).astype(F32)
        last = CH - 1 if d == 0 else 0

        def chunk_body(it, carry, d=d, incl=incl, strict=strict, tri=tri, last=last):
            c = it if d == 0 else n_chunks - 1 - it
            rows = pl.ds(pl.multiple_of(c * CH, CH), CH)
            gcum = _dot_hi(tri, g_s[rows, :])
            bet = b_s[rows, :]
            qc, kc, vc = q_s[rows, :], k_s[rows, :], v_s[rows, :]
            outs = []
            for h in range(C_HEADS):
                j = d * C_HEADS + h
                sl = slice(h * C_DIM, (h + 1) * C_DIM)
                g_col = gcum[:, j:j + 1]
                b_col = bet[:, 2 * C_HEADS + j:2 * C_HEADS + j + 1]
                g_row = jnp.sum(eye * g_col, axis=0, keepdims=True)
                decay = jnp.where(incl, jnp.exp(jnp.where(incl, g_col - g_row, 0.0)), 0.0)
                q_h, k_h, v_h = qc[:, sl], kc[:, sl], vc[:, sl]
                kbeta = k_h * b_col
                gm = _dot_nt(jnp.concatenate([kbeta, q_h], axis=0), k_h)
                attn = gm[CH:, :] * decay
                eg = jnp.exp(g_col)
                xm = _unit_lower_solve(-jnp.where(strict, gm[0:CH, :] * decay, 0.0),
                                       jnp.concatenate([v_h * b_col, kbeta * eg], axis=1))
                g_last = g_col[last:last + 1, :]
                s_old = st_s[d, h]
                ws = _dot(jnp.concatenate([xm[:, C_DIM:], q_h * eg], axis=0), s_old)
                v_new = xm[:, 0:C_DIM] - ws[0:CH, :]
                outs.append(ws[CH:, :] + _dot(attn, v_new))
                st_s[d, h] = s_old * jnp.exp(g_last) + _dot_tn(k_h * jnp.exp(g_last - g_col), v_new)
            oc = jnp.concatenate(outs, axis=1)
            if d == 0:
                o_s[rows, :] = oc
            else:
                o_s[rows, :] = o_s[rows, :] + oc
            return carry

        lax.fori_loop(0, n_chunks, chunk_body, 0)
    z = c_ref[:, 3 * C_WIDTH:]
    for h in range(C_HEADS):
        sl = slice(h * C_DIM, (h + 1) * C_DIM)
        oh = o_s[:, sl]
        on = oh * lax.rsqrt(jnp.mean(oh * oh, axis=-1, keepdims=True) + NORM_EPS) * on_ref[...]
        oc_ref[:, sl] = on * _silu(z[:, sl])
    sf_ref[0] = st_s[...]


def _delta_call(cmat, abmat, row_off_blocks, n_seq, T, p, s0=None):
    has_state = s0 is not None
    st_shape = (2, C_HEADS, C_DIM, C_DIM)
    arow = jnp.zeros((1, LANE), F32).at[0, 0:2 * C_HEADS].set(-jnp.exp(p['c_A_log'].reshape(-1)))
    dtrow = jnp.zeros((1, LANE), F32).at[0, 0:2 * C_HEADS].set(p['c_dt_bias'].reshape(-1))
    in_specs = [pl.BlockSpec((T, 4 * C_WIDTH), lambda i: (row_off_blocks + i, 0)),
                pl.BlockSpec((T, LANE), lambda i: (row_off_blocks + i, 0)),
                _full((3, 3 * C_WIDTH)), _full((1, LANE)), _full((1, LANE)), _full((1, C_DIM))]
    args = [cmat, abmat, p['c_conv'], arow, dtrow, p['c_onorm'].reshape(1, -1)]
    if has_state:
        in_specs.append(pl.BlockSpec((1,) + st_shape, lambda i: (i, 0, 0, 0, 0)))
        args.append(s0)
    scr = [pltpu.VMEM((T, C_WIDTH), F32)] * 3 + [pltpu.VMEM((T, LANE), F32)] * 2 + \
          [pltpu.VMEM((T, C_WIDTH), F32), pltpu.VMEM(st_shape, F32)]
    return pl.pallas_call(
        functools.partial(_delta_kernel, T, has_state),
        grid=(n_seq,),
        in_specs=in_specs,
        out_specs=[pl.BlockSpec((T, C_WIDTH), lambda i: (i, 0)),
                   pl.BlockSpec((1,) + st_shape, lambda i: (i, 0, 0, 0, 0))],
        out_shape=[jax.ShapeDtypeStruct((n_seq * T, C_WIDTH), F32),
                   jax.ShapeDtypeStruct((n_seq,) + st_shape, F32)],
        scratch_shapes=scr,
        compiler_params=pltpu.CompilerParams(vmem_limit_bytes=VMEM_LIMIT),
    )(*args)


def _first_max(x, iota, size):
    m = jnp.max(x, axis=0, keepdims=True)
    idx = jnp.min(jnp.where(x == m, iota, size), axis=0, keepdims=True)
    return m, idx


def _route_kernel(lg_ref, bias_ref, idx_ref, wts_ref):
    n = lg_ref.shape[1]
    scores = jax.nn.sigmoid(lg_ref[...])
    biased = scores + bias_ref[...]
    e_iota = lax.broadcasted_iota(jnp.int32, (N_EXPERTS, n), 0)
    g_iota = lax.broadcasted_iota(jnp.int32, (PER_GROUP, n), 0)
    gs = []
    for g in range(N_GROUPS):
        xg = biased[g * PER_GROUP:(g + 1) * PER_GROUP, :]
        m1, i1 = _first_max(xg, g_iota, PER_GROUP)
        m2 = jnp.max(jnp.where(g_iota == i1, NEG, xg), axis=0, keepdims=True)
        gs.append(m1 + m2)
    gscore = jnp.concatenate(gs, axis=0)
    gi = lax.broadcasted_iota(jnp.int32, (N_GROUPS, n), 0)
    gsel = jnp.zeros((N_GROUPS, n), F32)
    for _ in range(TOPK_GROUPS):
        _, ig = _first_max(gscore, gi, N_GROUPS)
        hit = gi == ig
        gsel = jnp.where(hit, 1.0, gsel)
        gscore = jnp.where(hit, NEG, gscore)
    masked = jnp.concatenate(
        [jnp.where(gsel[g:g + 1, :] > 0.0, biased[g * PER_GROUP:(g + 1) * PER_GROUP, :], NEG)
         for g in range(N_GROUPS)], axis=0)
    ids, ws = [], []
    for _ in range(TOP_K):
        _, ie = _first_max(masked, e_iota, N_EXPERTS)
        hit = e_iota == ie
        ids.append(ie)
        ws.append(jnp.sum(jnp.where(hit, scores, 0.0), axis=0, keepdims=True))
        masked = jnp.where(hit, NEG, masked)
    wsum = ws[0]
    for w in ws[1:]:
        wsum = wsum + w
    inv = ROUTED_SCALE / (wsum + 1e-20)
    idx_ref[...] = jnp.concatenate(ids + [jnp.zeros((8 - TOP_K, n), jnp.int32)], axis=0)
    wts_ref[...] = jnp.concatenate([w * inv for w in ws] + [jnp.zeros((8 - TOP_K, n), F32)], axis=0)


def _route_call(logits_t, bias):
    m = logits_t.shape[1]
    return pl.pallas_call(
        _route_kernel,
        grid=(m // TM,),
        in_specs=[pl.BlockSpec((N_EXPERTS, TM), lambda i: (0, i)), _full((N_EXPERTS, 1))],
        out_specs=[pl.BlockSpec((8, TM), lambda i: (0, i))] * 2,
        out_shape=[jax.ShapeDtypeStruct((8, m), jnp.int32), jax.ShapeDtypeStruct((8, m), F32)],
    )(logits_t, bias.reshape(N_EXPERTS, 1))


def _dispatch_tables(idx, m):
    n_asg = m * TOP_K
    nb = -(-n_asg // BM) + N_EXPERTS
    flat_e = idx[0:TOP_K, :].T.reshape(-1)
    order = jnp.argsort(flat_e).astype(jnp.int32)
    counts = jnp.bincount(flat_e, length=N_EXPERTS).astype(jnp.int32)
    padded = (counts + BM - 1) // BM * BM
    pad_end = jnp.cumsum(padded)
    pad_start = pad_end - padded
    seg_start = jnp.cumsum(counts) - counts
    block_expert = jnp.minimum(jnp.searchsorted(pad_end, jnp.arange(nb, dtype=jnp.int32) * BM, side='right'),
                               N_EXPERTS - 1).astype(jnp.int32)
    s = jnp.arange(nb * BM, dtype=jnp.int32)
    e_s = block_expert[s // BM]
    r = s - pad_start[e_s]
    valid = r < counts[e_s]
    asg = order[jnp.clip(seg_start[e_s] + r, 0, n_asg - 1)]
    tok = asg // TOP_K
    slot_tok = jnp.where(valid, tok, 0).astype(jnp.int32)
    slot_dst = jnp.where(valid, (asg % TOP_K) * m + tok, TOP_K * m + s % BM).astype(jnp.int32)
    n_used = (pad_end[-1] // BM).astype(jnp.int32).reshape(1)
    return slot_tok.reshape(nb, 1, BM), slot_dst.reshape(nb, 1, BM), block_expert, n_used


def _expert_kernel(be_ref, nu_ref, st_ref, sd_ref, tok_hbm, wg_ref, wu_ref, wd_ref, y_hbm, xbuf, ybuf, sem_in, sem_out):
    @pl.when(pl.program_id(0) == 0)
    def _():
        ybuf[...] = jnp.zeros_like(ybuf)
        spare = pltpu.make_async_copy(ybuf, y_hbm.at[pl.ds(y_hbm.shape[0] - BM, BM), :], sem_out)
        spare.start()
        spare.wait()

    @pl.when(pl.program_id(0) < nu_ref[0])
    def _():
        def issue_in(r, c):
            pltpu.make_async_copy(tok_hbm.at[pl.ds(st_ref[0, 0, r], 1), :], xbuf.at[pl.ds(r, 1), :], sem_in).start()
            return c

        lax.fori_loop(0, BM, issue_in, 0)
        pltpu.make_async_copy(tok_hbm.at[pl.ds(0, BM), :], xbuf, sem_in).wait()
        x = xbuf[...].astype(BF16)
        g = jnp.dot(x, wg_ref[0, 0].astype(BF16), preferred_element_type=F32)
        u = jnp.dot(x, wu_ref[0, 0].astype(BF16), preferred_element_type=F32)
        ybuf[...] = jnp.dot((_silu(g) * u).astype(BF16), wd_ref[0, 0].astype(BF16), preferred_element_type=F32)

        def issue_out(r, c):
            pltpu.make_async_copy(ybuf.at[pl.ds(r, 1), :], y_hbm.at[pl.ds(sd_ref[0, 0, r], 1), :], sem_out).start()
            return c

        lax.fori_loop(0, BM, issue_out, 0)
        pltpu.make_async_copy(ybuf, y_hbm.at[pl.ds(0, BM), :], sem_out).wait()


def _expert_call(tok, tables, layer, wg, wu, wd):
    m = tok.shape[0]
    slot_tok, slot_dst, block_expert, n_used = tables
    nb = slot_tok.shape[0]
    smem_blk = lambda: pl.BlockSpec((1, 1, BM), lambda i, be, nu: (i, 0, 0), memory_space=pltpu.SMEM)
    grid_spec = pltpu.PrefetchScalarGridSpec(
        num_scalar_prefetch=2,
        grid=(nb,),
        in_specs=[smem_blk(), smem_blk(),
                  pl.BlockSpec(memory_space=pl.ANY),
                  pl.BlockSpec((1, 1, D_MODEL, EXPERT_FF), lambda i, be, nu: (layer, be[i], 0, 0)),
                  pl.BlockSpec((1, 1, D_MODEL, EXPERT_FF), lambda i, be, nu: (layer, be[i], 0, 0)),
                  pl.BlockSpec((1, 1, EXPERT_FF, D_MODEL), lambda i, be, nu: (layer, be[i], 0, 0))],
        out_specs=pl.BlockSpec(memory_space=pl.ANY),
        scratch_shapes=[pltpu.VMEM((BM, D_MODEL), F32), pltpu.VMEM((BM, D_MODEL), F32),
                        pltpu.SemaphoreType.DMA(()), pltpu.SemaphoreType.DMA(())])
    return pl.pallas_call(
        _expert_kernel, grid_spec=grid_spec,
        out_shape=jax.ShapeDtypeStruct((TOP_K * m + BM, D_MODEL), F32),
        compiler_params=pltpu.CompilerParams(vmem_limit_bytes=VMEM_LIMIT),
    )(block_expert, n_used, slot_tok, slot_dst, tok, wg, wu, wd)


def _combine_kernel(x_ref, sh_ref, w_ref, mod_ref, *rest):
    y_refs, o_ref = rest[:TOP_K], rest[TOP_K]
    w = w_ref[...]
    acc = sh_ref[...]
    for k, y_ref in enumerate(y_refs):
        acc = acc + w[:, k:k + 1] * y_ref[...]
    o_ref[...] = x_ref[...] + mod_ref[0, 0, 5:6, :] * acc


def _combine_call(x, sh, wts_rows, mod, layer, yrows):
    nt = M_TOK // TM
    row = pl.BlockSpec((TM, D_MODEL), lambda i: (i, 0))
    ysp = [pl.BlockSpec((TM, D_MODEL), (lambda i, k=k: (k * nt + i, 0))) for k in range(TOP_K)]
    return pl.pallas_call(
        _combine_kernel,
        grid=(nt,),
        in_specs=[row, row, pl.BlockSpec((TM, 8), lambda i: (i, 0)),
                  pl.BlockSpec((1, 1, 6, D_MODEL), lambda i: (layer, _mod_group(i), 0, 0))] + ysp,
        out_specs=row,
        out_shape=jax.ShapeDtypeStruct((M_TOK, D_MODEL), F32),
        compiler_params=pltpu.CompilerParams(vmem_limit_bytes=VMEM_LIMIT),
    )(x, sh, wts_rows, mod, *([yrows] * TOP_K))


def kernel(x_prompt, x_sample, cache_attn_k, cache_attn_v, state_rwkv, state_delta, cache_diff_k,
           cache_diff_v, c, c_ctx, mod_w, mod_b, norm1_g, norm2_g, ev_w_in, ev_w_out, a_qn, a_kn, b_mu,
           b_w0, b_w2, b_a0, b_a2, b_g2, b_kk, b_ka, b_rk, b_lnx_w, b_lnx_b, od_w_in, od_w_out, c_conv,
           c_A_log, c_dt_bias, c_onorm, d_qn, d_kn, d_lambda, d_subln, router_w, router_bias, exp_w_gate,
           exp_w_up, exp_w_down, sh_w_gate, sh_w_up, sh_w_down):
    x = jnp.concatenate([x_prompt.reshape(N_CTX, D_MODEL), x_sample.reshape(N_LAT, D_MODEL)], axis=0)
    cond = jnp.concatenate([c_ctx[None], c, jnp.zeros((8 - N_MOD, D_MODEL), F32)], axis=0)
    mod = _adaln_call(cond, mod_w, mod_b)[:, 0:N_MOD].reshape(DEPTH, N_MOD, 6, D_MODEL)
    lat_blk = N_CTX // DEC_SEQ
    new_ak, new_av, new_sr, new_sd, new_dk, new_dv = [], [], [], [], [], []
    for l in range(DEPTH):
        j = l // 2
        if l % 2 == 0:
            w = ev_w_in[j]
            q, k, v = w[:, 0:512], w[:, 512:640], w[:, 640:768]
            w_a = jnp.concatenate([q[:, 0:256], k[:, 0:64], v[:, 0:64], q[:, 256:], k[:, 64:], v[:, 64:]], axis=1)
            amat, bmat = _proj_in_call(x, mod, l, norm1_g[l], [w_a.astype(BF16), w[:, 768:].astype(BF16)])
            gain = jnp.concatenate([jnp.tile(a_qn[j], A_GROUP), a_kn[j], jnp.ones((HD,), F32)]).reshape(1, GW)
            oa_c, nk, nv = _attn_a_call(amat, 0, BATCH, SEQ, gain)
            oa_l, = _attn_a_call(amat, lat_blk, DEC_BATCH, DEC_SEQ, gain, (cache_attn_k, cache_attn_v, j))
            p = {'b_mu': b_mu[j], 'b_w0': b_w0[j], 'b_w2': b_w2[j], 'b_a0': b_a0[j], 'b_a2': b_a2[j],
                 'b_g2': b_g2[j], 'b_kk': b_kk[j], 'b_ka': b_ka[j], 'b_rk': b_rk[j], 'b_lnx_w': b_lnx_w[j],
                 'b_lnx_b': b_lnx_b[j]}
            ob_c, sr = _rwkv_call(bmat, 0, BATCH, SEQ, p)
            ob_l, _ = _rwkv_call(bmat, lat_blk, DEC_BATCH, DEC_SEQ, p, state_rwkv[:, j])
            o1 = jnp.concatenate([oa_c, oa_l], axis=0)
            o2 = jnp.concatenate([ob_c, ob_l], axis=0)
            w_out = ev_w_out[j]
            new_ak.append(nk)
            new_av.append(nv)
            new_sr.append(sr)
        else:
            lam_init = 0.8 - 0.6 * math.exp(-0.3 * l)
            w = od_w_in[j]
            s0 = 4 * C_WIDTH + 4 * C_HEADS
            dq, dk, dv = w[:, s0:s0 + 512], w[:, s0 + 512:s0 + 1024], w[:, s0 + 1024:]
            w_ab = jnp.pad(w[:, 4 * C_WIDTH:s0], ((0, 0), (0, LANE - 4 * C_HEADS)))
            w_d = jnp.concatenate([jnp.concatenate([dq[:, 128 * h:128 * (h + 1)], dk[:, 128 * h:128 * (h + 1)],
                                                    dv[:, 128 * h:128 * (h + 1)]], axis=1) for h in range(D_HEADS)],
                                  axis=1)
            cmat, abmat, dmat = _proj_in_call(x, mod, l, norm1_g[l],
                                              [w[:, 0:4 * C_WIDTH].astype(BF16), w_ab.astype(BF16), w_d.astype(BF16)])
            p = {'c_conv': c_conv[j], 'c_A_log': c_A_log[j], 'c_dt_bias': c_dt_bias[j], 'c_onorm': c_onorm[j]}
            oc_c, sd_ = _delta_call(cmat, abmat, 0, BATCH, SEQ, p)
            oc_l, _ = _delta_call(cmat, abmat, lat_blk, DEC_BATCH, DEC_SEQ, p, state_delta[:, j])
            gain = jnp.concatenate([jnp.tile(d_qn[j], 2), jnp.tile(d_kn[j], 2), jnp.ones((D_VDIM,), F32)]).reshape(1, GW)
            od_c, ndk, ndv = _attn_d_call(dmat, 0, BATCH, SEQ, gain, d_lambda[j], d_subln[j], lam_init)
            od_l, = _attn_d_call(dmat, lat_blk, DEC_BATCH, DEC_SEQ, gain, d_lambda[j], d_subln[j], lam_init,
                                 (cache_diff_k, cache_diff_v, j))
            o1 = jnp.concatenate([oc_c, oc_l], axis=0)
            o2 = jnp.concatenate([od_c, od_l], axis=0)
            w_out = od_w_out[j]
            new_dk.append(ndk)
            new_dv.append(ndv)
            new_sd.append(sd_)
        x, tok, sh, logits_t = _proj_out_call(
            o1, o2, x, mod, l, norm2_g[l], w_out.astype(BF16), router_w[l].T.astype(BF16),
            sh_w_gate[l].astype(BF16), sh_w_up[l].astype(BF16), sh_w_down[l].astype(BF16))
        idx, wts = _route_call(logits_t, router_bias[l])
        tables = _dispatch_tables(idx, M_TOK)
        yrows = _expert_call(tok, tables, l, exp_w_gate, exp_w_up, exp_w_down)
        x = _combine_call(x, sh, wts.T, mod, l, yrows)
    return (x[0:N_CTX].reshape(BATCH, SEQ, D_MODEL), x[N_CTX:].reshape(DEC_BATCH, DEC_SEQ, D_MODEL),
            jnp.stack(new_ak, axis=1), jnp.stack(new_av, axis=1), jnp.stack(new_sr, axis=1),
            jnp.stack(new_sd, axis=1), jnp.stack(new_dk, axis=1), jnp.stack(new_dv, axis=1))
```

```python
import functools
import math
import jax
import jax.numpy as jnp
from jax import lax
from jax.experimental import pallas as pl
from jax.experimental.pallas import tpu as pltpu

D_MODEL = 1024
BATCH = 32
SEQ = 256
DEPTH = 4
DEC_BATCH = 2
DEC_SEQ = 1024
PAST_LEN = 512
GRID_W = 64
NORM_EPS = 1e-6
ROPE_THETA = 10000.0
A_HEADS = 8
A_KV_HEADS = 2
A_GROUP = A_HEADS // A_KV_HEADS
B_HEADS = 8
B_DIM = 64
B_WIDTH = B_HEADS * B_DIM
DECAY_LORA = 64
ICLR_LORA = 64
GATE_LORA = 128
B_COLS = 3 * B_WIDTH + DECAY_LORA + ICLR_LORA + GATE_LORA
LNX_EPS = 64e-5
C_HEADS = 4
C_DIM = 128
C_WIDTH = C_HEADS * C_DIM
D_HEADS = 4
D_VDIM = 128
N_EXPERTS = 64
TOP_K = 6
N_GROUPS = 8
TOPK_GROUPS = 4
PER_GROUP = N_EXPERTS // N_GROUPS
EXPERT_FF = 256
SHARED_FF = 256
ROUTED_SCALE = 1.0

HD = 64
GW = 384
CH = 64
LANE = 128
BM = 128
DMA_UNROLL = 8
TM = 512
N_CTX = BATCH * SEQ
N_LAT = DEC_BATCH * DEC_SEQ
M_TOK = N_CTX + N_LAT
N_MOD = 1 + DEC_BATCH
VMEM_LIMIT = 56 * 1024 * 1024

F32 = jnp.float32
BF16 = jnp.bfloat16
HI = lax.Precision.HIGHEST
NEG = -jnp.inf


def _dot(a, b):
    return jnp.dot(a.astype(BF16), b.astype(BF16), preferred_element_type=F32)


def _dot_nt(a, b):
    return lax.dot_general(a.astype(BF16), b.astype(BF16), (((1,), (1,)), ((), ())), preferred_element_type=F32)


def _dot_tn(a, b):
    return lax.dot_general(a.astype(BF16), b.astype(BF16), (((0,), (0,)), ((), ())), preferred_element_type=F32)


def _dot_hi(a, b):
    return jnp.dot(a, b, preferred_element_type=F32, precision=HI)


def _silu(x):
    return x * jax.nn.sigmoid(x)


def _mod_group(i):
    n_ctx_tiles = N_CTX // TM
    return jnp.where(i < n_ctx_tiles, 0, 1 + (i - n_ctx_tiles) // (DEC_SEQ // TM))


def _full(shape):
    return pl.BlockSpec(shape, lambda *_: (0,) * len(shape))


def _adaln_kernel(c_ref, w_ref, b_ref, o_ref):
    o_ref[0] = _dot(_silu(c_ref[...]), w_ref[0]) + b_ref[0]


def _adaln_call(cond, mod_w, mod_b):
    n = 6
    return pl.pallas_call(
        _adaln_kernel,
        grid=(DEPTH, n),
        in_specs=[_full((8, D_MODEL)),
                  pl.BlockSpec((1, D_MODEL, D_MODEL), lambda l, j: (l, 0, j)),
                  pl.BlockSpec((1, 1, D_MODEL), lambda l, j: (l, 0, j))],
        out_specs=pl.BlockSpec((1, 8, D_MODEL), lambda l, j: (l, 0, j)),
        out_shape=jax.ShapeDtypeStruct((DEPTH, 8, n * D_MODEL), F32),
    )(cond, mod_w, mod_b.reshape(DEPTH, 1, n * D_MODEL))


def _proj_in_kernel(n_out, x_ref, mod_ref, g_ref, *rest):
    x = x_ref[...]
    y = x * lax.rsqrt(jnp.mean(x * x, axis=-1, keepdims=True) + NORM_EPS) * g_ref[...]
    h = (y * (1.0 + mod_ref[0, 0, 1:2, :]) + mod_ref[0, 0, 0:1, :]).astype(BF16)
    for w_ref, o_ref in zip(rest[:n_out], rest[n_out:]):
        o_ref[...] = jnp.dot(h, w_ref[...], preferred_element_type=F32)


def _proj_in_call(x, mod, layer, g, weights):
    n_out = len(weights)
    return pl.pallas_call(
        functools.partial(_proj_in_kernel, n_out),
        grid=(M_TOK // TM,),
        in_specs=[pl.BlockSpec((TM, D_MODEL), lambda i: (i, 0)),
                  pl.BlockSpec((1, 1, 6, D_MODEL), lambda i: (layer, _mod_group(i), 0, 0)),
                  _full((1, D_MODEL))] + [_full(w.shape) for w in weights],
        out_specs=[pl.BlockSpec((TM, w.shape[1]), lambda i: (i, 0)) for w in weights],
        out_shape=[jax.ShapeDtypeStruct((M_TOK, w.shape[1]), F32) for w in weights],
        compiler_params=pltpu.CompilerParams(vmem_limit_bytes=VMEM_LIMIT),
    )(x, mod, g.reshape(1, D_MODEL), *weights)


def _proj_out_kernel(o1_ref, o2_ref, x_ref, mod_ref, g_ref, wo_ref, rwt_ref, sg_ref, su_ref, sd_ref,
                     xn_ref, tok_ref, sh_ref, lg_ref):
    o = jnp.concatenate([o1_ref[...], o2_ref[...]], axis=1).astype(BF16)
    xn = x_ref[...] + mod_ref[0, 0, 2:3, :] * jnp.dot(o, wo_ref[...], preferred_element_type=F32)
    xn_ref[...] = xn
    y = xn * lax.rsqrt(jnp.mean(xn * xn, axis=-1, keepdims=True) + NORM_EPS) * g_ref[...]
    h2 = y * (1.0 + mod_ref[0, 0, 4:5, :]) + mod_ref[0, 0, 3:4, :]
    tok_ref[...] = h2
    hb = h2.astype(BF16)
    lg_ref[...] = _dot_nt(rwt_ref[...], hb)
    hid = _silu(jnp.dot(hb, sg_ref[...], preferred_element_type=F32)) * jnp.dot(hb, su_ref[...], preferred_element_type=F32)
    sh_ref[...] = jnp.dot(hid.astype(BF16), sd_ref[...], preferred_element_type=F32)


def _proj_out_call(o1, o2, x, mod, layer, g2, w_out, rw_t, sg, su, sd):
    row = lambda n: pl.BlockSpec((TM, n), lambda i: (i, 0))
    return pl.pallas_call(
        _proj_out_kernel,
        grid=(M_TOK // TM,),
        in_specs=[row(o1.shape[1]), row(o2.shape[1]), row(D_MODEL),
                  pl.BlockSpec((1, 1, 6, D_MODEL), lambda i: (layer, _mod_group(i), 0, 0)),
                  _full((1, D_MODEL)), _full(w_out.shape), _full(rw_t.shape), _full(sg.shape), _full(su.shape),
                  _full(sd.shape)],
        out_specs=[row(D_MODEL), row(D_MODEL), row(D_MODEL), pl.BlockSpec((N_EXPERTS, TM), lambda i: (0, i))],
        out_shape=[jax.ShapeDtypeStruct((M_TOK, D_MODEL), F32)] * 3 + [jax.ShapeDtypeStruct((N_EXPERTS, M_TOK), F32)],
        compiler_params=pltpu.CompilerParams(vmem_limit_bytes=VMEM_LIMIT),
    )(o1, o2, x, mod, g2.reshape(1, D_MODEL), w_out, rw_t, sg, su, sd)


def _rope_lane_tables(n_tok):
    rows = n_tok // GRID_W
    row = jnp.repeat(jnp.arange(rows, dtype=F32), GRID_W)
    col = jnp.tile(jnp.arange(GRID_W, dtype=F32), rows)
    n_freq = HD // 4
    inv = ROPE_THETA ** (-jnp.arange(n_freq, dtype=F32) / n_freq)
    ang = jnp.concatenate([row[:, None] * inv, col[:, None] * inv], axis=-1)
    cos = jnp.repeat(jnp.cos(ang), 2, axis=-1)
    sin = jnp.repeat(jnp.sin(ang), 2, axis=-1) * jnp.tile(jnp.array([-1.0, 1.0], F32), HD // 2)
    return cos, sin


def _seg_matrix(width, seg, value):
    li = lax.broadcasted_iota(jnp.int32, (width, width), 0) // seg
    lj = lax.broadcasted_iota(jnp.int32, (width, width), 1) // seg
    return jnp.where(li == lj, value, 0.0).astype(F32)


def _norm_rope(x, gain, n_norm, cos, sin):
    lane = lax.broadcasted_iota(jnp.int32, x.shape, 1)
    ms = _dot_hi(x * x, _seg_matrix(x.shape[1], HD, 1.0 / HD))
    xn = jnp.where(lane < n_norm, x * lax.rsqrt(ms + NORM_EPS) * gain, x)
    if cos is None:
        return xn, xn
    w = x.shape[1]
    swapped = jnp.where(lane % 2 == 0, pltpu.roll(xn, w - 1, 1), pltpu.roll(xn, 1, 1))
    return xn, xn * cos + swapped * sin


def _softmax_pv(s, v):
    m = jnp.max(s, axis=-1, keepdims=True)
    p = jnp.exp(s - m)
    l = jnp.sum(p, axis=-1, keepdims=True)
    return _dot(p, v) / l


def _attn_a_kernel(T, P, TQ, *refs):
    if P:
        x_ref, gain_ref, cos_ref, sin_ref, ck_ref, cv_ref, o_ref, xr_s = refs
        xn, xr = _norm_rope(x_ref[...], gain_ref[...], 5 * HD, cos_ref[...], sin_ref[...])
    else:
        x_ref, gain_ref, o_ref, nk_ref, nv_ref, xr_s = refs
        xn, xr = _norm_rope(x_ref[...], gain_ref[...], 5 * HD, None, None)
        nk_ref[0, 0] = xn[:, 4 * HD:5 * HD]
        nv_ref[0, 0] = xn[:, 5 * HD:6 * HD]
    xr_s[...] = xr
    k_new = xr[:, 4 * HD:5 * HD]
    v_new = xr[:, 5 * HD:6 * HD]
    if P:
        k_all = jnp.concatenate([ck_ref[0, 0, 0], k_new], axis=0).astype(BF16)
        v_all = jnp.concatenate([cv_ref[0, 0, 0], v_new], axis=0).astype(BF16)
    else:
        k_all, v_all = k_new.astype(BF16), v_new.astype(BF16)

    def q_block(qb, carry):
        rows = pl.ds(pl.multiple_of(qb * TQ, TQ), TQ)
        qx = xr_s[rows, 0:A_GROUP * HD]
        qs = jnp.concatenate([qx[:, i * HD:(i + 1) * HD] for i in range(A_GROUP)], axis=0)
        o = _softmax_pv(_dot_nt(qs, k_all) * (HD ** -0.5), v_all)
        o_ref[rows, :] = jnp.concatenate([o[i * TQ:(i + 1) * TQ, :] for i in range(A_GROUP)], axis=1)
        return carry

    lax.fori_loop(0, T // TQ, q_block, 0)


def _attn_a_call(amat, row_off_blocks, n_seq, T, gain, cache=None):
    P = 0 if cache is None else cache[0].shape[3]
    TQ = T if P == 0 else 128
    in_specs = [pl.BlockSpec((T, GW), lambda i, g: (row_off_blocks + i, g)), _full((1, GW))]
    args = [amat, gain]
    out_specs = [pl.BlockSpec((T, A_GROUP * HD), lambda i, g: (i, g))]
    out_shape = [jax.ShapeDtypeStruct((n_seq * T, A_HEADS * HD), F32)]
    if P:
        ck, cv, j = cache
        cos, sin = _rope_lane_tables(T)
        cos_g = jnp.concatenate([cos] * 5 + [jnp.ones((T, HD), F32)], axis=1)
        sin_g = jnp.concatenate([sin] * 5 + [jnp.zeros((T, HD), F32)], axis=1)
        in_specs += [_full((T, GW)), _full((T, GW)),
                     pl.BlockSpec((1, 1, 1, P, HD), lambda i, g: (i, j, g, 0, 0)),
                     pl.BlockSpec((1, 1, 1, P, HD), lambda i, g: (i, j, g, 0, 0))]
        args += [cos_g, sin_g, ck, cv]
    else:
        out_specs += [pl.BlockSpec((1, 1, T, HD), lambda i, g: (i, g, 0, 0))] * 2
        out_shape += [jax.ShapeDtypeStruct((n_seq, A_KV_HEADS, T, HD), F32)] * 2
    return pl.pallas_call(
        functools.partial(_attn_a_kernel, T, P, TQ),
        grid=(n_seq, A_KV_HEADS),
        in_specs=in_specs, out_specs=out_specs, out_shape=out_shape,
        scratch_shapes=[pltpu.VMEM((T, GW), F32)],
        compiler_params=pltpu.CompilerParams(vmem_limit_bytes=VMEM_LIMIT),
    )(*args)


def _attn_d_kernel(T, P, TQ, lam_init, *refs):
    if P:
        x_ref, gain_ref, lam_ref, sub_ref, cos_ref, sin_ref, ck_ref, cv_ref, o_ref, xr_s = refs
        xn, xr = _norm_rope(x_ref[...], gain_ref[...], 4 * HD, cos_ref[...], sin_ref[...])
    else:
        x_ref, gain_ref, lam_ref, sub_ref, o_ref, nk_ref, nv_ref, xr_s = refs
        xn, xr = _norm_rope(x_ref[...], gain_ref[...], 4 * HD, None, None)
        nk_ref[0, 0, 0] = xn[:, 2 * HD:3 * HD]
        nk_ref[0, 0, 1] = xn[:, 3 * HD:4 * HD]
        nv_ref[0, 0] = xn[:, 4 * HD:]
    xr_s[...] = xr
    v_new = xr[:, 4 * HD:]
    ks = []
    for m in range(2):
        k_new = xr[:, (2 + m) * HD:(3 + m) * HD]
        if P:
            ks.append(jnp.concatenate([ck_ref[0, 0, 0, m], k_new], axis=0).astype(BF16))
        else:
            ks.append(k_new.astype(BF16))
    v_all = (jnp.concatenate([cv_ref[0, 0, 0], v_new], axis=0) if P else v_new).astype(BF16)
    lm = lam_ref[...]
    lam = (jnp.exp(jnp.sum(lm[0:1, :] * lm[1:2, :], axis=-1, keepdims=True))
           - jnp.exp(jnp.sum(lm[2:3, :] * lm[3:4, :], axis=-1, keepdims=True)) + lam_init)

    def q_block(qb, carry):
        rows = pl.ds(pl.multiple_of(qb * TQ, TQ), TQ)
        qx = xr_s[rows, 0:2 * HD]
        o1 = _softmax_pv(_dot_nt(qx[:, 0:HD], ks[0]) * (HD ** -0.5), v_all)
        o2 = _softmax_pv(_dot_nt(qx[:, HD:], ks[1]) * (HD ** -0.5), v_all)
        od = o1 - lam * o2
        od = od * lax.rsqrt(jnp.mean(od * od, axis=-1, keepdims=True) + NORM_EPS) * sub_ref[...]
        o_ref[rows, :] = od * (1.0 - lam_init)
        return carry

    lax.fori_loop(0, T // TQ, q_block, 0)


def _attn_d_call(dmat, row_off_blocks, n_seq, T, gain, lam, subln, lam_init, cache=None):
    P = 0 if cache is None else cache[0].shape[4]
    TQ = T if P == 0 else 256
    in_specs = [pl.BlockSpec((T, GW), lambda i, h: (row_off_blocks + i, h)),
                _full((1, GW)), _full((4, HD)), _full((1, D_VDIM))]
    args = [dmat, gain, lam, subln.reshape(1, -1)]
    out_specs = [pl.BlockSpec((T, D_VDIM), lambda i, h: (i, h))]
    out_shape = [jax.ShapeDtypeStruct((n_seq * T, D_HEADS * D_VDIM), F32)]
    if P:
        ck, cv, j = cache
        cos, sin = _rope_lane_tables(T)
        cos_g = jnp.concatenate([cos] * 4 + [jnp.ones((T, D_VDIM), F32)], axis=1)
        sin_g = jnp.concatenate([sin] * 4 + [jnp.zeros((T, D_VDIM), F32)], axis=1)
        in_specs += [_full((T, GW)), _full((T, GW)),
                     pl.BlockSpec((1, 1, 1, 2, P, HD), lambda i, h: (i, j, h, 0, 0, 0)),
                     pl.BlockSpec((1, 1, 1, P, D_VDIM), lambda i, h: (i, j, h, 0, 0))]
        args += [cos_g, sin_g, ck, cv]
    else:
        out_specs += [pl.BlockSpec((1, 1, 2, T, HD), lambda i, h: (i, h, 0, 0, 0)),
                      pl.BlockSpec((1, 1, T, D_VDIM), lambda i, h: (i, h, 0, 0))]
        out_shape += [jax.ShapeDtypeStruct((n_seq, D_HEADS, 2, T, HD), F32),
                      jax.ShapeDtypeStruct((n_seq, D_HEADS, T, D_VDIM), F32)]
    return pl.pallas_call(
        functools.partial(_attn_d_kernel, T, P, TQ, lam_init),
        grid=(n_seq, D_HEADS),
        in_specs=in_specs, out_specs=out_specs, out_shape=out_shape,
        scratch_shapes=[pltpu.VMEM((T, GW), F32)],
        compiler_params=pltpu.CompilerParams(vmem_limit_bytes=VMEM_LIMIT),
    )(*args)


def _chunk_masks(d):
    ci = lax.broadcasted_iota(jnp.int32, (CH, CH), 0)
    cj = lax.broadcasted_iota(jnp.int32, (CH, CH), 1)
    return (ci >= cj, ci > cj) if d == 0 else (ci <= cj, ci < cj)


def _dot_3pass(a, b):
    a_hi = a.astype(BF16)
    a_lo = (a - a_hi.astype(F32)).astype(BF16)
    b_hi = b.astype(BF16)
    b_lo = (b - b_hi.astype(F32)).astype(BF16)
    d = lambda x, y: jnp.dot(x, y, preferred_element_type=F32)
    return d(a_hi, b_hi) + (d(a_hi, b_lo) + d(a_lo, b_hi))


def _unit_lower_solve(am, xm, dot):
    n_stage = CH.bit_length() - 1
    for s in range(n_stage):
        xm = xm + dot(am, xm)
        if s < n_stage - 1:
            am = dot(am, am)
    return xm


def _shifted_rows(x, T):
    row = lax.broadcasted_iota(jnp.int32, (T, 1), 0)
    prev = jnp.where(row == 0, 0.0, pltpu.roll(x, 1, 0))
    nxt = jnp.where(row == T - 1, 0.0, pltpu.roll(x, T - 1, 0))
    return prev, nxt


def _rwkv_kernel(T, has_state, *refs):
    if has_state:
        (xb_ref, mu_ref, w0_ref, w2_ref, a0_ref, a2_ref, g2_ref, kk_ref, ka_ref, rk_ref, lnw_ref, lnb_ref, s0_ref,
         ob_ref, sf_ref, r_s, v_s, av_s, lw_s, kd_s, bv_s, y_s, st_s) = refs
    else:
        (xb_ref, mu_ref, w0_ref, w2_ref, a0_ref, a2_ref, g2_ref, kk_ref, ka_ref, rk_ref, lnw_ref, lnb_ref,
         ob_ref, sf_ref, r_s, v_s, av_s, lw_s, kd_s, bv_s, y_s, st_s) = refs
    n_chunks = T // CH
    x = xb_ref[...]
    prev, nxt = _shifted_rows(x, T)
    xs = x + mu_ref[...] * (0.5 * (prev + nxt) - x)
    r = xs[:, 0:B_WIDTH]
    kb = xs[:, B_WIDTH:2 * B_WIDTH]
    vb = xs[:, 2 * B_WIDTH:3 * B_WIDTH]
    wd = xs[:, 3 * B_WIDTH:3 * B_WIDTH + DECAY_LORA]
    ad = xs[:, 3 * B_WIDTH + DECAY_LORA:3 * B_WIDTH + DECAY_LORA + ICLR_LORA]
    gd = xs[:, 3 * B_WIDTH + DECAY_LORA + ICLR_LORA:]
    seg = _seg_matrix(B_WIDTH, B_DIM, 1.0)
    kk = kb * kk_ref[...]
    kkn = kk * lax.rsqrt(_dot_hi(kk * kk, seg) + 1e-6)
    r_s[...] = r
    v_s[...] = vb
    av_s[...] = -kkn
    twd = jnp.tanh(wd)
    for d in range(2):
        wl = w0_ref[d:d + 1, :] + _dot(twd, w2_ref[d])
        w_log = -jax.nn.softplus(-wl) - 0.5
        lw_s[d] = -jnp.exp(w_log)
        a = jax.nn.sigmoid(a0_ref[d:d + 1, :] + _dot(ad, a2_ref[d]))
        kd_s[d] = kb * (1.0 + (a - 1.0) * ka_ref[...])
        bv_s[d] = kkn * a
    if has_state:
        st_s[...] = s0_ref[0]
    else:
        st_s[...] = jnp.zeros_like(st_s)
    ci2 = lax.broadcasted_iota(jnp.int32, (CH, 2 * CH), 0)
    cj2 = lax.broadcasted_iota(jnp.int32, (CH, 2 * CH), 1) % CH
    for d in range(2):
        incl, strict = _chunk_masks(d)
        incl2 = ci2 >= cj2 if d == 0 else ci2 <= cj2
        tri = jnp.where(incl, 1.0, 0.0).astype(F32)
        last = CH - 1 if d == 0 else 0

        def chunk_body(it, carry, d=d, incl2=incl2, strict=strict, tri=tri, last=last):
            c = it if d == 0 else n_chunks - 1 - it
            rows = pl.ds(pl.multiple_of(c * CH, CH), CH)
            lwc = lw_s[d, rows, :]
            cum = _dot_hi(tri, lwc)
            e_pos = jnp.exp(cum)
            e_neg = jnp.exp(-cum)
            rt = r_s[rows, :] * e_pos
            at = av_s[rows, :] * jnp.exp(cum - lwc)
            bh = bv_s[d, rows, :] * e_neg
            kh = kd_s[d, rows, :] * e_neg
            vc = v_s[rows, :]
            pc = e_pos[last:last + 1, :]
            ys = []
            for h in range(B_HEADS):
                sl = slice(h * B_DIM, (h + 1) * B_DIM)
                at_h, rt_h, v_h = at[:, sl], rt[:, sl], vc[:, sl]
                bk_h = jnp.concatenate([bh[:, sl], kh[:, sl]], axis=0)
                g = _dot_nt(jnp.concatenate([at_h, rt_h], axis=0), bk_h)
                a_ab = jnp.where(strict, g[0:CH, 0:CH], 0.0)
                a_ak = jnp.where(strict, g[0:CH, CH:], 0.0)
                a_r = jnp.where(incl2, g[CH:, :], 0.0)
                xm = _unit_lower_solve(a_ab, jnp.concatenate([at_h, _dot(a_ak, v_h)], axis=1), _dot)
                s_old = st_s[d, h]
                ws = _dot_nt(jnp.concatenate([xm[:, 0:B_DIM], rt_h], axis=0), s_old)
                uv = jnp.concatenate([ws[0:CH, :] + xm[:, B_DIM:], v_h], axis=0)
                ys.append(ws[CH:, :] + _dot(a_r, uv))
                st_s[d, h] = (s_old + _dot_tn(uv, bk_h)) * pc[:, sl]
            yc = jnp.concatenate(ys, axis=1)
            if d == 0:
                y_s[rows, :] = yc
            else:
                y_s[rows, :] = y_s[rows, :] + yc
            return carry

        lax.fori_loop(0, n_chunks, chunk_body, 0)
    y = y_s[...]
    yc = y - _dot_hi(y, seg) * (1.0 / B_DIM)
    var = _dot_hi(yc * yc, seg) * (1.0 / B_DIM)
    yn = yc * lax.rsqrt(var + LNX_EPS) * lnw_ref[...] + lnb_ref[...]
    yn = yn + _dot_hi(r * kb * rk_ref[...], seg) * vb
    ob_ref[...] = yn * _dot(jax.nn.sigmoid(gd), g2_ref[...])
    sf_ref[0] = st_s[...]


def _rwkv_call(xb, row_off_blocks, n_seq, T, p, s0=None):
    has_state = s0 is not None
    st_shape = (2, B_HEADS, B_DIM, B_DIM)
    in_specs = [pl.BlockSpec((T, B_COLS), lambda i: (row_off_blocks + i, 0)),
                _full((1, B_COLS)), _full((2, B_WIDTH)), _full((2, DECAY_LORA, B_WIDTH)), _full((2, B_WIDTH)),
                _full((2, ICLR_LORA, B_WIDTH)), _full((GATE_LORA, B_WIDTH))] + [_full((1, B_WIDTH))] * 5
    args = [xb, p['b_mu'].reshape(1, -1), p['b_w0'], p['b_w2'], p['b_a0'], p['b_a2'], p['b_g2'],
            p['b_kk'].reshape(1, -1), p['b_ka'].reshape(1, -1), p['b_rk'].reshape(1, -1),
            p['b_lnx_w'].reshape(1, -1), p['b_lnx_b'].reshape(1, -1)]
    if has_state:
        in_specs.append(pl.BlockSpec((1,) + st_shape, lambda i: (i, 0, 0, 0, 0)))
        args.append(s0)
    scr = [pltpu.VMEM((T, B_WIDTH), F32)] * 3 + [pltpu.VMEM((2, T, B_WIDTH), F32)] * 3 + \
          [pltpu.VMEM((T, B_WIDTH), F32), pltpu.VMEM(st_shape, F32)]
    return pl.pallas_call(
        functools.partial(_rwkv_kernel, T, has_state),
        grid=(n_seq,),
        in_specs=in_specs,
        out_specs=[pl.BlockSpec((T, B_WIDTH), lambda i: (i, 0)),
                   pl.BlockSpec((1,) + st_shape, lambda i: (i, 0, 0, 0, 0))],
        out_shape=[jax.ShapeDtypeStruct((n_seq * T, B_WIDTH), F32),
                   jax.ShapeDtypeStruct((n_seq,) + st_shape, F32)],
        scratch_shapes=scr,
        compiler_params=pltpu.CompilerParams(vmem_limit_bytes=VMEM_LIMIT),
    )(*args)


def _delta_kernel(T, has_state, *refs):
    if has_state:
        (c_ref, ab_ref, conv_ref, arow_ref, dtrow_ref, on_ref, s0_ref, oc_ref, sf_ref,
         q_s, k_s, v_s, g_s, b_s, o_s, st_s) = refs
    else:
        (c_ref, ab_ref, conv_ref, arow_ref, dtrow_ref, on_ref, oc_ref, sf_ref,
         q_s, k_s, v_s, g_s, b_s, o_s, st_s) = refs
    n_chunks = T // CH
    x = c_ref[:, 0:3 * C_WIDTH]
    prev, nxt = _shifted_rows(x, T)
    xc = _silu(conv_ref[0:1, :] * prev + conv_ref[1:2, :] * x + conv_ref[2:3, :] * nxt)
    for h in range(C_HEADS):
        sl = slice(h * C_DIM, (h + 1) * C_DIM)
        qh = xc[:, h * C_DIM:(h + 1) * C_DIM]
        kh = xc[:, C_WIDTH + h * C_DIM:C_WIDTH + (h + 1) * C_DIM]
        q_s[:, sl] = qh * lax.rsqrt(jnp.sum(qh * qh, axis=-1, keepdims=True) + 1e-6) * (C_DIM ** -0.5)
        k_s[:, sl] = kh * lax.rsqrt(jnp.sum(kh * kh, axis=-1, keepdims=True) + 1e-6)
    v_s[...] = xc[:, 2 * C_WIDTH:]
    ab = ab_ref[...]
    g_s[...] = arow_ref[...] * jax.nn.softplus(ab + dtrow_ref[...])
    b_s[...] = jax.nn.sigmoid(ab)
    if has_state:
        st_s[...] = s0_ref[0]
    else:
        st_s[...] = jnp.zeros_like(st_s)
    ci = lax.broadcasted_iota(jnp.int32, (CH, CH), 0)
    cj = lax.broadcasted_iota(jnp.int32, (CH, CH), 1)
    eye = jnp.where(ci == cj, 1.0, 0.0).astype(F32)
    for d in range(2):
        incl, strict = _chunk_masks(d)
        tri = jnp.where(incl, 1.0, 0.0).astype(F32)
        last = CH - 1 if d == 0 else 0

        def chunk_body(it, carry, d=d, incl=incl, strict=strict, tri=tri, last=last):
            c = it if d == 0 else n_chunks - 1 - it
            rows = pl.ds(pl.multiple_of(c * CH, CH), CH)
            gcum = _dot_hi(tri, g_s[rows, :])
            bet = b_s[rows, :]
            qc, kc, vc = q_s[rows, :], k_s[rows, :], v_s[rows, :]
            outs = []
            for h in range(C_HEADS):
                j = d * C_HEADS + h
                sl = slice(h * C_DIM, (h + 1) * C_DIM)
                g_col = gcum[:, j:j + 1]
                b_col = bet[:, 2 * C_HEADS + j:2 * C_HEADS + j + 1]
                g_row = jnp.sum(eye * g_col, axis=0, keepdims=True)
                decay = jnp.where(incl, jnp.exp(jnp.where(incl, g_col - g_row, 0.0)), 0.0)
                q_h, k_h, v_h = qc[:, sl], kc[:, sl], vc[:, sl]
                kbeta = k_h * b_col
                gm = _dot_nt(jnp.concatenate([kbeta, q_h], axis=0), k_h)
                attn = gm[CH:, :] * decay
                eg = jnp.exp(g_col)
                xm = _unit_lower_solve(-jnp.where(strict, gm[0:CH, :] * decay, 0.0),
                                       jnp.concatenate([v_h * b_col, kbeta * eg], axis=1), _dot_3pass)
                g_last = g_col[last:last + 1, :]
                s_old = st_s[d, h]
                ws = _dot(jnp.concatenate([xm[:, C_DIM:], q_h * eg], axis=0), s_old)
                v_new = xm[:, 0:C_DIM] - ws[0:CH, :]
                outs.append(ws[CH:, :] + _dot(attn, v_new))
                st_s[d, h] = s_old * jnp.exp(g_last) + _dot_tn(k_h * jnp.exp(g_last - g_col), v_new)
            oc = jnp.concatenate(outs, axis=1)
            if d == 0:
                o_s[rows, :] = oc
            else:
                o_s[rows, :] = o_s[rows, :] + oc
            return carry

        lax.fori_loop(0, n_chunks, chunk_body, 0)
    z = c_ref[:, 3 * C_WIDTH:]
    for h in range(C_HEADS):
        sl = slice(h * C_DIM, (h + 1) * C_DIM)
        oh = o_s[:, sl]
        on = oh * lax.rsqrt(jnp.mean(oh * oh, axis=-1, keepdims=True) + NORM_EPS) * on_ref[...]
        oc_ref[:, sl] = on * _silu(z[:, sl])
    sf_ref[0] = st_s[...]


def _delta_call(cmat, abmat, row_off_blocks, n_seq, T, p, s0=None):
    has_state = s0 is not None
    st_shape = (2, C_HEADS, C_DIM, C_DIM)
    arow = jnp.zeros((1, LANE), F32).at[0, 0:2 * C_HEADS].set(-jnp.exp(p['c_A_log'].reshape(-1)))
    dtrow = jnp.zeros((1, LANE), F32).at[0, 0:2 * C_HEADS].set(p['c_dt_bias'].reshape(-1))
    in_specs = [pl.BlockSpec((T, 4 * C_WIDTH), lambda i: (row_off_blocks + i, 0)),
                pl.BlockSpec((T, LANE), lambda i: (row_off_blocks + i, 0)),
                _full((3, 3 * C_WIDTH)), _full((1, LANE)), _full((1, LANE)), _full((1, C_DIM))]
    args = [cmat, abmat, p['c_conv'], arow, dtrow, p['c_onorm'].reshape(1, -1)]
    if has_state:
        in_specs.append(pl.BlockSpec((1,) + st_shape, lambda i: (i, 0, 0, 0, 0)))
        args.append(s0)
    scr = [pltpu.VMEM((T, C_WIDTH), F32)] * 3 + [pltpu.VMEM((T, LANE), F32)] * 2 + \
          [pltpu.VMEM((T, C_WIDTH), F32), pltpu.VMEM(st_shape, F32)]
    return pl.pallas_call(
        functools.partial(_delta_kernel, T, has_state),
        grid=(n_seq,),
        in_specs=in_specs,
        out_specs=[pl.BlockSpec((T, C_WIDTH), lambda i: (i, 0)),
                   pl.BlockSpec((1,) + st_shape, lambda i: (i, 0, 0, 0, 0))],
        out_shape=[jax.ShapeDtypeStruct((n_seq * T, C_WIDTH), F32),
                   jax.ShapeDtypeStruct((n_seq,) + st_shape, F32)],
        scratch_shapes=scr,
        compiler_params=pltpu.CompilerParams(vmem_limit_bytes=VMEM_LIMIT),
    )(*args)


def _first_max(x, iota, size):
    m = jnp.max(x, axis=0, keepdims=True)
    idx = jnp.min(jnp.where(x == m, iota, size), axis=0, keepdims=True)
    return m, idx


def _route_kernel(lg_ref, bias_ref, idx_ref, wts_ref):
    n = lg_ref.shape[1]
    scores = jax.nn.sigmoid(lg_ref[...])
    biased = scores + bias_ref[...]
    e_iota = lax.broadcasted_iota(jnp.int32, (N_EXPERTS, n), 0)
    g_iota = lax.broadcasted_iota(jnp.int32, (PER_GROUP, n), 0)
    gs = []
    for g in range(N_GROUPS):
        xg = biased[g * PER_GROUP:(g + 1) * PER_GROUP, :]
        m1, i1 = _first_max(xg, g_iota, PER_GROUP)
        m2 = jnp.max(jnp.where(g_iota == i1, NEG, xg), axis=0, keepdims=True)
        gs.append(m1 + m2)
    gscore = jnp.concatenate(gs, axis=0)
    gi = lax.broadcasted_iota(jnp.int32, (N_GROUPS, n), 0)
    gsel = jnp.zeros((N_GROUPS, n), F32)
    for _ in range(TOPK_GROUPS):
        _, ig = _first_max(gscore, gi, N_GROUPS)
        hit = gi == ig
        gsel = jnp.where(hit, 1.0, gsel)
        gscore = jnp.where(hit, NEG, gscore)
    masked = jnp.concatenate(
        [jnp.where(gsel[g:g + 1, :] > 0.0, biased[g * PER_GROUP:(g + 1) * PER_GROUP, :], NEG)
         for g in range(N_GROUPS)], axis=0)
    ids, ws = [], []
    for _ in range(TOP_K):
        _, ie = _first_max(masked, e_iota, N_EXPERTS)
        hit = e_iota == ie
        ids.append(ie)
        ws.append(jnp.sum(jnp.where(hit, scores, 0.0), axis=0, keepdims=True))
        masked = jnp.where(hit, NEG, masked)
    wsum = ws[0]
    for w in ws[1:]:
        wsum = wsum + w
    inv = ROUTED_SCALE / (wsum + 1e-20)
    idx_ref[...] = jnp.concatenate(ids + [jnp.zeros((8 - TOP_K, n), jnp.int32)], axis=0)
    wts_ref[...] = jnp.concatenate([w * inv for w in ws] + [jnp.zeros((8 - TOP_K, n), F32)], axis=0)


def _route_call(logits_t, bias):
    m = logits_t.shape[1]
    return pl.pallas_call(
        _route_kernel,
        grid=(m // TM,),
        in_specs=[pl.BlockSpec((N_EXPERTS, TM), lambda i: (0, i)), _full((N_EXPERTS, 1))],
        out_specs=[pl.BlockSpec((8, TM), lambda i: (0, i))] * 2,
        out_shape=[jax.ShapeDtypeStruct((8, m), jnp.int32), jax.ShapeDtypeStruct((8, m), F32)],
    )(logits_t, bias.reshape(N_EXPERTS, 1))


def _dispatch_tables(idx, m):
    n_asg = m * TOP_K
    nb = -(-n_asg // BM) + N_EXPERTS
    n_pad = nb * BM - n_asg
    flat_e = idx[0:TOP_K, :].reshape(-1)
    e_iota = jnp.arange(N_EXPERTS, dtype=jnp.int32)
    counts = jnp.sum((flat_e[:, None] == e_iota[None, :]).astype(jnp.int32), axis=0)
    padded = (counts + BM - 1) // BM * BM
    pad_end = jnp.cumsum(padded)
    pad_cum = jnp.cumsum(padded - counts)
    pad_e = jnp.sum((pad_cum[None, :] <= jnp.arange(n_pad, dtype=jnp.int32)[:, None]).astype(jnp.int32), axis=1)
    keys = jnp.concatenate([flat_e * 2, pad_e * 2 + 1])
    vals = jnp.concatenate([jnp.arange(n_asg, dtype=jnp.int32), jnp.full((n_pad,), -1, jnp.int32)])
    _, slot_asg = lax.sort((keys, vals), num_keys=1, is_stable=True)
    valid = slot_asg >= 0
    spare = TOP_K * m + jnp.arange(nb * BM, dtype=jnp.int32) % BM
    slot_dst = jnp.where(valid, slot_asg, spare)
    slot_tok = jnp.where(valid, slot_asg % m, 0)
    blk0 = jnp.arange(nb, dtype=jnp.int32) * BM
    block_expert = jnp.minimum(jnp.sum((pad_end[None, :] <= blk0[:, None]).astype(jnp.int32), axis=1), N_EXPERTS - 1)
    n_used = (pad_end[-1] // BM).astype(jnp.int32).reshape(1)
    return slot_tok.reshape(nb, 1, BM), slot_dst.reshape(nb, 1, BM), block_expert.astype(jnp.int32), n_used


def _expert_kernel(be_ref, nu_ref, st_ref, stn_ref, sd_ref, tok_hbm, wg_ref, wu_ref, wd_ref, y_hbm,
                   xbuf, ybuf, sem_in, sem_out):
    i = pl.program_id(0)
    n_used = nu_ref[0]
    slot = i % 2

    def gather(tab_ref, b):
        def body(r, c):
            pltpu.make_async_copy(tok_hbm.at[pl.ds(tab_ref[0, 0, r], 1), :], xbuf.at[b, pl.ds(r, 1), :],
                                  sem_in.at[b]).start()
            return c
        lax.fori_loop(0, BM, body, 0, unroll=DMA_UNROLL)

    def wait_gather(b):
        pltpu.make_async_copy(tok_hbm.at[pl.ds(0, BM), :], xbuf.at[b], sem_in.at[b]).wait()

    def wait_scatter(b):
        pltpu.make_async_copy(ybuf.at[b], y_hbm.at[pl.ds(0, BM), :], sem_out.at[b]).wait()

    @pl.when(i == 0)
    def _():
        ybuf[0] = jnp.zeros((BM, D_MODEL), F32)
        spare = pltpu.make_async_copy(ybuf.at[0], y_hbm.at[pl.ds(y_hbm.shape[0] - BM, BM), :], sem_out.at[0])
        spare.start()
        spare.wait()
        gather(st_ref, 0)

    @pl.when(i + 1 < n_used)
    def _():
        gather(stn_ref, 1 - slot)

    @pl.when(i < n_used)
    def _():
        wait_gather(slot)

        @pl.when(i >= 2)
        def _():
            wait_scatter(slot)

        x = xbuf[slot].astype(BF16)
        g = jnp.dot(x, wg_ref[0, 0].astype(BF16), preferred_element_type=F32)
        u = jnp.dot(x, wu_ref[0, 0].astype(BF16), preferred_element_type=F32)
        ybuf[slot] = jnp.dot((_silu(g) * u).astype(BF16), wd_ref[0, 0].astype(BF16), preferred_element_type=F32)

        def body(r, c):
            pltpu.make_async_copy(ybuf.at[slot, pl.ds(r, 1), :], y_hbm.at[pl.ds(sd_ref[0, 0, r], 1), :],
                                  sem_out.at[slot]).start()
            return c
        lax.fori_loop(0, BM, body, 0, unroll=DMA_UNROLL)

        @pl.when(i == n_used - 1)
        def _():
            wait_scatter(slot)

            @pl.when(i >= 1)
            def _():
                wait_scatter(1 - slot)


def _expert_call(tok, tables, layer, wg, wu, wd):
    m = tok.shape[0]
    slot_tok, slot_dst, block_expert, n_used = tables
    nb = slot_tok.shape[0]
    smem_blk = lambda f: pl.BlockSpec((1, 1, BM), lambda i, be, nu: (f(i), 0, 0), memory_space=pltpu.SMEM)
    grid_spec = pltpu.PrefetchScalarGridSpec(
        num_scalar_prefetch=2,
        grid=(nb,),
        in_specs=[smem_blk(lambda i: i), smem_blk(lambda i: jnp.minimum(i + 1, nb - 1)), smem_blk(lambda i: i),
                  pl.BlockSpec(memory_space=pl.ANY),
                  pl.BlockSpec((1, 1, D_MODEL, EXPERT_FF), lambda i, be, nu: (layer, be[i], 0, 0)),
                  pl.BlockSpec((1, 1, D_MODEL, EXPERT_FF), lambda i, be, nu: (layer, be[i], 0, 0)),
                  pl.BlockSpec((1, 1, EXPERT_FF, D_MODEL), lambda i, be, nu: (layer, be[i], 0, 0))],
        out_specs=pl.BlockSpec(memory_space=pl.ANY),
        scratch_shapes=[pltpu.VMEM((2, BM, D_MODEL), F32), pltpu.VMEM((2, BM, D_MODEL), F32),
                        pltpu.SemaphoreType.DMA((2,)), pltpu.SemaphoreType.DMA((2,))])
    return pl.pallas_call(
        _expert_kernel, grid_spec=grid_spec,
        out_shape=jax.ShapeDtypeStruct((TOP_K * m + BM, D_MODEL), F32),
        compiler_params=pltpu.CompilerParams(vmem_limit_bytes=VMEM_LIMIT),
    )(block_expert, n_used, slot_tok, slot_tok, slot_dst, tok, wg, wu, wd)


def _combine_kernel(x_ref, sh_ref, w_ref, mod_ref, *rest):
    y_refs, o_ref = rest[:TOP_K], rest[TOP_K]
    w = w_ref[...]
    acc = sh_ref[...]
    for k, y_ref in enumerate(y_refs):
        acc = acc + w[:, k:k + 1] * y_ref[...]
    o_ref[...] = x_ref[...] + mod_ref[0, 0, 5:6, :] * acc


def _combine_call(x, sh, wts_rows, mod, layer, yrows):
    nt = M_TOK // TM
    row = pl.BlockSpec((TM, D_MODEL), lambda i: (i, 0))
    ysp = [pl.BlockSpec((TM, D_MODEL), (lambda i, k=k: (k * nt + i, 0))) for k in range(TOP_K)]
    return pl.pallas_call(
        _combine_kernel,
        grid=(nt,),
        in_specs=[row, row, pl.BlockSpec((TM, 8), lambda i: (i, 0)),
                  pl.BlockSpec((1, 1, 6, D_MODEL), lambda i: (layer, _mod_group(i), 0, 0))] + ysp,
        out_specs=row,
        out_shape=jax.ShapeDtypeStruct((M_TOK, D_MODEL), F32),
        compiler_params=pltpu.CompilerParams(vmem_limit_bytes=VMEM_LIMIT),
    )(x, sh, wts_rows, mod, *([yrows] * TOP_K))


def kernel(x_prompt, x_sample, cache_attn_k, cache_attn_v, state_rwkv, state_delta, cache_diff_k,
           cache_diff_v, c, c_ctx, mod_w, mod_b, norm1_g, norm2_g, ev_w_in, ev_w_out, a_qn, a_kn, b_mu,
           b_w0, b_w2, b_a0, b_a2, b_g2, b_kk, b_ka, b_rk, b_lnx_w, b_lnx_b, od_w_in, od_w_out, c_conv,
           c_A_log, c_dt_bias, c_onorm, d_qn, d_kn, d_lambda, d_subln, router_w, router_bias, exp_w_gate,
           exp_w_up, exp_w_down, sh_w_gate, sh_w_up, sh_w_down):
    x = jnp.concatenate([x_prompt.reshape(N_CTX, D_MODEL), x_sample.reshape(N_LAT, D_MODEL)], axis=0)
    cond = jnp.concatenate([c_ctx[None], c, jnp.zeros((8 - N_MOD, D_MODEL), F32)], axis=0)
    mod = _adaln_call(cond, mod_w, mod_b)[:, 0:N_MOD].reshape(DEPTH, N_MOD, 6, D_MODEL)
    lat_blk = N_CTX // DEC_SEQ
    new_ak, new_av, new_sr, new_sd, new_dk, new_dv = [], [], [], [], [], []
    for l in range(DEPTH):
        j = l // 2
        if l % 2 == 0:
            w = ev_w_in[j]
            q, k, v = w[:, 0:512], w[:, 512:640], w[:, 640:768]
            w_a = jnp.concatenate([q[:, 0:256], k[:, 0:64], v[:, 0:64], q[:, 256:], k[:, 64:], v[:, 64:]], axis=1)
            amat, bmat = _proj_in_call(x, mod, l, norm1_g[l], [w_a.astype(BF16), w[:, 768:].astype(BF16)])
            gain = jnp.concatenate([jnp.tile(a_qn[j], A_GROUP), a_kn[j], jnp.ones((HD,), F32)]).reshape(1, GW)
            oa_c, nk, nv = _attn_a_call(amat, 0, BATCH, SEQ, gain)
            oa_l, = _attn_a_call(amat, lat_blk, DEC_BATCH, DEC_SEQ, gain, (cache_attn_k, cache_attn_v, j))
            p = {'b_mu': b_mu[j], 'b_w0': b_w0[j], 'b_w2': b_w2[j], 'b_a0': b_a0[j], 'b_a2': b_a2[j],
                 'b_g2': b_g2[j], 'b_kk': b_kk[j], 'b_ka': b_ka[j], 'b_rk': b_rk[j], 'b_lnx_w': b_lnx_w[j],
                 'b_lnx_b': b_lnx_b[j]}
            ob_c, sr = _rwkv_call(bmat, 0, BATCH, SEQ, p)
            ob_l, _ = _rwkv_call(bmat, lat_blk, DEC_BATCH, DEC_SEQ, p, state_rwkv[:, j])
            o1 = jnp.concatenate([oa_c, oa_l], axis=0)
            o2 = jnp.concatenate([ob_c, ob_l], axis=0)
            w_out = ev_w_out[j]
            new_ak.append(nk)
            new_av.append(nv)
            new_sr.append(sr)
        else:
            lam_init = 0.8 - 0.6 * math.exp(-0.3 * l)
            w = od_w_in[j]
            s0 = 4 * C_WIDTH + 4 * C_HEADS
            dq, dk, dv = w[:, s0:s0 + 512], w[:, s0 + 512:s0 + 1024], w[:, s0 + 1024:]
            w_ab = jnp.pad(w[:, 4 * C_WIDTH:s0], ((0, 0), (0, LANE - 4 * C_HEADS)))
            w_d = jnp.concatenate([jnp.concatenate([dq[:, 128 * h:128 * (h + 1)], dk[:, 128 * h:128 * (h + 1)],
                                                    dv[:, 128 * h:128 * (h + 1)]], axis=1) for h in range(D_HEADS)],
                                  axis=1)
            cmat, abmat, dmat = _proj_in_call(x, mod, l, norm1_g[l],
                                              [w[:, 0:4 * C_WIDTH].astype(BF16), w_ab.astype(BF16), w_d.astype(BF16)])
            p = {'c_conv': c_conv[j], 'c_A_log': c_A_log[j], 'c_dt_bias': c_dt_bias[j], 'c_onorm': c_onorm[j]}
            oc_c, sd_ = _delta_call(cmat, abmat, 0, BATCH, SEQ, p)
            oc_l, _ = _delta_call(cmat, abmat, lat_blk, DEC_BATCH, DEC_SEQ, p, state_delta[:, j])
            gain = jnp.concatenate([jnp.tile(d_qn[j], 2), jnp.tile(d_kn[j], 2), jnp.ones((D_VDIM,), F32)]).reshape(1, GW)
            od_c, ndk, ndv = _attn_d_call(dmat, 0, BATCH, SEQ, gain, d_lambda[j], d_subln[j], lam_init)
            od_l, = _attn_d_call(dmat, lat_blk, DEC_BATCH, DEC_SEQ, gain, d_lambda[j], d_subln[j], lam_init,
                                 (cache_diff_k, cache_diff_v, j))
            o1 = jnp.concatenate([oc_c, oc_l], axis=0)
            o2 = jnp.concatenate([od_c, od_l], axis=0)
            w_out = od_w_out[j]
            new_dk.append(ndk)
            new_dv.append(ndv)
            new_sd.append(sd_)
        x, tok, sh, logits_t = _proj_out_call(
            o1, o2, x, mod, l, norm2_g[l], w_out.astype(BF16), router_w[l].T.astype(BF16),
            sh_w_gate[l].astype(BF16), sh_w_up[l].astype(BF16), sh_w_down[l].astype(BF16))
        idx, wts = _route_call(logits_t, router_bias[l])
        tables = _dispatch_tables(idx, M_TOK)
        yrows = _expert_call(tok, tables, l, exp_w_gate, exp_w_up, exp_w_down)
        x = _combine_call(x, sh, wts.T, mod, l, yrows)
    return (x[0:N_CTX].reshape(BATCH, SEQ, D_MODEL), x[N_CTX:].reshape(DEC_BATCH, DEC_SEQ, D_MODEL),
            jnp.stack(new_ak, axis=1), jnp.stack(new_av, axis=1), jnp.stack(new_sr, axis=1),
            jnp.stack(new_sd, axis=1), jnp.stack(new_dk, axis=1), jnp.stack(new_dv, axis=1))
```

```python
import functools
import math
import jax
import jax.numpy as jnp
from jax import lax
from jax.experimental import pallas as pl
from jax.experimental.pallas import tpu as pltpu

D_MODEL = 1024
BATCH = 32
SEQ = 256
DEPTH = 4
DEC_BATCH = 2
DEC_SEQ = 1024
PAST_LEN = 512
GRID_W = 64
NORM_EPS = 1e-6
ROPE_THETA = 10000.0
A_HEADS = 8
A_KV_HEADS = 2
A_GROUP = A_HEADS // A_KV_HEADS
B_HEADS = 8
B_DIM = 64
B_WIDTH = B_HEADS * B_DIM
DECAY_LORA = 64
ICLR_LORA = 64
GATE_LORA = 128
B_COLS = 3 * B_WIDTH + DECAY_LORA + ICLR_LORA + GATE_LORA
LNX_EPS = 64e-5
C_HEADS = 4
C_DIM = 128
C_WIDTH = C_HEADS * C_DIM
D_HEADS = 4
D_VDIM = 128
N_EXPERTS = 64
TOP_K = 6
N_GROUPS = 8
TOPK_GROUPS = 4
PER_GROUP = N_EXPERTS // N_GROUPS
EXPERT_FF = 256
SHARED_FF = 256
ROUTED_SCALE = 1.0

HD = 64
GW = 384
CH = 64
LANE = 128
BM = 128
DMA_UNROLL = 8
TM = 512
SEQ_BLK = 256
N_CTX = BATCH * SEQ
N_LAT = DEC_BATCH * DEC_SEQ
M_TOK = N_CTX + N_LAT
N_MOD = 1 + DEC_BATCH
VMEM_LIMIT = 56 * 1024 * 1024

F32 = jnp.float32
BF16 = jnp.bfloat16
HI = lax.Precision.HIGHEST
NEG = -jnp.inf


def _dot(a, b):
    return jnp.dot(a.astype(BF16), b.astype(BF16), preferred_element_type=F32)


def _dot_nt(a, b):
    return lax.dot_general(a.astype(BF16), b.astype(BF16), (((1,), (1,)), ((), ())), preferred_element_type=F32)


def _dot_tn(a, b):
    return lax.dot_general(a.astype(BF16), b.astype(BF16), (((0,), (0,)), ((), ())), preferred_element_type=F32)


def _dot_hi(a, b):
    return jnp.dot(a, b, preferred_element_type=F32, precision=HI)


def _silu(x):
    return x * jax.nn.sigmoid(x)


def _mod_group(i):
    n_ctx_tiles = N_CTX // TM
    return jnp.where(i < n_ctx_tiles, 0, 1 + (i - n_ctx_tiles) // (DEC_SEQ // TM))


def _full(shape):
    return pl.BlockSpec(shape, lambda *_: (0,) * len(shape))


def _adaln_kernel(c_ref, w_ref, b_ref, o_ref):
    o_ref[0] = _dot(_silu(c_ref[...]), w_ref[0]) + b_ref[0]


def _adaln_call(cond, mod_w, mod_b):
    n = 6
    return pl.pallas_call(
        _adaln_kernel,
        grid=(DEPTH, n),
        in_specs=[_full((8, D_MODEL)),
                  pl.BlockSpec((1, D_MODEL, D_MODEL), lambda l, j: (l, 0, j)),
                  pl.BlockSpec((1, 1, D_MODEL), lambda l, j: (l, 0, j))],
        out_specs=pl.BlockSpec((1, 8, D_MODEL), lambda l, j: (l, 0, j)),
        out_shape=jax.ShapeDtypeStruct((DEPTH, 8, n * D_MODEL), F32),
    )(cond, mod_w, mod_b.reshape(DEPTH, 1, n * D_MODEL))


def _proj_in_kernel(n_out, x_ref, mod_ref, g_ref, *rest):
    x = x_ref[...]
    y = x * lax.rsqrt(jnp.mean(x * x, axis=-1, keepdims=True) + NORM_EPS) * g_ref[...]
    h = (y * (1.0 + mod_ref[0, 0, 1:2, :]) + mod_ref[0, 0, 0:1, :]).astype(BF16)
    for w_ref, o_ref in zip(rest[:n_out], rest[n_out:]):
        o_ref[...] = jnp.dot(h, w_ref[...], preferred_element_type=F32)


def _proj_in_call(x, mod, layer, g, weights):
    n_out = len(weights)
    return pl.pallas_call(
        functools.partial(_proj_in_kernel, n_out),
        grid=(M_TOK // TM,),
        in_specs=[pl.BlockSpec((TM, D_MODEL), lambda i: (i, 0)),
                  pl.BlockSpec((1, 1, 6, D_MODEL), lambda i: (layer, _mod_group(i), 0, 0)),
                  _full((1, D_MODEL))] + [_full(w.shape) for w in weights],
        out_specs=[pl.BlockSpec((TM, w.shape[1]), lambda i: (i, 0)) for w in weights],
        out_shape=[jax.ShapeDtypeStruct((M_TOK, w.shape[1]), F32) for w in weights],
        compiler_params=pltpu.CompilerParams(vmem_limit_bytes=VMEM_LIMIT),
    )(x, mod, g.reshape(1, D_MODEL), *weights)


def _proj_out_kernel(o1_ref, o2_ref, x_ref, mod_ref, g_ref, wo_ref, rwt_ref, sg_ref, su_ref, sd_ref,
                     xn_ref, tok_ref, sh_ref, lg_ref):
    o = jnp.concatenate([o1_ref[...], o2_ref[...]], axis=1).astype(BF16)
    xn = x_ref[...] + mod_ref[0, 0, 2:3, :] * jnp.dot(o, wo_ref[...], preferred_element_type=F32)
    xn_ref[...] = xn
    y = xn * lax.rsqrt(jnp.mean(xn * xn, axis=-1, keepdims=True) + NORM_EPS) * g_ref[...]
    h2 = y * (1.0 + mod_ref[0, 0, 4:5, :]) + mod_ref[0, 0, 3:4, :]
    tok_ref[...] = h2
    hb = h2.astype(BF16)
    lg_ref[...] = _dot_nt(rwt_ref[...], hb)
    hid = _silu(jnp.dot(hb, sg_ref[...], preferred_element_type=F32)) * jnp.dot(hb, su_ref[...], preferred_element_type=F32)
    sh_ref[...] = jnp.dot(hid.astype(BF16), sd_ref[...], preferred_element_type=F32)


def _proj_out_call(o1, o2, x, mod, layer, g2, w_out, rw_t, sg, su, sd):
    row = lambda n: pl.BlockSpec((TM, n), lambda i: (i, 0))
    return pl.pallas_call(
        _proj_out_kernel,
        grid=(M_TOK // TM,),
        in_specs=[row(o1.shape[1]), row(o2.shape[1]), row(D_MODEL),
                  pl.BlockSpec((1, 1, 6, D_MODEL), lambda i: (layer, _mod_group(i), 0, 0)),
                  _full((1, D_MODEL)), _full(w_out.shape), _full(rw_t.shape), _full(sg.shape), _full(su.shape),
                  _full(sd.shape)],
        out_specs=[row(D_MODEL), row(D_MODEL), row(D_MODEL), pl.BlockSpec((N_EXPERTS, TM), lambda i: (0, i))],
        out_shape=[jax.ShapeDtypeStruct((M_TOK, D_MODEL), F32)] * 3 + [jax.ShapeDtypeStruct((N_EXPERTS, M_TOK), F32)],
        compiler_params=pltpu.CompilerParams(vmem_limit_bytes=VMEM_LIMIT),
    )(o1, o2, x, mod, g2.reshape(1, D_MODEL), w_out, rw_t, sg, su, sd)


def _rope_lane_tables(n_tok):
    rows = n_tok // GRID_W
    row = jnp.repeat(jnp.arange(rows, dtype=F32), GRID_W)
    col = jnp.tile(jnp.arange(GRID_W, dtype=F32), rows)
    n_freq = HD // 4
    inv = ROPE_THETA ** (-jnp.arange(n_freq, dtype=F32) / n_freq)
    ang = jnp.concatenate([row[:, None] * inv, col[:, None] * inv], axis=-1)
    cos = jnp.repeat(jnp.cos(ang), 2, axis=-1)
    sin = jnp.repeat(jnp.sin(ang), 2, axis=-1) * jnp.tile(jnp.array([-1.0, 1.0], F32), HD // 2)
    return cos, sin


def _seg_matrix(width, seg, value):
    li = lax.broadcasted_iota(jnp.int32, (width, width), 0) // seg
    lj = lax.broadcasted_iota(jnp.int32, (width, width), 1) // seg
    return jnp.where(li == lj, value, 0.0).astype(F32)


def _norm_rope(x, gain, n_norm, cos, sin):
    lane = lax.broadcasted_iota(jnp.int32, x.shape, 1)
    ms = _dot_hi(x * x, _seg_matrix(x.shape[1], HD, 1.0 / HD))
    xn = jnp.where(lane < n_norm, x * lax.rsqrt(ms + NORM_EPS) * gain, x)
    if cos is None:
        return xn, xn
    w = x.shape[1]
    swapped = jnp.where(lane % 2 == 0, pltpu.roll(xn, w - 1, 1), pltpu.roll(xn, 1, 1))
    return xn, xn * cos + swapped * sin


def _softmax_pv(s, v):
    m = jnp.max(s, axis=-1, keepdims=True)
    p = jnp.exp(s - m)
    l = jnp.sum(p, axis=-1, keepdims=True)
    return _dot(p, v) / l


def _attn_a_kernel(T, P, TQ, *refs):
    if P:
        x_ref, gain_ref, cos_ref, sin_ref, ck_ref, cv_ref, o_ref, xr_s = refs
        xn, xr = _norm_rope(x_ref[...], gain_ref[...], 5 * HD, cos_ref[...], sin_ref[...])
    else:
        x_ref, gain_ref, o_ref, nk_ref, nv_ref, xr_s = refs
        xn, xr = _norm_rope(x_ref[...], gain_ref[...], 5 * HD, None, None)
        nk_ref[0, 0] = xn[:, 4 * HD:5 * HD]
        nv_ref[0, 0] = xn[:, 5 * HD:6 * HD]
    xr_s[...] = xr
    k_new = xr[:, 4 * HD:5 * HD]
    v_new = xr[:, 5 * HD:6 * HD]
    if P:
        k_all = jnp.concatenate([ck_ref[0, 0, 0], k_new], axis=0).astype(BF16)
        v_all = jnp.concatenate([cv_ref[0, 0, 0], v_new], axis=0).astype(BF16)
    else:
        k_all, v_all = k_new.astype(BF16), v_new.astype(BF16)

    def q_block(qb, carry):
        rows = pl.ds(pl.multiple_of(qb * TQ, TQ), TQ)
        qx = xr_s[rows, 0:A_GROUP * HD]
        qs = jnp.concatenate([qx[:, i * HD:(i + 1) * HD] for i in range(A_GROUP)], axis=0)
        o = _softmax_pv(_dot_nt(qs, k_all) * (HD ** -0.5), v_all)
        o_ref[rows, :] = jnp.concatenate([o[i * TQ:(i + 1) * TQ, :] for i in range(A_GROUP)], axis=1)
        return carry

    lax.fori_loop(0, T // TQ, q_block, 0)


def _attn_a_call(amat, row_off_blocks, n_seq, T, gain, cache=None):
    P = 0 if cache is None else cache[0].shape[3]
    TQ = T if P == 0 else 128
    in_specs = [pl.BlockSpec((T, GW), lambda i, g: (row_off_blocks + i, g)), _full((1, GW))]
    args = [amat, gain]
    out_specs = [pl.BlockSpec((T, A_GROUP * HD), lambda i, g: (i, g))]
    out_shape = [jax.ShapeDtypeStruct((n_seq * T, A_HEADS * HD), F32)]
    if P:
        ck, cv, j = cache
        cos, sin = _rope_lane_tables(T)
        cos_g = jnp.concatenate([cos] * 5 + [jnp.ones((T, HD), F32)], axis=1)
        sin_g = jnp.concatenate([sin] * 5 + [jnp.zeros((T, HD), F32)], axis=1)
        in_specs += [_full((T, GW)), _full((T, GW)),
                     pl.BlockSpec((1, 1, 1, P, HD), lambda i, g: (i, j, g, 0, 0)),
                     pl.BlockSpec((1, 1, 1, P, HD), lambda i, g: (i, j, g, 0, 0))]
        args += [cos_g, sin_g, ck, cv]
    else:
        out_specs += [pl.BlockSpec((1, 1, T, HD), lambda i, g: (i, g, 0, 0))] * 2
        out_shape += [jax.ShapeDtypeStruct((n_seq, A_KV_HEADS, T, HD), F32)] * 2
    return pl.pallas_call(
        functools.partial(_attn_a_kernel, T, P, TQ),
        grid=(n_seq, A_KV_HEADS),
        in_specs=in_specs, out_specs=out_specs, out_shape=out_shape,
        scratch_shapes=[pltpu.VMEM((T, GW), F32)],
        compiler_params=pltpu.CompilerParams(vmem_limit_bytes=VMEM_LIMIT),
    )(*args)


def _attn_d_kernel(T, P, TQ, lam_init, *refs):
    if P:
        x_ref, gain_ref, lam_ref, sub_ref, cos_ref, sin_ref, ck_ref, cv_ref, o_ref, xr_s = refs
        xn, xr = _norm_rope(x_ref[...], gain_ref[...], 4 * HD, cos_ref[...], sin_ref[...])
    else:
        x_ref, gain_ref, lam_ref, sub_ref, o_ref, nk_ref, nv_ref, xr_s = refs
        xn, xr = _norm_rope(x_ref[...], gain_ref[...], 4 * HD, None, None)
        nk_ref[0, 0, 0] = xn[:, 2 * HD:3 * HD]
        nk_ref[0, 0, 1] = xn[:, 3 * HD:4 * HD]
        nv_ref[0, 0] = xn[:, 4 * HD:]
    xr_s[...] = xr
    v_new = xr[:, 4 * HD:]
    ks = []
    for m in range(2):
        k_new = xr[:, (2 + m) * HD:(3 + m) * HD]
        if P:
            ks.append(jnp.concatenate([ck_ref[0, 0, 0, m], k_new], axis=0).astype(BF16))
        else:
            ks.append(k_new.astype(BF16))
    v_all = (jnp.concatenate([cv_ref[0, 0, 0], v_new], axis=0) if P else v_new).astype(BF16)
    lm = lam_ref[...]
    lam = (jnp.exp(jnp.sum(lm[0:1, :] * lm[1:2, :], axis=-1, keepdims=True))
           - jnp.exp(jnp.sum(lm[2:3, :] * lm[3:4, :], axis=-1, keepdims=True)) + lam_init)

    def q_block(qb, carry):
        rows = pl.ds(pl.multiple_of(qb * TQ, TQ), TQ)
        qx = xr_s[rows, 0:2 * HD]
        o1 = _softmax_pv(_dot_nt(qx[:, 0:HD], ks[0]) * (HD ** -0.5), v_all)
        o2 = _softmax_pv(_dot_nt(qx[:, HD:], ks[1]) * (HD ** -0.5), v_all)
        od = o1 - lam * o2
        od = od * lax.rsqrt(jnp.mean(od * od, axis=-1, keepdims=True) + NORM_EPS) * sub_ref[...]
        o_ref[rows, :] = od * (1.0 - lam_init)
        return carry

    lax.fori_loop(0, T // TQ, q_block, 0)


def _attn_d_call(dmat, row_off_blocks, n_seq, T, gain, lam, subln, lam_init, cache=None):
    P = 0 if cache is None else cache[0].shape[4]
    TQ = T if P == 0 else 256
    in_specs = [pl.BlockSpec((T, GW), lambda i, h: (row_off_blocks + i, h)),
                _full((1, GW)), _full((4, HD)), _full((1, D_VDIM))]
    args = [dmat, gain, lam, subln.reshape(1, -1)]
    out_specs = [pl.BlockSpec((T, D_VDIM), lambda i, h: (i, h))]
    out_shape = [jax.ShapeDtypeStruct((n_seq * T, D_HEADS * D_VDIM), F32)]
    if P:
        ck, cv, j = cache
        cos, sin = _rope_lane_tables(T)
        cos_g = jnp.concatenate([cos] * 4 + [jnp.ones((T, D_VDIM), F32)], axis=1)
        sin_g = jnp.concatenate([sin] * 4 + [jnp.zeros((T, D_VDIM), F32)], axis=1)
        in_specs += [_full((T, GW)), _full((T, GW)),
                     pl.BlockSpec((1, 1, 1, 2, P, HD), lambda i, h: (i, j, h, 0, 0, 0)),
                     pl.BlockSpec((1, 1, 1, P, D_VDIM), lambda i, h: (i, j, h, 0, 0))]
        args += [cos_g, sin_g, ck, cv]
    else:
        out_specs += [pl.BlockSpec((1, 1, 2, T, HD), lambda i, h: (i, h, 0, 0, 0)),
                      pl.BlockSpec((1, 1, T, D_VDIM), lambda i, h: (i, h, 0, 0))]
        out_shape += [jax.ShapeDtypeStruct((n_seq, D_HEADS, 2, T, HD), F32),
                      jax.ShapeDtypeStruct((n_seq, D_HEADS, T, D_VDIM), F32)]
    return pl.pallas_call(
        functools.partial(_attn_d_kernel, T, P, TQ, lam_init),
        grid=(n_seq, D_HEADS),
        in_specs=in_specs, out_specs=out_specs, out_shape=out_shape,
        scratch_shapes=[pltpu.VMEM((T, GW), F32)],
        compiler_params=pltpu.CompilerParams(vmem_limit_bytes=VMEM_LIMIT),
    )(*args)


def _dot_3pass(a, b):
    a_hi = a.astype(BF16)
    a_lo = (a - a_hi.astype(F32)).astype(BF16)
    b_hi = b.astype(BF16)
    b_lo = (b - b_hi.astype(F32)).astype(BF16)
    d = lambda x, y: jnp.dot(x, y, preferred_element_type=F32)
    return d(a_hi, b_hi) + (d(a_hi, b_lo) + d(a_lo, b_hi))


def _unit_lower_solve(am, xm, apply_dot, square_dot):
    n_stage = CH.bit_length() - 1
    for s in range(n_stage):
        xm = xm + apply_dot(am, xm)
        if s < n_stage - 1:
            am = square_dot(am, am)
    return xm


def _head_stack(x, n_heads, width):
    return jnp.concatenate([x[:, h * width:(h + 1) * width] for h in range(n_heads)], axis=0)


def _head_unstack(x, n_heads):
    return jnp.concatenate([x[h * CH:(h + 1) * CH, :] for h in range(n_heads)], axis=1)


def _head_masks(n_heads, d):
    hc = n_heads * CH
    ri = lax.broadcasted_iota(jnp.int32, (hc, hc), 0)
    rj = lax.broadcasted_iota(jnp.int32, (hc, hc), 1)
    same = (ri // CH) == (rj // CH)
    return (same & (ri >= rj), same & (ri > rj)) if d == 0 else (same & (ri <= rj), same & (ri < rj))


def _chunk_tri(d):
    ci = lax.broadcasted_iota(jnp.int32, (CH, CH), 0)
    cj = lax.broadcasted_iota(jnp.int32, (CH, CH), 1)
    return jnp.where(ci >= cj if d == 0 else ci <= cj, 1.0, 0.0).astype(F32)


def _rwkv_kernel(T, has_state, *refs):
    if has_state:
        (xb_ref, mu_ref, w0_ref, w2_ref, a0_ref, a2_ref, g2_ref, kk_ref, ka_ref, rk_ref, lnw_ref, lnb_ref, s0_ref,
         ob_ref, sf_ref, r_s, v_s, av_s, lw_s, kd_s, bv_s, y_s, bon_s, gate_s, st_s, mbd_s, mst_s, min_s) = refs
    else:
        (xb_ref, mu_ref, w0_ref, w2_ref, a0_ref, a2_ref, g2_ref, kk_ref, ka_ref, rk_ref, lnw_ref, lnb_ref,
         ob_ref, sf_ref, r_s, v_s, av_s, lw_s, kd_s, bv_s, y_s, bon_s, gate_s, st_s, mbd_s, mst_s, min_s) = refs
    n_chunks = T // CH
    n_blk = T // SEQ_BLK
    hc = B_HEADS * CH
    seg = _seg_matrix(B_WIDTH, B_DIM, 1.0)
    row = lax.broadcasted_iota(jnp.int32, (SEQ_BLK, 1), 0)
    for b in range(n_blk):
        r0 = b * SEQ_BLK
        rb = slice(r0, r0 + SEQ_BLK)
        x = xb_ref[rb, :]
        before = xb_ref[r0 - 1:r0, :] if b > 0 else jnp.zeros((1, B_COLS), F32)
        after = xb_ref[r0 + SEQ_BLK:r0 + SEQ_BLK + 1, :] if b < n_blk - 1 else jnp.zeros((1, B_COLS), F32)
        prev = jnp.where(row == 0, before, pltpu.roll(x, 1, 0))
        nxt = jnp.where(row == SEQ_BLK - 1, after, pltpu.roll(x, SEQ_BLK - 1, 0))
        xs = x + mu_ref[...] * (0.5 * (prev + nxt) - x)
        r = xs[:, 0:B_WIDTH]
        kb = xs[:, B_WIDTH:2 * B_WIDTH]
        vb = xs[:, 2 * B_WIDTH:3 * B_WIDTH]
        wd = xs[:, 3 * B_WIDTH:3 * B_WIDTH + DECAY_LORA]
        ad = xs[:, 3 * B_WIDTH + DECAY_LORA:3 * B_WIDTH + DECAY_LORA + ICLR_LORA]
        gd = xs[:, 3 * B_WIDTH + DECAY_LORA + ICLR_LORA:]
        kk = kb * kk_ref[...]
        kkn = kk * lax.rsqrt(_dot_hi(kk * kk, seg) + 1e-6)
        r_s[rb, :] = r
        v_s[rb, :] = vb
        av_s[rb, :] = -kkn
        bon_s[rb, :] = _dot_hi(r * kb * rk_ref[...], seg) * vb
        gate_s[rb, :] = _dot(jax.nn.sigmoid(gd), g2_ref[...])
        twd = jnp.tanh(wd)
        for d in range(2):
            wl = w0_ref[d:d + 1, :] + _dot(twd, w2_ref[d])
            w_log = -jax.nn.softplus(-wl) - 0.5
            lw_s[d, rb, :] = -jnp.exp(w_log)
            a = jax.nn.sigmoid(a0_ref[d:d + 1, :] + _dot(ad, a2_ref[d]))
            kd_s[d, rb, :] = kb * (1.0 + (a - 1.0) * ka_ref[...])
            bv_s[d, rb, :] = kkn * a
    if has_state:
        for d in range(2):
            st_s[d] = jnp.concatenate([s0_ref[0, d, h] for h in range(B_HEADS)], axis=1)
    else:
        st_s[...] = jnp.zeros_like(st_s)
    incl0, _ = _head_masks(B_HEADS, 0)
    incl1, _ = _head_masks(B_HEADS, 1)
    mbd_s[...] = jnp.where(incl0 | incl1, 1.0, 0.0).astype(F32)

    def expand(x):
        return jnp.concatenate([x] * B_HEADS, axis=1) * mbd_s[...]

    for d in range(2):
        incl, strict = _head_masks(B_HEADS, d)
        min_s[...] = jnp.where(incl, 1.0, 0.0).astype(F32)
        mst_s[...] = jnp.where(strict, 1.0, 0.0).astype(F32)
        tri = _chunk_tri(d)
        last = CH - 1 if d == 0 else 0

        def chunk_body(it, carry, d=d, tri=tri, last=last):
            c = it if d == 0 else n_chunks - 1 - it
            rows = pl.ds(pl.multiple_of(c * CH, CH), CH)
            lwc = lw_s[d, rows, :]
            cum = _dot_hi(tri, lwc)
            e_pos = jnp.exp(cum)
            e_neg = jnp.exp(-cum)
            rt = _head_stack(r_s[rows, :] * e_pos, B_HEADS, B_DIM)
            at = _head_stack(av_s[rows, :] * jnp.exp(cum - lwc), B_HEADS, B_DIM)
            bk = jnp.concatenate([_head_stack(bv_s[d, rows, :] * e_neg, B_HEADS, B_DIM),
                                  _head_stack(kd_s[d, rows, :] * e_neg, B_HEADS, B_DIM)], axis=0)
            vs = _head_stack(v_s[rows, :], B_HEADS, B_DIM)
            pc = e_pos[last:last + 1, :]
            g1 = _dot_nt(at, bk)
            g2 = _dot_nt(rt, bk)
            mst = mst_s[...]
            a_ab = g1[:, 0:hc] * mst
            a_ak = g1[:, hc:] * mst
            mi = min_s[...]
            a_r = jnp.concatenate([g2[:, 0:hc] * mi, g2[:, hc:] * mi], axis=1)
            xm = _unit_lower_solve(a_ab, jnp.concatenate([at, _dot(a_ak, vs)], axis=1),
                                   _dot_3pass, _dot_3pass)
            s_old = st_s[d]
            ws = _dot_nt(jnp.concatenate([expand(xm[:, 0:B_DIM]), expand(rt)], axis=0), s_old)
            uv = jnp.concatenate([ws[0:hc, :] + xm[:, B_DIM:], vs], axis=0)
            ys = ws[hc:, :] + _dot(a_r, uv)
            bk_e = jnp.concatenate([expand(bk[0:hc, :]), expand(bk[hc:, :])], axis=0)
            st_s[d] = (s_old + _dot_tn(uv, bk_e)) * pc
            yc = _head_unstack(ys, B_HEADS)
            if d == 0:
                y_s[rows, :] = yc
            else:
                y_s[rows, :] = y_s[rows, :] + yc
            return carry

        lax.fori_loop(0, n_chunks, chunk_body, 0)
    for b in range(n_blk):
        rb = slice(b * SEQ_BLK, (b + 1) * SEQ_BLK)
        y = y_s[rb, :]
        yc = y - _dot_hi(y, seg) * (1.0 / B_DIM)
        var = _dot_hi(yc * yc, seg) * (1.0 / B_DIM)
        yn = yc * lax.rsqrt(var + LNX_EPS) * lnw_ref[...] + lnb_ref[...]
        ob_ref[rb, :] = (yn + bon_s[rb, :]) * gate_s[rb, :]
    for d in range(2):
        sd = st_s[d]
        for h in range(B_HEADS):
            sf_ref[0, d, h] = sd[:, h * B_DIM:(h + 1) * B_DIM]


def _seq_spec(T, cols, row_off_blocks):
    extra = {'pipeline_mode': pl.Buffered(1)} if T > SEQ_BLK else {}
    return pl.BlockSpec((T, cols), lambda i: (row_off_blocks + i, 0), **extra)


def _rwkv_call(xb, row_off_blocks, n_seq, T, p, s0=None):
    has_state = s0 is not None
    st_shape = (2, B_HEADS, B_DIM, B_DIM)
    hc = B_HEADS * CH
    in_specs = [_seq_spec(T, B_COLS, row_off_blocks),
                _full((1, B_COLS)), _full((2, B_WIDTH)), _full((2, DECAY_LORA, B_WIDTH)), _full((2, B_WIDTH)),
                _full((2, ICLR_LORA, B_WIDTH)), _full((GATE_LORA, B_WIDTH))] + [_full((1, B_WIDTH))] * 5
    args = [xb, p['b_mu'].reshape(1, -1), p['b_w0'], p['b_w2'], p['b_a0'], p['b_a2'], p['b_g2'],
            p['b_kk'].reshape(1, -1), p['b_ka'].reshape(1, -1), p['b_rk'].reshape(1, -1),
            p['b_lnx_w'].reshape(1, -1), p['b_lnx_b'].reshape(1, -1)]
    if has_state:
        in_specs.append(pl.BlockSpec((1,) + st_shape, lambda i: (i, 0, 0, 0, 0)))
        args.append(s0)
    scr = [pltpu.VMEM((T, B_WIDTH), F32)] * 3 + [pltpu.VMEM((2, T, B_WIDTH), F32)] * 3 + \
          [pltpu.VMEM((T, B_WIDTH), F32)] * 3 + [pltpu.VMEM((2, B_DIM, B_WIDTH), F32)] + \
          [pltpu.VMEM((hc, hc), F32)] * 3
    return pl.pallas_call(
        functools.partial(_rwkv_kernel, T, has_state),
        grid=(n_seq,),
        in_specs=in_specs,
        out_specs=[pl.BlockSpec((T, B_WIDTH), lambda i: (i, 0)),
                   pl.BlockSpec((1,) + st_shape, lambda i: (i, 0, 0, 0, 0))],
        out_shape=[jax.ShapeDtypeStruct((n_seq * T, B_WIDTH), F32),
                   jax.ShapeDtypeStruct((n_seq,) + st_shape, F32)],
        scratch_shapes=scr,
        compiler_params=pltpu.CompilerParams(vmem_limit_bytes=VMEM_LIMIT),
    )(*args)


def _delta_kernel(T, has_state, *refs):
    if has_state:
        (c_ref, ab_ref, conv_ref, arow_ref, dtrow_ref, on_ref, s0_ref, oc_ref, sf_ref,
         q_s, k_s, v_s, g_s, b_s, o_s, st_s) = refs
    else:
        (c_ref, ab_ref, conv_ref, arow_ref, dtrow_ref, on_ref, oc_ref, sf_ref,
         q_s, k_s, v_s, g_s, b_s, o_s, st_s) = refs
    n_chunks = T // CH
    n_blk = T // SEQ_BLK
    hc = C_HEADS * CH
    row = lax.broadcasted_iota(jnp.int32, (SEQ_BLK, 1), 0)
    for b in range(n_blk):
        r0 = b * SEQ_BLK
        rb = slice(r0, r0 + SEQ_BLK)
        x = c_ref[rb, 0:3 * C_WIDTH]
        before = c_ref[r0 - 1:r0, 0:3 * C_WIDTH] if b > 0 else jnp.zeros((1, 3 * C_WIDTH), F32)
        after = c_ref[r0 + SEQ_BLK:r0 + SEQ_BLK + 1, 0:3 * C_WIDTH] if b < n_blk - 1 else jnp.zeros((1, 3 * C_WIDTH), F32)
        prev = jnp.where(row == 0, before, pltpu.roll(x, 1, 0))
        nxt = jnp.where(row == SEQ_BLK - 1, after, pltpu.roll(x, SEQ_BLK - 1, 0))
        xc = _silu(conv_ref[0:1, :] * prev + conv_ref[1:2, :] * x + conv_ref[2:3, :] * nxt)
        for h in range(C_HEADS):
            sl = slice(h * C_DIM, (h + 1) * C_DIM)
            qh = xc[:, h * C_DIM:(h + 1) * C_DIM]
            kh = xc[:, C_WIDTH + h * C_DIM:C_WIDTH + (h + 1) * C_DIM]
            q_s[rb, sl] = qh * lax.rsqrt(jnp.sum(qh * qh, axis=-1, keepdims=True) + 1e-6) * (C_DIM ** -0.5)
            k_s[rb, sl] = kh * lax.rsqrt(jnp.sum(kh * kh, axis=-1, keepdims=True) + 1e-6)
        v_s[rb, :] = xc[:, 2 * C_WIDTH:]
        ab = ab_ref[rb, :]
        g_s[rb, :] = arow_ref[...] * jax.nn.softplus(ab + dtrow_ref[...])
        b_s[rb, :] = jax.nn.sigmoid(ab)
    if has_state:
        for d in range(2):
            st_s[d] = jnp.concatenate([s0_ref[0, d, h] for h in range(C_HEADS)], axis=0)
    else:
        st_s[...] = jnp.zeros_like(st_s)
    ei = lax.broadcasted_iota(jnp.int32, (hc, C_WIDTH), 0) // CH
    ej = lax.broadcasted_iota(jnp.int32, (hc, C_WIDTH), 1) // C_DIM
    own = ei == ej

    def expand(x):
        return jnp.where(own, jnp.concatenate([x] * C_HEADS, axis=1), 0.0)

    ri = lax.broadcasted_iota(jnp.int32, (hc, hc), 0)
    rj = lax.broadcasted_iota(jnp.int32, (hc, hc), 1)
    eye = jnp.where(ri == rj, 1.0, 0.0).astype(F32)
    for d in range(2):
        incl, strict = _head_masks(C_HEADS, d)
        tri = _chunk_tri(d)
        last = CH - 1 if d == 0 else 0
        j0 = d * C_HEADS

        def chunk_body(it, carry, d=d, incl=incl, strict=strict, tri=tri, last=last, j0=j0):
            c = it if d == 0 else n_chunks - 1 - it
            rows = pl.ds(pl.multiple_of(c * CH, CH), CH)
            gcum = _dot_hi(tri, g_s[rows, :])
            bet = b_s[rows, :]
            heads = range(C_HEADS)
            g_col = jnp.concatenate([gcum[:, j0 + h:j0 + h + 1] for h in heads], axis=0)
            b_col = jnp.concatenate([bet[:, 2 * C_HEADS + j0 + h:2 * C_HEADS + j0 + h + 1] for h in heads], axis=0)
            g_end = [gcum[last:last + 1, j0 + h:j0 + h + 1] for h in heads]
            g_last = jnp.concatenate([jnp.broadcast_to(g, (CH, 1)) for g in g_end], axis=0)
            e_last = jnp.concatenate([jnp.broadcast_to(jnp.exp(g), (C_DIM, 1)) for g in g_end], axis=0)
            g_row = jnp.sum(eye * g_col, axis=0, keepdims=True)
            decay = jnp.where(incl, jnp.exp(jnp.where(incl, g_col - g_row, 0.0)), 0.0)
            qs = _head_stack(q_s[rows, :], C_HEADS, C_DIM)
            ks = _head_stack(k_s[rows, :], C_HEADS, C_DIM)
            vs = _head_stack(v_s[rows, :], C_HEADS, C_DIM)
            kbeta = ks * b_col
            gm = _dot_nt(jnp.concatenate([kbeta, qs], axis=0), ks)
            attn = gm[hc:, :] * decay
            eg = jnp.exp(g_col)
            xm = _unit_lower_solve(-jnp.where(strict, gm[0:hc, :] * decay, 0.0),
                                   jnp.concatenate([vs * b_col, kbeta * eg], axis=1),
                                   _dot_3pass, _dot_3pass)
            s_old = st_s[d]
            ws = _dot(jnp.concatenate([expand(xm[:, C_DIM:]), expand(qs * eg)], axis=0), s_old)
            v_new = xm[:, 0:C_DIM] - ws[0:hc, :]
            o = ws[hc:, :] + _dot(attn, v_new)
            st_s[d] = s_old * e_last + _dot_tn(expand(ks * jnp.exp(g_last - g_col)), v_new)
            oc = _head_unstack(o, C_HEADS)
            if d == 0:
                o_s[rows, :] = oc
            else:
                o_s[rows, :] = o_s[rows, :] + oc
            return carry

        lax.fori_loop(0, n_chunks, chunk_body, 0)
    for b in range(n_blk):
        rb = slice(b * SEQ_BLK, (b + 1) * SEQ_BLK)
        z = c_ref[rb, 3 * C_WIDTH:]
        for h in range(C_HEADS):
            sl = slice(h * C_DIM, (h + 1) * C_DIM)
            oh = o_s[rb, sl]
            on = oh * lax.rsqrt(jnp.mean(oh * oh, axis=-1, keepdims=True) + NORM_EPS) * on_ref[...]
            oc_ref[rb, sl] = on * _silu(z[:, sl])
    for d in range(2):
        sd = st_s[d]
        for h in range(C_HEADS):
            sf_ref[0, d, h] = sd[h * C_DIM:(h + 1) * C_DIM, :]


def _delta_call(cmat, abmat, row_off_blocks, n_seq, T, p, s0=None):
    has_state = s0 is not None
    st_shape = (2, C_HEADS, C_DIM, C_DIM)
    arow = jnp.zeros((1, LANE), F32).at[0, 0:2 * C_HEADS].set(-jnp.exp(p['c_A_log'].reshape(-1)))
    dtrow = jnp.zeros((1, LANE), F32).at[0, 0:2 * C_HEADS].set(p['c_dt_bias'].reshape(-1))
    in_specs = [pl.BlockSpec((T, 4 * C_WIDTH), lambda i: (row_off_blocks + i, 0)),
                pl.BlockSpec((T, LANE), lambda i: (row_off_blocks + i, 0)),
                _full((3, 3 * C_WIDTH)), _full((1, LANE)), _full((1, LANE)), _full((1, C_DIM))]
    args = [cmat, abmat, p['c_conv'], arow, dtrow, p['c_onorm'].reshape(1, -1)]
    if has_state:
        in_specs.append(pl.BlockSpec((1,) + st_shape, lambda i: (i, 0, 0, 0, 0)))
        args.append(s0)
    scr = [pltpu.VMEM((T, C_WIDTH), F32)] * 3 + [pltpu.VMEM((T, LANE), F32)] * 2 + \
          [pltpu.VMEM((T, C_WIDTH), F32), pltpu.VMEM((2, C_HEADS * C_DIM, C_DIM), F32)]
    return pl.pallas_call(
        functools.partial(_delta_kernel, T, has_state),
        grid=(n_seq,),
        in_specs=in_specs,
        out_specs=[pl.BlockSpec((T, C_WIDTH), lambda i: (i, 0)),
                   pl.BlockSpec((1,) + st_shape, lambda i: (i, 0, 0, 0, 0))],
        out_shape=[jax.ShapeDtypeStruct((n_seq * T, C_WIDTH), F32),
                   jax.ShapeDtypeStruct((n_seq,) + st_shape, F32)],
        scratch_shapes=scr,
        compiler_params=pltpu.CompilerParams(vmem_limit_bytes=VMEM_LIMIT),
    )(*args)


def _first_max(x, iota, size):
    m = jnp.max(x, axis=0, keepdims=True)
    idx = jnp.min(jnp.where(x == m, iota, size), axis=0, keepdims=True)
    return m, idx


def _route_kernel(lg_ref, bias_ref, idx_ref, wts_ref):
    n = lg_ref.shape[1]
    scores = jax.nn.sigmoid(lg_ref[...])
    biased = scores + bias_ref[...]
    e_iota = lax.broadcasted_iota(jnp.int32, (N_EXPERTS, n), 0)
    g_iota = lax.broadcasted_iota(jnp.int32, (PER_GROUP, n), 0)
    gs = []
    for g in range(N_GROUPS):
        xg = biased[g * PER_GROUP:(g + 1) * PER_GROUP, :]
        m1, i1 = _first_max(xg, g_iota, PER_GROUP)
        m2 = jnp.max(jnp.where(g_iota == i1, NEG, xg), axis=0, keepdims=True)
        gs.append(m1 + m2)
    gscore = jnp.concatenate(gs, axis=0)
    gi = lax.broadcasted_iota(jnp.int32, (N_GROUPS, n), 0)
    gsel = jnp.zeros((N_GROUPS, n), F32)
    for _ in range(TOPK_GROUPS):
        _, ig = _first_max(gscore, gi, N_GROUPS)
        hit = gi == ig
        gsel = jnp.where(hit, 1.0, gsel)
        gscore = jnp.where(hit, NEG, gscore)
    masked = jnp.concatenate(
        [jnp.where(gsel[g:g + 1, :] > 0.0, biased[g * PER_GROUP:(g + 1) * PER_GROUP, :], NEG)
         for g in range(N_GROUPS)], axis=0)
    ids, ws = [], []
    for _ in range(TOP_K):
        _, ie = _first_max(masked, e_iota, N_EXPERTS)
        hit = e_iota == ie
        ids.append(ie)
        ws.append(jnp.sum(jnp.where(hit, scores, 0.0), axis=0, keepdims=True))
        masked = jnp.where(hit, NEG, masked)
    wsum = ws[0]
    for w in ws[1:]:
        wsum = wsum + w
    inv = ROUTED_SCALE / (wsum + 1e-20)
    idx_ref[...] = jnp.concatenate(ids + [jnp.zeros((8 - TOP_K, n), jnp.int32)], axis=0)
    wts_ref[...] = jnp.concatenate([w * inv for w in ws] + [jnp.zeros((8 - TOP_K, n), F32)], axis=0)


def _route_call(logits_t, bias):
    m = logits_t.shape[1]
    return pl.pallas_call(
        _route_kernel,
        grid=(m // TM,),
        in_specs=[pl.BlockSpec((N_EXPERTS, TM), lambda i: (0, i)), _full((N_EXPERTS, 1))],
        out_specs=[pl.BlockSpec((8, TM), lambda i: (0, i))] * 2,
        out_shape=[jax.ShapeDtypeStruct((8, m), jnp.int32), jax.ShapeDtypeStruct((8, m), F32)],
    )(logits_t, bias.reshape(N_EXPERTS, 1))


def _dispatch_tables(idx, m):
    n_asg = m * TOP_K
    nb = -(-n_asg // BM) + N_EXPERTS
    n_pad = nb * BM - n_asg
    flat_e = idx[0:TOP_K, :].reshape(-1)
    e_iota = jnp.arange(N_EXPERTS, dtype=jnp.int32)
    counts = jnp.sum((flat_e[:, None] == e_iota[None, :]).astype(jnp.int32), axis=0)
    padded = (counts + BM - 1) // BM * BM
    pad_end = jnp.cumsum(padded)
    pad_cum = jnp.cumsum(padded - counts)
    pad_e = jnp.sum((pad_cum[None, :] <= jnp.arange(n_pad, dtype=jnp.int32)[:, None]).astype(jnp.int32), axis=1)
    keys = jnp.concatenate([flat_e * 2, pad_e * 2 + 1])
    vals = jnp.concatenate([jnp.arange(n_asg, dtype=jnp.int32), jnp.full((n_pad,), -1, jnp.int32)])
    _, slot_asg = lax.sort((keys, vals), num_keys=1, is_stable=True)
    valid = slot_asg >= 0
    spare = TOP_K * m + jnp.arange(nb * BM, dtype=jnp.int32) % BM
    slot_dst = jnp.where(valid, slot_asg, spare)
    slot_tok = jnp.where(valid, slot_asg % m, 0)
    blk0 = jnp.arange(nb, dtype=jnp.int32) * BM
    block_expert = jnp.minimum(jnp.sum((pad_end[None, :] <= blk0[:, None]).astype(jnp.int32), axis=1), N_EXPERTS - 1)
    n_used = (pad_end[-1] // BM).astype(jnp.int32).reshape(1)
    return slot_tok.reshape(nb, 1, BM), slot_dst.reshape(nb, 1, BM), block_expert.astype(jnp.int32), n_used


def _expert_kernel(be_ref, nu_ref, st_ref, stn_ref, sd_ref, tok_hbm, wg_ref, wu_ref, wd_ref, y_hbm,
                   xbuf, ybuf, sem_in, sem_out):
    i = pl.program_id(0)
    n_used = nu_ref[0]
    slot = i % 2

    def gather(tab_ref, b):
        def body(r, c):
            pltpu.make_async_copy(tok_hbm.at[pl.ds(tab_ref[0, 0, r], 1), :], xbuf.at[b, pl.ds(r, 1), :],
                                  sem_in.at[b]).start()
            return c
        lax.fori_loop(0, BM, body, 0, unroll=DMA_UNROLL)

    def wait_gather(b):
        pltpu.make_async_copy(tok_hbm.at[pl.ds(0, BM), :], xbuf.at[b], sem_in.at[b]).wait()

    def wait_scatter(b):
        pltpu.make_async_copy(ybuf.at[b], y_hbm.at[pl.ds(0, BM), :], sem_out.at[b]).wait()

    @pl.when(i == 0)
    def _():
        ybuf[0] = jnp.zeros((BM, D_MODEL), F32)
        spare = pltpu.make_async_copy(ybuf.at[0], y_hbm.at[pl.ds(y_hbm.shape[0] - BM, BM), :], sem_out.at[0])
        spare.start()
        spare.wait()
        gather(st_ref, 0)

    @pl.when(i + 1 < n_used)
    def _():
        gather(stn_ref, 1 - slot)

    @pl.when(i < n_used)
    def _():
        wait_gather(slot)

        @pl.when(i >= 2)
        def _():
            wait_scatter(slot)

        x = xbuf[slot].astype(BF16)
        g = jnp.dot(x, wg_ref[0, 0].astype(BF16), preferred_element_type=F32)
        u = jnp.dot(x, wu_ref[0, 0].astype(BF16), preferred_element_type=F32)
        ybuf[slot] = jnp.dot((_silu(g) * u).astype(BF16), wd_ref[0, 0].astype(BF16), preferred_element_type=F32)

        def body(r, c):
            pltpu.make_async_copy(ybuf.at[slot, pl.ds(r, 1), :], y_hbm.at[pl.ds(sd_ref[0, 0, r], 1), :],
                                  sem_out.at[slot]).start()
            return c
        lax.fori_loop(0, BM, body, 0, unroll=DMA_UNROLL)

        @pl.when(i == n_used - 1)
        def _():
            wait_scatter(slot)

            @pl.when(i >= 1)
            def _():
                wait_scatter(1 - slot)


def _expert_call(tok, tables, layer, wg, wu, wd):
    m = tok.shape[0]
    slot_tok, slot_dst, block_expert, n_used = tables
    nb = slot_tok.shape[0]
    smem_blk = lambda f: pl.BlockSpec((1, 1, BM), lambda i, be, nu: (f(i), 0, 0), memory_space=pltpu.SMEM)
    grid_spec = pltpu.PrefetchScalarGridSpec(
        num_scalar_prefetch=2,
        grid=(nb,),
        in_specs=[smem_blk(lambda i: i), smem_blk(lambda i: jnp.minimum(i + 1, nb - 1)), smem_blk(lambda i: i),
                  pl.BlockSpec(memory_space=pl.ANY),
                  pl.BlockSpec((1, 1, D_MODEL, EXPERT_FF), lambda i, be, nu: (layer, be[i], 0, 0)),
                  pl.BlockSpec((1, 1, D_MODEL, EXPERT_FF), lambda i, be, nu: (layer, be[i], 0, 0)),
                  pl.BlockSpec((1, 1, EXPERT_FF, D_MODEL), lambda i, be, nu: (layer, be[i], 0, 0))],
        out_specs=pl.BlockSpec(memory_space=pl.ANY),
        scratch_shapes=[pltpu.VMEM((2, BM, D_MODEL), F32), pltpu.VMEM((2, BM, D_MODEL), F32),
                        pltpu.SemaphoreType.DMA((2,)), pltpu.SemaphoreType.DMA((2,))])
    return pl.pallas_call(
        _expert_kernel, grid_spec=grid_spec,
        out_shape=jax.ShapeDtypeStruct((TOP_K * m + BM, D_MODEL), F32),
        compiler_params=pltpu.CompilerParams(vmem_limit_bytes=VMEM_LIMIT),
    )(block_expert, n_used, slot_tok, slot_tok, slot_dst, tok, wg, wu, wd)


def _combine_kernel(x_ref, sh_ref, w_ref, mod_ref, *rest):
    y_refs, o_ref = rest[:TOP_K], rest[TOP_K]
    w = w_ref[...]
    acc = sh_ref[...]
    for k, y_ref in enumerate(y_refs):
        acc = acc + w[:, k:k + 1] * y_ref[...]
    o_ref[...] = x_ref[...] + mod_ref[0, 0, 5:6, :] * acc


def _combine_call(x, sh, wts_rows, mod, layer, yrows):
    nt = M_TOK // TM
    row = pl.BlockSpec((TM, D_MODEL), lambda i: (i, 0))
    ysp = [pl.BlockSpec((TM, D_MODEL), (lambda i, k=k: (k * nt + i, 0))) for k in range(TOP_K)]
    return pl.pallas_call(
        _combine_kernel,
        grid=(nt,),
        in_specs=[row, row, pl.BlockSpec((TM, 8), lambda i: (i, 0)),
                  pl.BlockSpec((1, 1, 6, D_MODEL), lambda i: (layer, _mod_group(i), 0, 0))] + ysp,
        out_specs=row,
        out_shape=jax.ShapeDtypeStruct((M_TOK, D_MODEL), F32),
        compiler_params=pltpu.CompilerParams(vmem_limit_bytes=VMEM_LIMIT),
    )(x, sh, wts_rows, mod, *([yrows] * TOP_K))


def kernel(x_prompt, x_sample, cache_attn_k, cache_attn_v, state_rwkv, state_delta, cache_diff_k,
           cache_diff_v, c, c_ctx, mod_w, mod_b, norm1_g, norm2_g, ev_w_in, ev_w_out, a_qn, a_kn, b_mu,
           b_w0, b_w2, b_a0, b_a2, b_g2, b_kk, b_ka, b_rk, b_lnx_w, b_lnx_b, od_w_in, od_w_out, c_conv,
           c_A_log, c_dt_bias, c_onorm, d_qn, d_kn, d_lambda, d_subln, router_w, router_bias, exp_w_gate,
           exp_w_up, exp_w_down, sh_w_gate, sh_w_up, sh_w_down):
    x = jnp.concatenate([x_prompt.reshape(N_CTX, D_MODEL), x_sample.reshape(N_LAT, D_MODEL)], axis=0)
    cond = jnp.concatenate([c_ctx[None], c, jnp.zeros((8 - N_MOD, D_MODEL), F32)], axis=0)
    mod = _adaln_call(cond, mod_w, mod_b)[:, 0:N_MOD].reshape(DEPTH, N_MOD, 6, D_MODEL)
    lat_blk = N_CTX // DEC_SEQ
    new_ak, new_av, new_sr, new_sd, new_dk, new_dv = [], [], [], [], [], []
    for l in range(DEPTH):
        j = l // 2
        if l % 2 == 0:
            w = ev_w_in[j]
            q, k, v = w[:, 0:512], w[:, 512:640], w[:, 640:768]
            w_a = jnp.concatenate([q[:, 0:256], k[:, 0:64], v[:, 0:64], q[:, 256:], k[:, 64:], v[:, 64:]], axis=1)
            amat, bmat = _proj_in_call(x, mod, l, norm1_g[l], [w_a.astype(BF16), w[:, 768:].astype(BF16)])
            gain = jnp.concatenate([jnp.tile(a_qn[j], A_GROUP), a_kn[j], jnp.ones((HD,), F32)]).reshape(1, GW)
            oa_c, nk, nv = _attn_a_call(amat, 0, BATCH, SEQ, gain)
            oa_l, = _attn_a_call(amat, lat_blk, DEC_BATCH, DEC_SEQ, gain, (cache_attn_k, cache_attn_v, j))
            p = {'b_mu': b_mu[j], 'b_w0': b_w0[j], 'b_w2': b_w2[j], 'b_a0': b_a0[j], 'b_a2': b_a2[j],
                 'b_g2': b_g2[j], 'b_kk': b_kk[j], 'b_ka': b_ka[j], 'b_rk': b_rk[j], 'b_lnx_w': b_lnx_w[j],
                 'b_lnx_b': b_lnx_b[j]}
            ob_c, sr = _rwkv_call(bmat, 0, BATCH, SEQ, p)
            ob_l, _ = _rwkv_call(bmat, lat_blk, DEC_BATCH, DEC_SEQ, p, state_rwkv[:, j])
            o1 = jnp.concatenate([oa_c, oa_l], axis=0)
            o2 = jnp.concatenate([ob_c, ob_l], axis=0)
            w_out = ev_w_out[j]
            new_ak.append(nk)
            new_av.append(nv)
            new_sr.append(sr)
        else:
            lam_init = 0.8 - 0.6 * math.exp(-0.3 * l)
            w = od_w_in[j]
            s0 = 4 * C_WIDTH + 4 * C_HEADS
            dq, dk, dv = w[:, s0:s0 + 512], w[:, s0 + 512:s0 + 1024], w[:, s0 + 1024:]
            w_ab = jnp.pad(w[:, 4 * C_WIDTH:s0], ((0, 0), (0, LANE - 4 * C_HEADS)))
            w_d = jnp.concatenate([jnp.concatenate([dq[:, 128 * h:128 * (h + 1)], dk[:, 128 * h:128 * (h + 1)],
                                                    dv[:, 128 * h:128 * (h + 1)]], axis=1) for h in range(D_HEADS)],
                                  axis=1)
            cmat, abmat, dmat = _proj_in_call(x, mod, l, norm1_g[l],
                                              [w[:, 0:4 * C_WIDTH].astype(BF16), w_ab.astype(BF16), w_d.astype(BF16)])
            p = {'c_conv': c_conv[j], 'c_A_log': c_A_log[j], 'c_dt_bias': c_dt_bias[j], 'c_onorm': c_onorm[j]}
            oc_c, sd_ = _delta_call(cmat, abmat, 0, BATCH, SEQ, p)
            oc_l, _ = _delta_call(cmat, abmat, lat_blk, DEC_BATCH, DEC_SEQ, p, state_delta[:, j])
            gain = jnp.concatenate([jnp.tile(d_qn[j], 2), jnp.tile(d_kn[j], 2), jnp.ones((D_VDIM,), F32)]).reshape(1, GW)
            od_c, ndk, ndv = _attn_d_call(dmat, 0, BATCH, SEQ, gain, d_lambda[j], d_subln[j], lam_init)
            od_l, = _attn_d_call(dmat, lat_blk, DEC_BATCH, DEC_SEQ, gain, d_lambda[j], d_subln[j], lam_init,
                                 (cache_diff_k, cache_diff_v, j))
            o1 = jnp.concatenate([oc_c, oc_l], axis=0)
            o2 = jnp.concatenate([od_c, od_l], axis=0)
            w_out = od_w_out[j]
            new_dk.append(ndk)
            new_dv.append(ndv)
            new_sd.append(sd_)
        x, tok, sh, logits_t = _proj_out_call(
            o1, o2, x, mod, l, norm2_g[l], w_out.astype(BF16), router_w[l].T.astype(BF16),
            sh_w_gate[l].astype(BF16), sh_w_up[l].astype(BF16), sh_w_down[l].astype(BF16))
        idx, wts = _route_call(logits_t, router_bias[l])
        tables = _dispatch_tables(idx, M_TOK)
        yrows = _expert_call(tok, tables, l, exp_w_gate, exp_w_up, exp_w_down)
        x = _combine_call(x, sh, wts.T, mod, l, yrows)
    return (x[0:N_CTX].reshape(BATCH, SEQ, D_MODEL), x[N_CTX:].reshape(DEC_BATCH, DEC_SEQ, D_MODEL),
            jnp.stack(new_ak, axis=1), jnp.stack(new_av, axis=1), jnp.stack(new_sr, axis=1),
            jnp.stack(new_sd, axis=1), jnp.stack(new_dk, axis=1), jnp.stack(new_dv, axis=1))
```

```python
import functools
import math
import jax
import jax.numpy as jnp
from jax import lax
from jax.experimental import pallas as pl
from jax.experimental.pallas import tpu as pltpu

D_MODEL = 1024
BATCH = 32
SEQ = 256
DEPTH = 4
DEC_BATCH = 2
DEC_SEQ = 1024
PAST_LEN = 512
GRID_W = 64
NORM_EPS = 1e-6
ROPE_THETA = 10000.0
A_HEADS = 8
A_KV_HEADS = 2
A_GROUP = A_HEADS // A_KV_HEADS
B_HEADS = 8
B_DIM = 64
B_WIDTH = B_HEADS * B_DIM
DECAY_LORA = 64
ICLR_LORA = 64
GATE_LORA = 128
B_COLS = 3 * B_WIDTH + DECAY_LORA + ICLR_LORA + GATE_LORA
LNX_EPS = 64e-5
C_HEADS = 4
C_DIM = 128
C_WIDTH = C_HEADS * C_DIM
D_HEADS = 4
D_VDIM = 128
N_EXPERTS = 64
TOP_K = 6
N_GROUPS = 8
TOPK_GROUPS = 4
PER_GROUP = N_EXPERTS // N_GROUPS
EXPERT_FF = 256
SHARED_FF = 256
ROUTED_SCALE = 1.0

HD = 64
GW = 384
CH = 64
LANE = 128
BM = 128
DMA_UNROLL = 8
TM = 512
SEQ_BLK = 256
SOLVE_BLK = 128
N_CTX = BATCH * SEQ
N_LAT = DEC_BATCH * DEC_SEQ
M_TOK = N_CTX + N_LAT
N_MOD = 1 + DEC_BATCH
VMEM_LIMIT = 56 * 1024 * 1024

F32 = jnp.float32
BF16 = jnp.bfloat16
HI = lax.Precision.HIGHEST
NEG = -jnp.inf


def _dot(a, b):
    return jnp.dot(a.astype(BF16), b.astype(BF16), preferred_element_type=F32)


def _dot_nt(a, b):
    return lax.dot_general(a.astype(BF16), b.astype(BF16), (((1,), (1,)), ((), ())), preferred_element_type=F32)


def _dot_tn(a, b):
    return lax.dot_general(a.astype(BF16), b.astype(BF16), (((0,), (0,)), ((), ())), preferred_element_type=F32)


def _dot_hi(a, b):
    return jnp.dot(a, b, preferred_element_type=F32, precision=HI)


def _silu(x):
    return x * jax.nn.sigmoid(x)


def _mod_group(i):
    n_ctx_tiles = N_CTX // TM
    return jnp.where(i < n_ctx_tiles, 0, 1 + (i - n_ctx_tiles) // (DEC_SEQ // TM))


def _full(shape):
    return pl.BlockSpec(shape, lambda *_: (0,) * len(shape))


def _adaln_kernel(c_ref, w_ref, b_ref, o_ref):
    o_ref[0] = _dot(_silu(c_ref[...]), w_ref[0]) + b_ref[0]


def _adaln_call(cond, mod_w, mod_b):
    n = 6
    return pl.pallas_call(
        _adaln_kernel,
        grid=(DEPTH, n),
        in_specs=[_full((8, D_MODEL)),
                  pl.BlockSpec((1, D_MODEL, D_MODEL), lambda l, j: (l, 0, j)),
                  pl.BlockSpec((1, 1, D_MODEL), lambda l, j: (l, 0, j))],
        out_specs=pl.BlockSpec((1, 8, D_MODEL), lambda l, j: (l, 0, j)),
        out_shape=jax.ShapeDtypeStruct((DEPTH, 8, n * D_MODEL), F32),
    )(cond, mod_w, mod_b.reshape(DEPTH, 1, n * D_MODEL))


def _proj_in_kernel(n_out, x_ref, mod_ref, g_ref, *rest):
    x = x_ref[...]
    y = x * lax.rsqrt(jnp.mean(x * x, axis=-1, keepdims=True) + NORM_EPS) * g_ref[...]
    h = (y * (1.0 + mod_ref[0, 0, 1:2, :]) + mod_ref[0, 0, 0:1, :]).astype(BF16)
    for w_ref, o_ref in zip(rest[:n_out], rest[n_out:]):
        o_ref[...] = jnp.dot(h, w_ref[...], preferred_element_type=F32)


def _proj_in_call(x, mod, layer, g, weights):
    n_out = len(weights)
    return pl.pallas_call(
        functools.partial(_proj_in_kernel, n_out),
        grid=(M_TOK // TM,),
        in_specs=[pl.BlockSpec((TM, D_MODEL), lambda i: (i, 0)),
                  pl.BlockSpec((1, 1, 6, D_MODEL), lambda i: (layer, _mod_group(i), 0, 0)),
                  _full((1, D_MODEL))] + [_full(w.shape) for w in weights],
        out_specs=[pl.BlockSpec((TM, w.shape[1]), lambda i: (i, 0)) for w in weights],
        out_shape=[jax.ShapeDtypeStruct((M_TOK, w.shape[1]), F32) for w in weights],
        compiler_params=pltpu.CompilerParams(vmem_limit_bytes=VMEM_LIMIT),
    )(x, mod, g.reshape(1, D_MODEL), *weights)


def _proj_out_kernel(o1_ref, o2_ref, x_ref, mod_ref, g_ref, wo_ref, rwt_ref, sg_ref, su_ref, sd_ref,
                     xn_ref, tok_ref, sh_ref, lg_ref):
    o = jnp.concatenate([o1_ref[...], o2_ref[...]], axis=1).astype(BF16)
    xn = x_ref[...] + mod_ref[0, 0, 2:3, :] * jnp.dot(o, wo_ref[...], preferred_element_type=F32)
    xn_ref[...] = xn
    y = xn * lax.rsqrt(jnp.mean(xn * xn, axis=-1, keepdims=True) + NORM_EPS) * g_ref[...]
    h2 = y * (1.0 + mod_ref[0, 0, 4:5, :]) + mod_ref[0, 0, 3:4, :]
    tok_ref[...] = h2
    hb = h2.astype(BF16)
    lg_ref[...] = _dot_nt(rwt_ref[...], hb)
    hid = _silu(jnp.dot(hb, sg_ref[...], preferred_element_type=F32)) * jnp.dot(hb, su_ref[...], preferred_element_type=F32)
    sh_ref[...] = jnp.dot(hid.astype(BF16), sd_ref[...], preferred_element_type=F32)


def _proj_out_call(o1, o2, x, mod, layer, g2, w_out, rw_t, sg, su, sd):
    row = lambda n: pl.BlockSpec((TM, n), lambda i: (i, 0))
    return pl.pallas_call(
        _proj_out_kernel,
        grid=(M_TOK // TM,),
        in_specs=[row(o1.shape[1]), row(o2.shape[1]), row(D_MODEL),
                  pl.BlockSpec((1, 1, 6, D_MODEL), lambda i: (layer, _mod_group(i), 0, 0)),
                  _full((1, D_MODEL)), _full(w_out.shape), _full(rw_t.shape), _full(sg.shape), _full(su.shape),
                  _full(sd.shape)],
        out_specs=[row(D_MODEL), row(D_MODEL), row(D_MODEL), pl.BlockSpec((N_EXPERTS, TM), lambda i: (0, i))],
        out_shape=[jax.ShapeDtypeStruct((M_TOK, D_MODEL), F32)] * 3 + [jax.ShapeDtypeStruct((N_EXPERTS, M_TOK), F32)],
        compiler_params=pltpu.CompilerParams(vmem_limit_bytes=VMEM_LIMIT),
    )(o1, o2, x, mod, g2.reshape(1, D_MODEL), w_out, rw_t, sg, su, sd)


def _rope_lane_tables(n_tok):
    rows = n_tok // GRID_W
    row = jnp.repeat(jnp.arange(rows, dtype=F32), GRID_W)
    col = jnp.tile(jnp.arange(GRID_W, dtype=F32), rows)
    n_freq = HD // 4
    inv = ROPE_THETA ** (-jnp.arange(n_freq, dtype=F32) / n_freq)
    ang = jnp.concatenate([row[:, None] * inv, col[:, None] * inv], axis=-1)
    cos = jnp.repeat(jnp.cos(ang), 2, axis=-1)
    sin = jnp.repeat(jnp.sin(ang), 2, axis=-1) * jnp.tile(jnp.array([-1.0, 1.0], F32), HD // 2)
    return cos, sin


def _seg_matrix(width, seg, value):
    li = lax.broadcasted_iota(jnp.int32, (width, width), 0) // seg
    lj = lax.broadcasted_iota(jnp.int32, (width, width), 1) // seg
    return jnp.where(li == lj, value, 0.0).astype(F32)


def _norm_rope(x, gain, n_norm, cos, sin):
    lane = lax.broadcasted_iota(jnp.int32, x.shape, 1)
    ms = _dot_hi(x * x, _seg_matrix(x.shape[1], HD, 1.0 / HD))
    xn = jnp.where(lane < n_norm, x * lax.rsqrt(ms + NORM_EPS) * gain, x)
    if cos is None:
        return xn, xn
    w = x.shape[1]
    swapped = jnp.where(lane % 2 == 0, pltpu.roll(xn, w - 1, 1), pltpu.roll(xn, 1, 1))
    return xn, xn * cos + swapped * sin


def _softmax_pv(s, v):
    m = jnp.max(s, axis=-1, keepdims=True)
    p = jnp.exp(s - m)
    l = jnp.sum(p, axis=-1, keepdims=True)
    return _dot(p, v) / l


def _attn_a_kernel(T, P, TQ, *refs):
    if P:
        x_ref, gain_ref, cos_ref, sin_ref, ck_ref, cv_ref, o_ref, xr_s = refs
        xn, xr = _norm_rope(x_ref[...], gain_ref[...], 5 * HD, cos_ref[...], sin_ref[...])
    else:
        x_ref, gain_ref, o_ref, nk_ref, nv_ref, xr_s = refs
        xn, xr = _norm_rope(x_ref[...], gain_ref[...], 5 * HD, None, None)
        nk_ref[0, 0] = xn[:, 4 * HD:5 * HD]
        nv_ref[0, 0] = xn[:, 5 * HD:6 * HD]
    xr_s[...] = xr
    k_new = xr[:, 4 * HD:5 * HD]
    v_new = xr[:, 5 * HD:6 * HD]
    if P:
        k_all = jnp.concatenate([ck_ref[0, 0, 0], k_new], axis=0).astype(BF16)
        v_all = jnp.concatenate([cv_ref[0, 0, 0], v_new], axis=0).astype(BF16)
    else:
        k_all, v_all = k_new.astype(BF16), v_new.astype(BF16)

    def q_block(qb, carry):
        rows = pl.ds(pl.multiple_of(qb * TQ, TQ), TQ)
        qx = xr_s[rows, 0:A_GROUP * HD]
        qs = jnp.concatenate([qx[:, i * HD:(i + 1) * HD] for i in range(A_GROUP)], axis=0)
        o = _softmax_pv(_dot_nt(qs, k_all) * (HD ** -0.5), v_all)
        o_ref[rows, :] = jnp.concatenate([o[i * TQ:(i + 1) * TQ, :] for i in range(A_GROUP)], axis=1)
        return carry

    lax.fori_loop(0, T // TQ, q_block, 0)


def _attn_a_call(amat, row_off_blocks, n_seq, T, gain, cache=None):
    P = 0 if cache is None else cache[0].shape[3]
    TQ = T if P == 0 else 128
    in_specs = [pl.BlockSpec((T, GW), lambda i, g: (row_off_blocks + i, g)), _full((1, GW))]
    args = [amat, gain]
    out_specs = [pl.BlockSpec((T, A_GROUP * HD), lambda i, g: (i, g))]
    out_shape = [jax.ShapeDtypeStruct((n_seq * T, A_HEADS * HD), F32)]
    if P:
        ck, cv, j = cache
        cos, sin = _rope_lane_tables(T)
        cos_g = jnp.concatenate([cos] * 5 + [jnp.ones((T, HD), F32)], axis=1)
        sin_g = jnp.concatenate([sin] * 5 + [jnp.zeros((T, HD), F32)], axis=1)
        in_specs += [_full((T, GW)), _full((T, GW)),
                     pl.BlockSpec((1, 1, 1, P, HD), lambda i, g: (i, j, g, 0, 0)),
                     pl.BlockSpec((1, 1, 1, P, HD), lambda i, g: (i, j, g, 0, 0))]
        args += [cos_g, sin_g, ck, cv]
    else:
        out_specs += [pl.BlockSpec((1, 1, T, HD), lambda i, g: (i, g, 0, 0))] * 2
        out_shape += [jax.ShapeDtypeStruct((n_seq, A_KV_HEADS, T, HD), F32)] * 2
    return pl.pallas_call(
        functools.partial(_attn_a_kernel, T, P, TQ),
        grid=(n_seq, A_KV_HEADS),
        in_specs=in_specs, out_specs=out_specs, out_shape=out_shape,
        scratch_shapes=[pltpu.VMEM((T, GW), F32)],
        compiler_params=pltpu.CompilerParams(vmem_limit_bytes=VMEM_LIMIT),
    )(*args)


def _attn_d_kernel(T, P, TQ, lam_init, *refs):
    if P:
        x_ref, gain_ref, lam_ref, sub_ref, cos_ref, sin_ref, ck_ref, cv_ref, o_ref, xr_s = refs
        xn, xr = _norm_rope(x_ref[...], gain_ref[...], 4 * HD, cos_ref[...], sin_ref[...])
    else:
        x_ref, gain_ref, lam_ref, sub_ref, o_ref, nk_ref, nv_ref, xr_s = refs
        xn, xr = _norm_rope(x_ref[...], gain_ref[...], 4 * HD, None, None)
        nk_ref[0, 0, 0] = xn[:, 2 * HD:3 * HD]
        nk_ref[0, 0, 1] = xn[:, 3 * HD:4 * HD]
        nv_ref[0, 0] = xn[:, 4 * HD:]
    xr_s[...] = xr
    v_new = xr[:, 4 * HD:]
    ks = []
    for m in range(2):
        k_new = xr[:, (2 + m) * HD:(3 + m) * HD]
        if P:
            ks.append(jnp.concatenate([ck_ref[0, 0, 0, m], k_new], axis=0).astype(BF16))
        else:
            ks.append(k_new.astype(BF16))
    v_all = (jnp.concatenate([cv_ref[0, 0, 0], v_new], axis=0) if P else v_new).astype(BF16)
    lm = lam_ref[...]
    lam = (jnp.exp(jnp.sum(lm[0:1, :] * lm[1:2, :], axis=-1, keepdims=True))
           - jnp.exp(jnp.sum(lm[2:3, :] * lm[3:4, :], axis=-1, keepdims=True)) + lam_init)

    def q_block(qb, carry):
        rows = pl.ds(pl.multiple_of(qb * TQ, TQ), TQ)
        qx = xr_s[rows, 0:2 * HD]
        o1 = _softmax_pv(_dot_nt(qx[:, 0:HD], ks[0]) * (HD ** -0.5), v_all)
        o2 = _softmax_pv(_dot_nt(qx[:, HD:], ks[1]) * (HD ** -0.5), v_all)
        od = o1 - lam * o2
        od = od * lax.rsqrt(jnp.mean(od * od, axis=-1, keepdims=True) + NORM_EPS) * sub_ref[...]
        o_ref[rows, :] = od * (1.0 - lam_init)
        return carry

    lax.fori_loop(0, T // TQ, q_block, 0)


def _attn_d_call(dmat, row_off_blocks, n_seq, T, gain, lam, subln, lam_init, cache=None):
    P = 0 if cache is None else cache[0].shape[4]
    TQ = T if P == 0 else 256
    in_specs = [pl.BlockSpec((T, GW), lambda i, h: (row_off_blocks + i, h)),
                _full((1, GW)), _full((4, HD)), _full((1, D_VDIM))]
    args = [dmat, gain, lam, subln.reshape(1, -1)]
    out_specs = [pl.BlockSpec((T, D_VDIM), lambda i, h: (i, h))]
    out_shape = [jax.ShapeDtypeStruct((n_seq * T, D_HEADS * D_VDIM), F32)]
    if P:
        ck, cv, j = cache
        cos, sin = _rope_lane_tables(T)
        cos_g = jnp.concatenate([cos] * 4 + [jnp.ones((T, D_VDIM), F32)], axis=1)
        sin_g = jnp.concatenate([sin] * 4 + [jnp.zeros((T, D_VDIM), F32)], axis=1)
        in_specs += [_full((T, GW)), _full((T, GW)),
                     pl.BlockSpec((1, 1, 1, 2, P, HD), lambda i, h: (i, j, h, 0, 0, 0)),
                     pl.BlockSpec((1, 1, 1, P, D_VDIM), lambda i, h: (i, j, h, 0, 0))]
        args += [cos_g, sin_g, ck, cv]
    else:
        out_specs += [pl.BlockSpec((1, 1, 2, T, HD), lambda i, h: (i, h, 0, 0, 0)),
                      pl.BlockSpec((1, 1, T, D_VDIM), lambda i, h: (i, h, 0, 0))]
        out_shape += [jax.ShapeDtypeStruct((n_seq, D_HEADS, 2, T, HD), F32),
                      jax.ShapeDtypeStruct((n_seq, D_HEADS, T, D_VDIM), F32)]
    return pl.pallas_call(
        functools.partial(_attn_d_kernel, T, P, TQ, lam_init),
        grid=(n_seq, D_HEADS),
        in_specs=in_specs, out_specs=out_specs, out_shape=out_shape,
        scratch_shapes=[pltpu.VMEM((T, GW), F32)],
        compiler_params=pltpu.CompilerParams(vmem_limit_bytes=VMEM_LIMIT),
    )(*args)


def _dot_3pass(a, b):
    a_hi = a.astype(BF16)
    a_lo = (a - a_hi.astype(F32)).astype(BF16)
    b_hi = b.astype(BF16)
    b_lo = (b - b_hi.astype(F32)).astype(BF16)
    d = lambda x, y: jnp.dot(x, y, preferred_element_type=F32)
    return d(a_hi, b_hi) + (d(a_hi, b_lo) + d(a_lo, b_hi))


def _unit_lower_solve(a_bd, x):
    n_stage = CH.bit_length() - 1
    outs = []
    for g in range(a_bd.shape[0] // SOLVE_BLK):
        sl = slice(g * SOLVE_BLK, (g + 1) * SOLVE_BLK)
        am, xm = a_bd[sl, sl], x[sl, :]
        for s in range(n_stage):
            xm = xm + _dot_3pass(am, xm)
            if s < n_stage - 1:
                am = _dot_3pass(am, am)
        outs.append(xm)
    return jnp.concatenate(outs, axis=0)


def _head_stack(x, n_heads, width):
    return jnp.concatenate([x[:, h * width:(h + 1) * width] for h in range(n_heads)], axis=0)


def _head_unstack(x, n_heads):
    return jnp.concatenate([x[h * CH:(h + 1) * CH, :] for h in range(n_heads)], axis=1)


def _head_masks(n_heads, d):
    hc = n_heads * CH
    ri = lax.broadcasted_iota(jnp.int32, (hc, hc), 0)
    rj = lax.broadcasted_iota(jnp.int32, (hc, hc), 1)
    same = (ri // CH) == (rj // CH)
    return (same & (ri >= rj), same & (ri > rj)) if d == 0 else (same & (ri <= rj), same & (ri < rj))


def _chunk_tri(d):
    ci = lax.broadcasted_iota(jnp.int32, (CH, CH), 0)
    cj = lax.broadcasted_iota(jnp.int32, (CH, CH), 1)
    return jnp.where(ci >= cj if d == 0 else ci <= cj, 1.0, 0.0).astype(F32)


def _rwkv_kernel(T, has_state, *refs):
    if has_state:
        (xb_ref, mu_ref, w0_ref, w2_ref, a0_ref, a2_ref, g2_ref, kk_ref, ka_ref, rk_ref, lnw_ref, lnb_ref, s0_ref,
         ob_ref, sf_ref, r_s, v_s, av_s, lw_s, kd_s, bv_s, y_s, bon_s, gate_s, st_s, mbd_s, mst_s, min_s) = refs
    else:
        (xb_ref, mu_ref, w0_ref, w2_ref, a0_ref, a2_ref, g2_ref, kk_ref, ka_ref, rk_ref, lnw_ref, lnb_ref,
         ob_ref, sf_ref, r_s, v_s, av_s, lw_s, kd_s, bv_s, y_s, bon_s, gate_s, st_s, mbd_s, mst_s, min_s) = refs
    n_chunks = T // CH
    n_blk = T // SEQ_BLK
    hc = B_HEADS * CH
    seg = _seg_matrix(B_WIDTH, B_DIM, 1.0)
    row = lax.broadcasted_iota(jnp.int32, (SEQ_BLK, 1), 0)
    for b in range(n_blk):
        r0 = b * SEQ_BLK
        rb = slice(r0, r0 + SEQ_BLK)
        x = xb_ref[rb, :]
        before = xb_ref[r0 - 1:r0, :] if b > 0 else jnp.zeros((1, B_COLS), F32)
        after = xb_ref[r0 + SEQ_BLK:r0 + SEQ_BLK + 1, :] if b < n_blk - 1 else jnp.zeros((1, B_COLS), F32)
        prev = jnp.where(row == 0, before, pltpu.roll(x, 1, 0))
        nxt = jnp.where(row == SEQ_BLK - 1, after, pltpu.roll(x, SEQ_BLK - 1, 0))
        xs = x + mu_ref[...] * (0.5 * (prev + nxt) - x)
        r = xs[:, 0:B_WIDTH]
        kb = xs[:, B_WIDTH:2 * B_WIDTH]
        vb = xs[:, 2 * B_WIDTH:3 * B_WIDTH]
        wd = xs[:, 3 * B_WIDTH:3 * B_WIDTH + DECAY_LORA]
        ad = xs[:, 3 * B_WIDTH + DECAY_LORA:3 * B_WIDTH + DECAY_LORA + ICLR_LORA]
        gd = xs[:, 3 * B_WIDTH + DECAY_LORA + ICLR_LORA:]
        kk = kb * kk_ref[...]
        kkn = kk * lax.rsqrt(_dot_hi(kk * kk, seg) + 1e-6)
        r_s[rb, :] = r
        v_s[rb, :] = vb
        av_s[rb, :] = -kkn
        bon_s[rb, :] = _dot_hi(r * kb * rk_ref[...], seg) * vb
        gate_s[rb, :] = _dot(jax.nn.sigmoid(gd), g2_ref[...])
        twd = jnp.tanh(wd)
        for d in range(2):
            wl = w0_ref[d:d + 1, :] + _dot(twd, w2_ref[d])
            w_log = -jax.nn.softplus(-wl) - 0.5
            lw_s[d, rb, :] = -jnp.exp(w_log)
            a = jax.nn.sigmoid(a0_ref[d:d + 1, :] + _dot(ad, a2_ref[d]))
            kd_s[d, rb, :] = kb * (1.0 + (a - 1.0) * ka_ref[...])
            bv_s[d, rb, :] = kkn * a
    if has_state:
        for d in range(2):
            st_s[d] = jnp.concatenate([s0_ref[0, d, h] for h in range(B_HEADS)], axis=1)
    else:
        st_s[...] = jnp.zeros_like(st_s)
    incl0, _ = _head_masks(B_HEADS, 0)
    incl1, _ = _head_masks(B_HEADS, 1)
    mbd_s[...] = jnp.where(incl0 | incl1, 1.0, 0.0).astype(F32)

    def expand(x):
        return jnp.concatenate([x] * B_HEADS, axis=1) * mbd_s[...]

    for d in range(2):
        incl, strict = _head_masks(B_HEADS, d)
        min_s[...] = jnp.where(incl, 1.0, 0.0).astype(F32)
        mst_s[...] = jnp.where(strict, 1.0, 0.0).astype(F32)
        tri = _chunk_tri(d)
        last = CH - 1 if d == 0 else 0

        def chunk_body(it, carry, d=d, tri=tri, last=last):
            c = it if d == 0 else n_chunks - 1 - it
            rows = pl.ds(pl.multiple_of(c * CH, CH), CH)
            lwc = lw_s[d, rows, :]
            cum = _dot_hi(tri, lwc)
            e_pos = jnp.exp(cum)
            e_neg = jnp.exp(-cum)
            rt = _head_stack(r_s[rows, :] * e_pos, B_HEADS, B_DIM)
            at = _head_stack(av_s[rows, :] * jnp.exp(cum - lwc), B_HEADS, B_DIM)
            bk = jnp.concatenate([_head_stack(bv_s[d, rows, :] * e_neg, B_HEADS, B_DIM),
                                  _head_stack(kd_s[d, rows, :] * e_neg, B_HEADS, B_DIM)], axis=0)
            vs = _head_stack(v_s[rows, :], B_HEADS, B_DIM)
            pc = e_pos[last:last + 1, :]
            g1 = _dot_nt(at, bk)
            g2 = _dot_nt(rt, bk)
            mst = mst_s[...]
            a_ab = g1[:, 0:hc] * mst
            a_ak = g1[:, hc:] * mst
            mi = min_s[...]
            a_r = jnp.concatenate([g2[:, 0:hc] * mi, g2[:, hc:] * mi], axis=1)
            xm = _unit_lower_solve(a_ab, jnp.concatenate([at, _dot(a_ak, vs)], axis=1))
            s_old = st_s[d]
            ws = _dot_nt(jnp.concatenate([expand(xm[:, 0:B_DIM]), expand(rt)], axis=0), s_old)
            uv = jnp.concatenate([ws[0:hc, :] + xm[:, B_DIM:], vs], axis=0)
            ys = ws[hc:, :] + _dot(a_r, uv)
            bk_e = jnp.concatenate([expand(bk[0:hc, :]), expand(bk[hc:, :])], axis=0)
            st_s[d] = (s_old + _dot_tn(uv, bk_e)) * pc
            yc = _head_unstack(ys, B_HEADS)
            if d == 0:
                y_s[rows, :] = yc
            else:
                y_s[rows, :] = y_s[rows, :] + yc
            return carry

        lax.fori_loop(0, n_chunks, chunk_body, 0)
    for b in range(n_blk):
        rb = slice(b * SEQ_BLK, (b + 1) * SEQ_BLK)
        y = y_s[rb, :]
        yc = y - _dot_hi(y, seg) * (1.0 / B_DIM)
        var = _dot_hi(yc * yc, seg) * (1.0 / B_DIM)
        yn = yc * lax.rsqrt(var + LNX_EPS) * lnw_ref[...] + lnb_ref[...]
        ob_ref[rb, :] = (yn + bon_s[rb, :]) * gate_s[rb, :]
    for d in range(2):
        sd = st_s[d]
        for h in range(B_HEADS):
            sf_ref[0, d, h] = sd[:, h * B_DIM:(h + 1) * B_DIM]


def _seq_spec(T, cols, row_off_blocks):
    extra = {'pipeline_mode': pl.Buffered(1)} if T > SEQ_BLK else {}
    return pl.BlockSpec((T, cols), lambda i: (row_off_blocks + i, 0), **extra)


def _rwkv_call(xb, row_off_blocks, n_seq, T, p, s0=None):
    has_state = s0 is not None
    st_shape = (2, B_HEADS, B_DIM, B_DIM)
    hc = B_HEADS * CH
    in_specs = [_seq_spec(T, B_COLS, row_off_blocks),
                _full((1, B_COLS)), _full((2, B_WIDTH)), _full((2, DECAY_LORA, B_WIDTH)), _full((2, B_WIDTH)),
                _full((2, ICLR_LORA, B_WIDTH)), _full((GATE_LORA, B_WIDTH))] + [_full((1, B_WIDTH))] * 5
    args = [xb, p['b_mu'].reshape(1, -1), p['b_w0'], p['b_w2'], p['b_a0'], p['b_a2'], p['b_g2'],
            p['b_kk'].reshape(1, -1), p['b_ka'].reshape(1, -1), p['b_rk'].reshape(1, -1),
            p['b_lnx_w'].reshape(1, -1), p['b_lnx_b'].reshape(1, -1)]
    if has_state:
        in_specs.append(pl.BlockSpec((1,) + st_shape, lambda i: (i, 0, 0, 0, 0)))
        args.append(s0)
    scr = [pltpu.VMEM((T, B_WIDTH), F32)] * 3 + [pltpu.VMEM((2, T, B_WIDTH), F32)] * 3 + \
          [pltpu.VMEM((T, B_WIDTH), F32)] * 3 + [pltpu.VMEM((2, B_DIM, B_WIDTH), F32)] + \
          [pltpu.VMEM((hc, hc), F32)] * 3
    return pl.pallas_call(
        functools.partial(_rwkv_kernel, T, has_state),
        grid=(n_seq,),
        in_specs=in_specs,
        out_specs=[pl.BlockSpec((T, B_WIDTH), lambda i: (i, 0)),
                   pl.BlockSpec((1,) + st_shape, lambda i: (i, 0, 0, 0, 0))],
        out_shape=[jax.ShapeDtypeStruct((n_seq * T, B_WIDTH), F32),
                   jax.ShapeDtypeStruct((n_seq,) + st_shape, F32)],
        scratch_shapes=scr,
        compiler_params=pltpu.CompilerParams(vmem_limit_bytes=VMEM_LIMIT),
    )(*args)


def _delta_kernel(T, has_state, *refs):
    if has_state:
        (c_ref, ab_ref, conv_ref, arow_ref, dtrow_ref, on_ref, s0_ref, oc_ref, sf_ref,
         q_s, k_s, v_s, g_s, b_s, o_s, st_s) = refs
    else:
        (c_ref, ab_ref, conv_ref, arow_ref, dtrow_ref, on_ref, oc_ref, sf_ref,
         q_s, k_s, v_s, g_s, b_s, o_s, st_s) = refs
    n_chunks = T // CH
    n_blk = T // SEQ_BLK
    hc = C_HEADS * CH
    row = lax.broadcasted_iota(jnp.int32, (SEQ_BLK, 1), 0)
    for b in range(n_blk):
        r0 = b * SEQ_BLK
        rb = slice(r0, r0 + SEQ_BLK)
        x = c_ref[rb, 0:3 * C_WIDTH]
        before = c_ref[r0 - 1:r0, 0:3 * C_WIDTH] if b > 0 else jnp.zeros((1, 3 * C_WIDTH), F32)
        after = c_ref[r0 + SEQ_BLK:r0 + SEQ_BLK + 1, 0:3 * C_WIDTH] if b < n_blk - 1 else jnp.zeros((1, 3 * C_WIDTH), F32)
        prev = jnp.where(row == 0, before, pltpu.roll(x, 1, 0))
        nxt = jnp.where(row == SEQ_BLK - 1, after, pltpu.roll(x, SEQ_BLK - 1, 0))
        xc = _silu(conv_ref[0:1, :] * prev + conv_ref[1:2, :] * x + conv_ref[2:3, :] * nxt)
        for h in range(C_HEADS):
            sl = slice(h * C_DIM, (h + 1) * C_DIM)
            qh = xc[:, h * C_DIM:(h + 1) * C_DIM]
            kh = xc[:, C_WIDTH + h * C_DIM:C_WIDTH + (h + 1) * C_DIM]
            q_s[rb, sl] = qh * lax.rsqrt(jnp.sum(qh * qh, axis=-1, keepdims=True) + 1e-6) * (C_DIM ** -0.5)
            k_s[rb, sl] = kh * lax.rsqrt(jnp.sum(kh * kh, axis=-1, keepdims=True) + 1e-6)
        v_s[rb, :] = xc[:, 2 * C_WIDTH:]
        ab = ab_ref[rb, :]
        g_s[rb, :] = arow_ref[...] * jax.nn.softplus(ab + dtrow_ref[...])
        b_s[rb, :] = jax.nn.sigmoid(ab)
    if has_state:
        for d in range(2):
            st_s[d] = jnp.concatenate([s0_ref[0, d, h] for h in range(C_HEADS)], axis=0)
    else:
        st_s[...] = jnp.zeros_like(st_s)
    ei = lax.broadcasted_iota(jnp.int32, (hc, C_WIDTH), 0) // CH
    ej = lax.broadcasted_iota(jnp.int32, (hc, C_WIDTH), 1) // C_DIM
    own = ei == ej

    def expand(x):
        return jnp.where(own, jnp.concatenate([x] * C_HEADS, axis=1), 0.0)

    ri = lax.broadcasted_iota(jnp.int32, (hc, hc), 0)
    rj = lax.broadcasted_iota(jnp.int32, (hc, hc), 1)
    eye = jnp.where(ri == rj, 1.0, 0.0).astype(F32)
    for d in range(2):
        incl, strict = _head_masks(C_HEADS, d)
        tri = _chunk_tri(d)
        last = CH - 1 if d == 0 else 0
        j0 = d * C_HEADS

        def chunk_body(it, carry, d=d, incl=incl, strict=strict, tri=tri, last=last, j0=j0):
            c = it if d == 0 else n_chunks - 1 - it
            rows = pl.ds(pl.multiple_of(c * CH, CH), CH)
            gcum = _dot_hi(tri, g_s[rows, :])
            bet = b_s[rows, :]
            heads = range(C_HEADS)
            g_col = jnp.concatenate([gcum[:, j0 + h:j0 + h + 1] for h in heads], axis=0)
            b_col = jnp.concatenate([bet[:, 2 * C_HEADS + j0 + h:2 * C_HEADS + j0 + h + 1] for h in heads], axis=0)
            g_end = [gcum[last:last + 1, j0 + h:j0 + h + 1] for h in heads]
            g_last = jnp.concatenate([jnp.broadcast_to(g, (CH, 1)) for g in g_end], axis=0)
            e_last = jnp.concatenate([jnp.broadcast_to(jnp.exp(g), (C_DIM, 1)) for g in g_end], axis=0)
            g_row = jnp.sum(eye * g_col, axis=0, keepdims=True)
            decay = jnp.where(incl, jnp.exp(jnp.where(incl, g_col - g_row, 0.0)), 0.0)
            qs = _head_stack(q_s[rows, :], C_HEADS, C_DIM)
            ks = _head_stack(k_s[rows, :], C_HEADS, C_DIM)
            vs = _head_stack(v_s[rows, :], C_HEADS, C_DIM)
            kbeta = ks * b_col
            gm = _dot_nt(jnp.concatenate([kbeta, qs], axis=0), ks)
            attn = gm[hc:, :] * decay
            eg = jnp.exp(g_col)
            xm = _unit_lower_solve(-jnp.where(strict, gm[0:hc, :] * decay, 0.0),
                                   jnp.concatenate([vs * b_col, kbeta * eg], axis=1))
            s_old = st_s[d]
            ws = _dot(jnp.concatenate([expand(xm[:, C_DIM:]), expand(qs * eg)], axis=0), s_old)
            v_new = xm[:, 0:C_DIM] - ws[0:hc, :]
            o = ws[hc:, :] + _dot(attn, v_new)
            st_s[d] = s_old * e_last + _dot_tn(expand(ks * jnp.exp(g_last - g_col)), v_new)
            oc = _head_unstack(o, C_HEADS)
            if d == 0:
                o_s[rows, :] = oc
            else:
                o_s[rows, :] = o_s[rows, :] + oc
            return carry

        lax.fori_loop(0, n_chunks, chunk_body, 0)
    for b in range(n_blk):
        rb = slice(b * SEQ_BLK, (b + 1) * SEQ_BLK)
        z = c_ref[rb, 3 * C_WIDTH:]
        for h in range(C_HEADS):
            sl = slice(h * C_DIM, (h + 1) * C_DIM)
            oh = o_s[rb, sl]
            on = oh * lax.rsqrt(jnp.mean(oh * oh, axis=-1, keepdims=True) + NORM_EPS) * on_ref[...]
            oc_ref[rb, sl] = on * _silu(z[:, sl])
    for d in range(2):
        sd = st_s[d]
        for h in range(C_HEADS):
            sf_ref[0, d, h] = sd[h * C_DIM:(h + 1) * C_DIM, :]


def _delta_call(cmat, abmat, row_off_blocks, n_seq, T, p, s0=None):
    has_state = s0 is not None
    st_shape = (2, C_HEADS, C_DIM, C_DIM)
    arow = jnp.zeros((1, LANE), F32).at[0, 0:2 * C_HEADS].set(-jnp.exp(p['c_A_log'].reshape(-1)))
    dtrow = jnp.zeros((1, LANE), F32).at[0, 0:2 * C_HEADS].set(p['c_dt_bias'].reshape(-1))
    in_specs = [pl.BlockSpec((T, 4 * C_WIDTH), lambda i: (row_off_blocks + i, 0)),
                pl.BlockSpec((T, LANE), lambda i: (row_off_blocks + i, 0)),
                _full((3, 3 * C_WIDTH)), _full((1, LANE)), _full((1, LANE)), _full((1, C_DIM))]
    args = [cmat, abmat, p['c_conv'], arow, dtrow, p['c_onorm'].reshape(1, -1)]
    if has_state:
        in_specs.append(pl.BlockSpec((1,) + st_shape, lambda i: (i, 0, 0, 0, 0)))
        args.append(s0)
    scr = [pltpu.VMEM((T, C_WIDTH), F32)] * 3 + [pltpu.VMEM((T, LANE), F32)] * 2 + \
          [pltpu.VMEM((T, C_WIDTH), F32), pltpu.VMEM((2, C_HEADS * C_DIM, C_DIM), F32)]
    return pl.pallas_call(
        functools.partial(_delta_kernel, T, has_state),
        grid=(n_seq,),
        in_specs=in_specs,
        out_specs=[pl.BlockSpec((T, C_WIDTH), lambda i: (i, 0)),
                   pl.BlockSpec((1,) + st_shape, lambda i: (i, 0, 0, 0, 0))],
        out_shape=[jax.ShapeDtypeStruct((n_seq * T, C_WIDTH), F32),
                   jax.ShapeDtypeStruct((n_seq,) + st_shape, F32)],
        scratch_shapes=scr,
        compiler_params=pltpu.CompilerParams(vmem_limit_bytes=VMEM_LIMIT),
    )(*args)


def _first_max(x, iota, size):
    m = jnp.max(x, axis=0, keepdims=True)
    idx = jnp.min(jnp.where(x == m, iota, size), axis=0, keepdims=True)
    return m, idx


def _route_kernel(lg_ref, bias_ref, idx_ref, wts_ref):
    n = lg_ref.shape[1]
    scores = jax.nn.sigmoid(lg_ref[...])
    biased = scores + bias_ref[...]
    e_iota = lax.broadcasted_iota(jnp.int32, (N_EXPERTS, n), 0)
    g_iota = lax.broadcasted_iota(jnp.int32, (PER_GROUP, n), 0)
    gs = []
    for g in range(N_GROUPS):
        xg = biased[g * PER_GROUP:(g + 1) * PER_GROUP, :]
        m1, i1 = _first_max(xg, g_iota, PER_GROUP)
        m2 = jnp.max(jnp.where(g_iota == i1, NEG, xg), axis=0, keepdims=True)
        gs.append(m1 + m2)
    gscore = jnp.concatenate(gs, axis=0)
    gi = lax.broadcasted_iota(jnp.int32, (N_GROUPS, n), 0)
    gsel = jnp.zeros((N_GROUPS, n), F32)
    for _ in range(TOPK_GROUPS):
        _, ig = _first_max(gscore, gi, N_GROUPS)
        hit = gi == ig
        gsel = jnp.where(hit, 1.0, gsel)
        gscore = jnp.where(hit, NEG, gscore)
    masked = jnp.concatenate(
        [jnp.where(gsel[g:g + 1, :] > 0.0, biased[g * PER_GROUP:(g + 1) * PER_GROUP, :], NEG)
         for g in range(N_GROUPS)], axis=0)
    ids, ws = [], []
    for _ in range(TOP_K):
        _, ie = _first_max(masked, e_iota, N_EXPERTS)
        hit = e_iota == ie
        ids.append(ie)
        ws.append(jnp.sum(jnp.where(hit, scores, 0.0), axis=0, keepdims=True))
        masked = jnp.where(hit, NEG, masked)
    wsum = ws[0]
    for w in ws[1:]:
        wsum = wsum + w
    inv = ROUTED_SCALE / (wsum + 1e-20)
    idx_ref[...] = jnp.concatenate(ids + [jnp.zeros((8 - TOP_K, n), jnp.int32)], axis=0)
    wts_ref[...] = jnp.concatenate([w * inv for w in ws] + [jnp.zeros((8 - TOP_K, n), F32)], axis=0)


def _route_call(logits_t, bias):
    m = logits_t.shape[1]
    return pl.pallas_call(
        _route_kernel,
        grid=(m // TM,),
        in_specs=[pl.BlockSpec((N_EXPERTS, TM), lambda i: (0, i)), _full((N_EXPERTS, 1))],
        out_specs=[pl.BlockSpec((8, TM), lambda i: (0, i))] * 2,
        out_shape=[jax.ShapeDtypeStruct((8, m), jnp.int32), jax.ShapeDtypeStruct((8, m), F32)],
    )(logits_t, bias.reshape(N_EXPERTS, 1))


def _dispatch_tables(idx, m):
    n_asg = m * TOP_K
    nb = -(-n_asg // BM) + N_EXPERTS
    n_pad = nb * BM - n_asg
    flat_e = idx[0:TOP_K, :].reshape(-1)
    e_iota = jnp.arange(N_EXPERTS, dtype=jnp.int32)
    counts = jnp.sum((flat_e[:, None] == e_iota[None, :]).astype(jnp.int32), axis=0)
    padded = (counts + BM - 1) // BM * BM
    pad_end = jnp.cumsum(padded)
    pad_cum = jnp.cumsum(padded - counts)
    pad_e = jnp.sum((pad_cum[None, :] <= jnp.arange(n_pad, dtype=jnp.int32)[:, None]).astype(jnp.int32), axis=1)
    keys = jnp.concatenate([flat_e * 2, pad_e * 2 + 1])
    vals = jnp.concatenate([jnp.arange(n_asg, dtype=jnp.int32), jnp.full((n_pad,), -1, jnp.int32)])
    _, slot_asg = lax.sort((keys, vals), num_keys=1, is_stable=True)
    valid = slot_asg >= 0
    spare = TOP_K * m + jnp.arange(nb * BM, dtype=jnp.int32) % BM
    slot_dst = jnp.where(valid, slot_asg, spare)
    slot_tok = jnp.where(valid, slot_asg % m, 0)
    blk0 = jnp.arange(nb, dtype=jnp.int32) * BM
    block_expert = jnp.minimum(jnp.sum((pad_end[None, :] <= blk0[:, None]).astype(jnp.int32), axis=1), N_EXPERTS - 1)
    n_used = (pad_end[-1] // BM).astype(jnp.int32).reshape(1)
    return slot_tok.reshape(nb, 1, BM), slot_dst.reshape(nb, 1, BM), block_expert.astype(jnp.int32), n_used


def _expert_kernel(be_ref, nu_ref, st_ref, stn_ref, sd_ref, tok_hbm, wg_ref, wu_ref, wd_ref, y_hbm,
                   xbuf, ybuf, sem_in, sem_out):
    i = pl.program_id(0)
    n_used = nu_ref[0]
    slot = i % 2

    def gather(tab_ref, b):
        def body(r, c):
            pltpu.make_async_copy(tok_hbm.at[pl.ds(tab_ref[0, 0, r], 1), :], xbuf.at[b, pl.ds(r, 1), :],
                                  sem_in.at[b]).start()
            return c
        lax.fori_loop(0, BM, body, 0, unroll=DMA_UNROLL)

    def wait_gather(b):
        pltpu.make_async_copy(tok_hbm.at[pl.ds(0, BM), :], xbuf.at[b], sem_in.at[b]).wait()

    def wait_scatter(b):
        pltpu.make_async_copy(ybuf.at[b], y_hbm.at[pl.ds(0, BM), :], sem_out.at[b]).wait()

    @pl.when(i == 0)
    def _():
        ybuf[0] = jnp.zeros((BM, D_MODEL), F32)
        spare = pltpu.make_async_copy(ybuf.at[0], y_hbm.at[pl.ds(y_hbm.shape[0] - BM, BM), :], sem_out.at[0])
        spare.start()
        spare.wait()
        gather(st_ref, 0)

    @pl.when(i + 1 < n_used)
    def _():
        gather(stn_ref, 1 - slot)

    @pl.when(i < n_used)
    def _():
        wait_gather(slot)

        @pl.when(i >= 2)
        def _():
            wait_scatter(slot)

        x = xbuf[slot].astype(BF16)
        g = jnp.dot(x, wg_ref[0, 0].astype(BF16), preferred_element_type=F32)
        u = jnp.dot(x, wu_ref[0, 0].astype(BF16), preferred_element_type=F32)
        ybuf[slot] = jnp.dot((_silu(g) * u).astype(BF16), wd_ref[0, 0].astype(BF16), preferred_element_type=F32)

        def body(r, c):
            pltpu.make_async_copy(ybuf.at[slot, pl.ds(r, 1), :], y_hbm.at[pl.ds(sd_ref[0, 0, r], 1), :],
                                  sem_out.at[slot]).start()
            return c
        lax.fori_loop(0, BM, body, 0, unroll=DMA_UNROLL)

        @pl.when(i == n_used - 1)
        def _():
            wait_scatter(slot)

            @pl.when(i >= 1)
            def _():
                wait_scatter(1 - slot)


def _expert_call(tok, tables, layer, wg, wu, wd):
    m = tok.shape[0]
    slot_tok, slot_dst, block_expert, n_used = tables
    nb = slot_tok.shape[0]
    smem_blk = lambda f: pl.BlockSpec((1, 1, BM), lambda i, be, nu: (f(i), 0, 0), memory_space=pltpu.SMEM)
    grid_spec = pltpu.PrefetchScalarGridSpec(
        num_scalar_prefetch=2,
        grid=(nb,),
        in_specs=[smem_blk(lambda i: i), smem_blk(lambda i: jnp.minimum(i + 1, nb - 1)), smem_blk(lambda i: i),
                  pl.BlockSpec(memory_space=pl.ANY),
                  pl.BlockSpec((1, 1, D_MODEL, EXPERT_FF), lambda i, be, nu: (layer, be[i], 0, 0)),
                  pl.BlockSpec((1, 1, D_MODEL, EXPERT_FF), lambda i, be, nu: (layer, be[i], 0, 0)),
                  pl.BlockSpec((1, 1, EXPERT_FF, D_MODEL), lambda i, be, nu: (layer, be[i], 0, 0))],
        out_specs=pl.BlockSpec(memory_space=pl.ANY),
        scratch_shapes=[pltpu.VMEM((2, BM, D_MODEL), F32), pltpu.VMEM((2, BM, D_MODEL), F32),
                        pltpu.SemaphoreType.DMA((2,)), pltpu.SemaphoreType.DMA((2,))])
    return pl.pallas_call(
        _expert_kernel, grid_spec=grid_spec,
        out_shape=jax.ShapeDtypeStruct((TOP_K * m + BM, D_MODEL), F32),
        compiler_params=pltpu.CompilerParams(vmem_limit_bytes=VMEM_LIMIT),
    )(block_expert, n_used, slot_tok, slot_tok, slot_dst, tok, wg, wu, wd)


def _combine_kernel(x_ref, sh_ref, w_ref, mod_ref, *rest):
    y_refs, o_ref = rest[:TOP_K], rest[TOP_K]
    w = w_ref[...]
    acc = sh_ref[...]
    for k, y_ref in enumerate(y_refs):
        acc = acc + w[:, k:k + 1] * y_ref[...]
    o_ref[...] = x_ref[...] + mod_ref[0, 0, 5:6, :] * acc


def _combine_call(x, sh, wts_rows, mod, layer, yrows):
    nt = M_TOK // TM
    row = pl.BlockSpec((TM, D_MODEL), lambda i: (i, 0))
    ysp = [pl.BlockSpec((TM, D_MODEL), (lambda i, k=k: (k * nt + i, 0))) for k in range(TOP_K)]
    return pl.pallas_call(
        _combine_kernel,
        grid=(nt,),
        in_specs=[row, row, pl.BlockSpec((TM, 8), lambda i: (i, 0)),
                  pl.BlockSpec((1, 1, 6, D_MODEL), lambda i: (layer, _mod_group(i), 0, 0))] + ysp,
        out_specs=row,
        out_shape=jax.ShapeDtypeStruct((M_TOK, D_MODEL), F32),
        compiler_params=pltpu.CompilerParams(vmem_limit_bytes=VMEM_LIMIT),
    )(x, sh, wts_rows, mod, *([yrows] * TOP_K))


def kernel(x_prompt, x_sample, cache_attn_k, cache_attn_v, state_rwkv, state_delta, cache_diff_k,
           cache_diff_v, c, c_ctx, mod_w, mod_b, norm1_g, norm2_g, ev_w_in, ev_w_out, a_qn, a_kn, b_mu,
           b_w0, b_w2, b_a0, b_a2, b_g2, b_kk, b_ka, b_rk, b_lnx_w, b_lnx_b, od_w_in, od_w_out, c_conv,
           c_A_log, c_dt_bias, c_onorm, d_qn, d_kn, d_lambda, d_subln, router_w, router_bias, exp_w_gate,
           exp_w_up, exp_w_down, sh_w_gate, sh_w_up, sh_w_down):
    x = jnp.concatenate([x_prompt.reshape(N_CTX, D_MODEL), x_sample.reshape(N_LAT, D_MODEL)], axis=0)
    cond = jnp.concatenate([c_ctx[None], c, jnp.zeros((8 - N_MOD, D_MODEL), F32)], axis=0)
    mod = _adaln_call(cond, mod_w, mod_b)[:, 0:N_MOD].reshape(DEPTH, N_MOD, 6, D_MODEL)
    lat_blk = N_CTX // DEC_SEQ
    new_ak, new_av, new_sr, new_sd, new_dk, new_dv = [], [], [], [], [], []
    for l in range(DEPTH):
        j = l // 2
        if l % 2 == 0:
            w = ev_w_in[j]
            q, k, v = w[:, 0:512], w[:, 512:640], w[:, 640:768]
            w_a = jnp.concatenate([q[:, 0:256], k[:, 0:64], v[:, 0:64], q[:, 256:], k[:, 64:], v[:, 64:]], axis=1)
            amat, bmat = _proj_in_call(x, mod, l, norm1_g[l], [w_a.astype(BF16), w[:, 768:].astype(BF16)])
            gain = jnp.concatenate([jnp.tile(a_qn[j], A_GROUP), a_kn[j], jnp.ones((HD,), F32)]).reshape(1, GW)
            oa_c, nk, nv = _attn_a_call(amat, 0, BATCH, SEQ, gain)
            oa_l, = _attn_a_call(amat, lat_blk, DEC_BATCH, DEC_SEQ, gain, (cache_attn_k, cache_attn_v, j))
            p = {'b_mu': b_mu[j], 'b_w0': b_w0[j], 'b_w2': b_w2[j], 'b_a0': b_a0[j], 'b_a2': b_a2[j],
                 'b_g2': b_g2[j], 'b_kk': b_kk[j], 'b_ka': b_ka[j], 'b_rk': b_rk[j], 'b_lnx_w': b_lnx_w[j],
                 'b_lnx_b': b_lnx_b[j]}
            ob_c, sr = _rwkv_call(bmat, 0, BATCH, SEQ, p)
            ob_l, _ = _rwkv_call(bmat, lat_blk, DEC_BATCH, DEC_SEQ, p, state_rwkv[:, j])
            o1 = jnp.concatenate([oa_c, oa_l], axis=0)
            o2 = jnp.concatenate([ob_c, ob_l], axis=0)
            w_out = ev_w_out[j]
            new_ak.append(nk)
            new_av.append(nv)
            new_sr.append(sr)
        else:
            lam_init = 0.8 - 0.6 * math.exp(-0.3 * l)
            w = od_w_in[j]
            s0 = 4 * C_WIDTH + 4 * C_HEADS
            dq, dk, dv = w[:, s0:s0 + 512], w[:, s0 + 512:s0 + 1024], w[:, s0 + 1024:]
            w_ab = jnp.pad(w[:, 4 * C_WIDTH:s0], ((0, 0), (0, LANE - 4 * C_HEADS)))
            w_d = jnp.concatenate([jnp.concatenate([dq[:, 128 * h:128 * (h + 1)], dk[:, 128 * h:128 * (h + 1)],
                                                    dv[:, 128 * h:128 * (h + 1)]], axis=1) for h in range(D_HEADS)],
                                  axis=1)
            cmat, abmat, dmat = _proj_in_call(x, mod, l, norm1_g[l],
                                              [w[:, 0:4 * C_WIDTH].astype(BF16), w_ab.astype(BF16), w_d.astype(BF16)])
            p = {'c_conv': c_conv[j], 'c_A_log': c_A_log[j], 'c_dt_bias': c_dt_bias[j], 'c_onorm': c_onorm[j]}
            oc_c, sd_ = _delta_call(cmat, abmat, 0, BATCH, SEQ, p)
            oc_l, _ = _delta_call(cmat, abmat, lat_blk, DEC_BATCH, DEC_SEQ, p, state_delta[:, j])
            gain = jnp.concatenate([jnp.tile(d_qn[j], 2), jnp.tile(d_kn[j], 2), jnp.ones((D_VDIM,), F32)]).reshape(1, GW)
            od_c, ndk, ndv = _attn_d_call(dmat, 0, BATCH, SEQ, gain, d_lambda[j], d_subln[j], lam_init)
            od_l, = _attn_d_call(dmat, lat_blk, DEC_BATCH, DEC_SEQ, gain, d_lambda[j], d_subln[j], lam_init,
                                 (cache_diff_k, cache_diff_v, j))
            o1 = jnp.concatenate([oc_c, oc_l], axis=0)
            o2 = jnp.concatenate([od_c, od_l], axis=0)
            w_out = od_w_out[j]
            new_dk.append(ndk)
            new_dv.append(ndv)
            new_sd.append(sd_)
        x, tok, sh, logits_t = _proj_out_call(
            o1, o2, x, mod, l, norm2_g[l], w_out.astype(BF16), router_w[l].T.astype(BF16),
            sh_w_gate[l].astype(BF16), sh_w_up[l].astype(BF16), sh_w_down[l].astype(BF16))
        idx, wts = _route_call(logits_t, router_bias[l])
        tables = _dispatch_tables(idx, M_TOK)
        yrows = _expert_call(tok, tables, l, exp_w_gate, exp_w_up, exp_w_down)
        x = _combine_call(x, sh, wts.T, mod, l, yrows)
    return (x[0:N_CTX].reshape(BATCH, SEQ, D_MODEL), x[N_CTX:].reshape(DEC_BATCH, DEC_SEQ, D_MODEL),
            jnp.stack(new_ak, axis=1), jnp.stack(new_av, axis=1), jnp.stack(new_sr, axis=1),
            jnp.stack(new_sd, axis=1), jnp.stack(new_dk, axis=1), jnp.stack(new_dv, axis=1))
```

```python
import functools
import math
import jax
import jax.numpy as jnp
from jax import lax
from jax.experimental import pallas as pl
from jax.experimental.pallas import tpu as pltpu

D_MODEL = 1024
BATCH = 32
SEQ = 256
DEPTH = 4
DEC_BATCH = 2
DEC_SEQ = 1024
PAST_LEN = 512
GRID_W = 64
NORM_EPS = 1e-6
ROPE_THETA = 10000.0
A_HEADS = 8
A_KV_HEADS = 2
A_GROUP = A_HEADS // A_KV_HEADS
B_HEADS = 8
B_DIM = 64
B_WIDTH = B_HEADS * B_DIM
DECAY_LORA = 64
ICLR_LORA = 64
GATE_LORA = 128
B_COLS = 3 * B_WIDTH + DECAY_LORA + ICLR_LORA + GATE_LORA
LNX_EPS = 64e-5
C_HEADS = 4
C_DIM = 128
C_WIDTH = C_HEADS * C_DIM
D_HEADS = 4
D_VDIM = 128
N_EXPERTS = 64
TOP_K = 6
N_GROUPS = 8
TOPK_GROUPS = 4
PER_GROUP = N_EXPERTS // N_GROUPS
EXPERT_FF = 256
SHARED_FF = 256
ROUTED_SCALE = 1.0

HD = 64
GW = 384
CH = 64
LANE = 128
BM = 128
DMA_UNROLL = 8
TM = 512
SEQ_BLK = 256
RWKV_SOLVE_BLK = 128
N_CTX = BATCH * SEQ
N_LAT = DEC_BATCH * DEC_SEQ
M_TOK = N_CTX + N_LAT
N_MOD = 1 + DEC_BATCH
VMEM_LIMIT = 56 * 1024 * 1024

F32 = jnp.float32
BF16 = jnp.bfloat16
HI = lax.Precision.HIGHEST
NEG = -jnp.inf


def _dot(a, b):
    return jnp.dot(a.astype(BF16), b.astype(BF16), preferred_element_type=F32)


def _dot_nt(a, b):
    return lax.dot_general(a.astype(BF16), b.astype(BF16), (((1,), (1,)), ((), ())), preferred_element_type=F32)


def _dot_tn(a, b):
    return lax.dot_general(a.astype(BF16), b.astype(BF16), (((0,), (0,)), ((), ())), preferred_element_type=F32)


def _dot_hi(a, b):
    return jnp.dot(a, b, preferred_element_type=F32, precision=HI)


def _silu(x):
    return x * jax.nn.sigmoid(x)


def _mod_group(i):
    n_ctx_tiles = N_CTX // TM
    return jnp.where(i < n_ctx_tiles, 0, 1 + (i - n_ctx_tiles) // (DEC_SEQ // TM))


def _full(shape):
    return pl.BlockSpec(shape, lambda *_: (0,) * len(shape))


def _adaln_kernel(c_ref, w_ref, b_ref, o_ref):
    o_ref[0] = _dot(_silu(c_ref[...]), w_ref[0]) + b_ref[0]


def _adaln_call(cond, mod_w, mod_b):
    n = 6
    return pl.pallas_call(
        _adaln_kernel,
        grid=(DEPTH, n),
        in_specs=[_full((8, D_MODEL)),
                  pl.BlockSpec((1, D_MODEL, D_MODEL), lambda l, j: (l, 0, j)),
                  pl.BlockSpec((1, 1, D_MODEL), lambda l, j: (l, 0, j))],
        out_specs=pl.BlockSpec((1, 8, D_MODEL), lambda l, j: (l, 0, j)),
        out_shape=jax.ShapeDtypeStruct((DEPTH, 8, n * D_MODEL), F32),
    )(cond, mod_w, mod_b.reshape(DEPTH, 1, n * D_MODEL))


def _proj_in_kernel(n_out, x_ref, mod_ref, g_ref, *rest):
    x = x_ref[...]
    y = x * lax.rsqrt(jnp.mean(x * x, axis=-1, keepdims=True) + NORM_EPS) * g_ref[...]
    h = (y * (1.0 + mod_ref[0, 0, 1:2, :]) + mod_ref[0, 0, 0:1, :]).astype(BF16)
    for w_ref, o_ref in zip(rest[:n_out], rest[n_out:]):
        o_ref[...] = jnp.dot(h, w_ref[...], preferred_element_type=F32)


def _proj_in_call(x, mod, layer, g, weights):
    n_out = len(weights)
    return pl.pallas_call(
        functools.partial(_proj_in_kernel, n_out),
        grid=(M_TOK // TM,),
        in_specs=[pl.BlockSpec((TM, D_MODEL), lambda i: (i, 0)),
                  pl.BlockSpec((1, 1, 6, D_MODEL), lambda i: (layer, _mod_group(i), 0, 0)),
                  _full((1, D_MODEL))] + [_full(w.shape) for w in weights],
        out_specs=[pl.BlockSpec((TM, w.shape[1]), lambda i: (i, 0)) for w in weights],
        out_shape=[jax.ShapeDtypeStruct((M_TOK, w.shape[1]), F32) for w in weights],
        compiler_params=pltpu.CompilerParams(vmem_limit_bytes=VMEM_LIMIT),
    )(x, mod, g.reshape(1, D_MODEL), *weights)


def _proj_out_kernel(o1_ref, o2_ref, x_ref, mod_ref, g_ref, wo_ref, rwt_ref, sg_ref, su_ref, sd_ref,
                     xn_ref, tok_ref, sh_ref, lg_ref):
    o = jnp.concatenate([o1_ref[...], o2_ref[...]], axis=1).astype(BF16)
    xn = x_ref[...] + mod_ref[0, 0, 2:3, :] * jnp.dot(o, wo_ref[...], preferred_element_type=F32)
    xn_ref[...] = xn
    y = xn * lax.rsqrt(jnp.mean(xn * xn, axis=-1, keepdims=True) + NORM_EPS) * g_ref[...]
    h2 = y * (1.0 + mod_ref[0, 0, 4:5, :]) + mod_ref[0, 0, 3:4, :]
    tok_ref[...] = h2
    hb = h2.astype(BF16)
    lg_ref[...] = _dot_nt(rwt_ref[...], hb)
    hid = _silu(jnp.dot(hb, sg_ref[...], preferred_element_type=F32)) * jnp.dot(hb, su_ref[...], preferred_element_type=F32)
    sh_ref[...] = jnp.dot(hid.astype(BF16), sd_ref[...], preferred_element_type=F32)


def _proj_out_call(o1, o2, x, mod, layer, g2, w_out, rw_t, sg, su, sd):
    row = lambda n: pl.BlockSpec((TM, n), lambda i: (i, 0))
    return pl.pallas_call(
        _proj_out_kernel,
        grid=(M_TOK // TM,),
        in_specs=[row(o1.shape[1]), row(o2.shape[1]), row(D_MODEL),
                  pl.BlockSpec((1, 1, 6, D_MODEL), lambda i: (layer, _mod_group(i), 0, 0)),
                  _full((1, D_MODEL)), _full(w_out.shape), _full(rw_t.shape), _full(sg.shape), _full(su.shape),
                  _full(sd.shape)],
        out_specs=[row(D_MODEL), row(D_MODEL), row(D_MODEL), pl.BlockSpec((N_EXPERTS, TM), lambda i: (0, i))],
        out_shape=[jax.ShapeDtypeStruct((M_TOK, D_MODEL), F32)] * 3 + [jax.ShapeDtypeStruct((N_EXPERTS, M_TOK), F32)],
        compiler_params=pltpu.CompilerParams(vmem_limit_bytes=VMEM_LIMIT),
    )(o1, o2, x, mod, g2.reshape(1, D_MODEL), w_out, rw_t, sg, su, sd)


def _rope_lane_tables(n_tok):
    rows = n_tok // GRID_W
    row = jnp.repeat(jnp.arange(rows, dtype=F32), GRID_W)
    col = jnp.tile(jnp.arange(GRID_W, dtype=F32), rows)
    n_freq = HD // 4
    inv = ROPE_THETA ** (-jnp.arange(n_freq, dtype=F32) / n_freq)
    ang = jnp.concatenate([row[:, None] * inv, col[:, None] * inv], axis=-1)
    cos = jnp.repeat(jnp.cos(ang), 2, axis=-1)
    sin = jnp.repeat(jnp.sin(ang), 2, axis=-1) * jnp.tile(jnp.array([-1.0, 1.0], F32), HD // 2)
    return cos, sin


def _seg_matrix(width, seg, value):
    li = lax.broadcasted_iota(jnp.int32, (width, width), 0) // seg
    lj = lax.broadcasted_iota(jnp.int32, (width, width), 1) // seg
    return jnp.where(li == lj, value, 0.0).astype(F32)


def _norm_rope(x, gain, n_norm, cos, sin):
    lane = lax.broadcasted_iota(jnp.int32, x.shape, 1)
    ms = _dot_hi(x * x, _seg_matrix(x.shape[1], HD, 1.0 / HD))
    xn = jnp.where(lane < n_norm, x * lax.rsqrt(ms + NORM_EPS) * gain, x)
    if cos is None:
        return xn, xn
    w = x.shape[1]
    swapped = jnp.where(lane % 2 == 0, pltpu.roll(xn, w - 1, 1), pltpu.roll(xn, 1, 1))
    return xn, xn * cos + swapped * sin


def _softmax_pv(s, v):
    m = jnp.max(s, axis=-1, keepdims=True)
    p = jnp.exp(s - m)
    l = jnp.sum(p, axis=-1, keepdims=True)
    return _dot(p, v) / l


def _attn_a_kernel(T, P, TQ, *refs):
    if P:
        x_ref, gain_ref, cos_ref, sin_ref, ck_ref, cv_ref, o_ref, xr_s = refs
        xn, xr = _norm_rope(x_ref[...], gain_ref[...], 5 * HD, cos_ref[...], sin_ref[...])
    else:
        x_ref, gain_ref, o_ref, nk_ref, nv_ref, xr_s = refs
        xn, xr = _norm_rope(x_ref[...], gain_ref[...], 5 * HD, None, None)
        nk_ref[0, 0] = xn[:, 4 * HD:5 * HD]
        nv_ref[0, 0] = xn[:, 5 * HD:6 * HD]
    xr_s[...] = xr
    k_new = xr[:, 4 * HD:5 * HD]
    v_new = xr[:, 5 * HD:6 * HD]
    if P:
        k_all = jnp.concatenate([ck_ref[0, 0, 0], k_new], axis=0).astype(BF16)
        v_all = jnp.concatenate([cv_ref[0, 0, 0], v_new], axis=0).astype(BF16)
    else:
        k_all, v_all = k_new.astype(BF16), v_new.astype(BF16)

    def q_block(qb, carry):
        rows = pl.ds(pl.multiple_of(qb * TQ, TQ), TQ)
        qx = xr_s[rows, 0:A_GROUP * HD]
        qs = jnp.concatenate([qx[:, i * HD:(i + 1) * HD] for i in range(A_GROUP)], axis=0)
        o = _softmax_pv(_dot_nt(qs, k_all) * (HD ** -0.5), v_all)
        o_ref[rows, :] = jnp.concatenate([o[i * TQ:(i + 1) * TQ, :] for i in range(A_GROUP)], axis=1)
        return carry

    lax.fori_loop(0, T // TQ, q_block, 0)


def _attn_a_call(amat, row_off_blocks, n_seq, T, gain, cache=None):
    P = 0 if cache is None else cache[0].shape[3]
    TQ = T if P == 0 else 128
    in_specs = [pl.BlockSpec((T, GW), lambda i, g: (row_off_blocks + i, g)), _full((1, GW))]
    args = [amat, gain]
    out_specs = [pl.BlockSpec((T, A_GROUP * HD), lambda i, g: (i, g))]
    out_shape = [jax.ShapeDtypeStruct((n_seq * T, A_HEADS * HD), F32)]
    if P:
        ck, cv, j = cache
        cos, sin = _rope_lane_tables(T)
        cos_g = jnp.concatenate([cos] * 5 + [jnp.ones((T, HD), F32)], axis=1)
        sin_g = jnp.concatenate([sin] * 5 + [jnp.zeros((T, HD), F32)], axis=1)
        in_specs += [_full((T, GW)), _full((T, GW)),
                     pl.BlockSpec((1, 1, 1, P, HD), lambda i, g: (i, j, g, 0, 0)),
                     pl.BlockSpec((1, 1, 1, P, HD), lambda i, g: (i, j, g, 0, 0))]
        args += [cos_g, sin_g, ck, cv]
    else:
        out_specs += [pl.BlockSpec((1, 1, T, HD), lambda i, g: (i, g, 0, 0))] * 2
        out_shape += [jax.ShapeDtypeStruct((n_seq, A_KV_HEADS, T, HD), F32)] * 2
    return pl.pallas_call(
        functools.partial(_attn_a_kernel, T, P, TQ),
        grid=(n_seq, A_KV_HEADS),
        in_specs=in_specs, out_specs=out_specs, out_shape=out_shape,
        scratch_shapes=[pltpu.VMEM((T, GW), F32)],
        compiler_params=pltpu.CompilerParams(vmem_limit_bytes=VMEM_LIMIT),
    )(*args)


def _attn_d_kernel(T, P, TQ, lam_init, *refs):
    if P:
        x_ref, gain_ref, lam_ref, sub_ref, cos_ref, sin_ref, ck_ref, cv_ref, o_ref, xr_s = refs
        xn, xr = _norm_rope(x_ref[...], gain_ref[...], 4 * HD, cos_ref[...], sin_ref[...])
    else:
        x_ref, gain_ref, lam_ref, sub_ref, o_ref, nk_ref, nv_ref, xr_s = refs
        xn, xr = _norm_rope(x_ref[...], gain_ref[...], 4 * HD, None, None)
        nk_ref[0, 0, 0] = xn[:, 2 * HD:3 * HD]
        nk_ref[0, 0, 1] = xn[:, 3 * HD:4 * HD]
        nv_ref[0, 0] = xn[:, 4 * HD:]
    xr_s[...] = xr
    v_new = xr[:, 4 * HD:]
    ks = []
    for m in range(2):
        k_new = xr[:, (2 + m) * HD:(3 + m) * HD]
        if P:
            ks.append(jnp.concatenate([ck_ref[0, 0, 0, m], k_new], axis=0).astype(BF16))
        else:
            ks.append(k_new.astype(BF16))
    v_all = (jnp.concatenate([cv_ref[0, 0, 0], v_new], axis=0) if P else v_new).astype(BF16)
    lm = lam_ref[...]
    lam = (jnp.exp(jnp.sum(lm[0:1, :] * lm[1:2, :], axis=-1, keepdims=True))
           - jnp.exp(jnp.sum(lm[2:3, :] * lm[3:4, :], axis=-1, keepdims=True)) + lam_init)

    def q_block(qb, carry):
        rows = pl.ds(pl.multiple_of(qb * TQ, TQ), TQ)
        qx = xr_s[rows, 0:2 * HD]
        o1 = _softmax_pv(_dot_nt(qx[:, 0:HD], ks[0]) * (HD ** -0.5), v_all)
        o2 = _softmax_pv(_dot_nt(qx[:, HD:], ks[1]) * (HD ** -0.5), v_all)
        od = o1 - lam * o2
        od = od * lax.rsqrt(jnp.mean(od * od, axis=-1, keepdims=True) + NORM_EPS) * sub_ref[...]
        o_ref[rows, :] = od * (1.0 - lam_init)
        return carry

    lax.fori_loop(0, T // TQ, q_block, 0)


def _attn_d_call(dmat, row_off_blocks, n_seq, T, gain, lam, subln, lam_init, cache=None):
    P = 0 if cache is None else cache[0].shape[4]
    TQ = T if P == 0 else 256
    in_specs = [pl.BlockSpec((T, GW), lambda i, h: (row_off_blocks + i, h)),
                _full((1, GW)), _full((4, HD)), _full((1, D_VDIM))]
    args = [dmat, gain, lam, subln.reshape(1, -1)]
    out_specs = [pl.BlockSpec((T, D_VDIM), lambda i, h: (i, h))]
    out_shape = [jax.ShapeDtypeStruct((n_seq * T, D_HEADS * D_VDIM), F32)]
    if P:
        ck, cv, j = cache
        cos, sin = _rope_lane_tables(T)
        cos_g = jnp.concatenate([cos] * 4 + [jnp.ones((T, D_VDIM), F32)], axis=1)
        sin_g = jnp.concatenate([sin] * 4 + [jnp.zeros((T, D_VDIM), F32)], axis=1)
        in_specs += [_full((T, GW)), _full((T, GW)),
                     pl.BlockSpec((1, 1, 1, 2, P, HD), lambda i, h: (i, j, h, 0, 0, 0)),
                     pl.BlockSpec((1, 1, 1, P, D_VDIM), lambda i, h: (i, j, h, 0, 0))]
        args += [cos_g, sin_g, ck, cv]
    else:
        out_specs += [pl.BlockSpec((1, 1, 2, T, HD), lambda i, h: (i, h, 0, 0, 0)),
                      pl.BlockSpec((1, 1, T, D_VDIM), lambda i, h: (i, h, 0, 0))]
        out_shape += [jax.ShapeDtypeStruct((n_seq, D_HEADS, 2, T, HD), F32),
                      jax.ShapeDtypeStruct((n_seq, D_HEADS, T, D_VDIM), F32)]
    return pl.pallas_call(
        functools.partial(_attn_d_kernel, T, P, TQ, lam_init),
        grid=(n_seq, D_HEADS),
        in_specs=in_specs, out_specs=out_specs, out_shape=out_shape,
        scratch_shapes=[pltpu.VMEM((T, GW), F32)],
        compiler_params=pltpu.CompilerParams(vmem_limit_bytes=VMEM_LIMIT),
    )(*args)


def _dot_3pass(a, b):
    a_hi = a.astype(BF16)
    a_lo = (a - a_hi.astype(F32)).astype(BF16)
    b_hi = b.astype(BF16)
    b_lo = (b - b_hi.astype(F32)).astype(BF16)
    d = lambda x, y: jnp.dot(x, y, preferred_element_type=F32)
    return d(a_hi, b_hi) + (d(a_hi, b_lo) + d(a_lo, b_hi))


def _unit_lower_solve(a_bd, x, blk):
    n_stage = CH.bit_length() - 1
    outs = []
    for g in range(a_bd.shape[0] // blk):
        sl = slice(g * blk, (g + 1) * blk)
        am, xm = a_bd[sl, sl], x[sl, :]
        for s in range(n_stage):
            xm = xm + _dot_3pass(am, xm)
            if s < n_stage - 1:
                am = _dot_3pass(am, am)
        outs.append(xm)
    return jnp.concatenate(outs, axis=0)


def _head_stack(x, n_heads, width):
    return jnp.concatenate([x[:, h * width:(h + 1) * width] for h in range(n_heads)], axis=0)


def _head_unstack(x, n_heads):
    return jnp.concatenate([x[h * CH:(h + 1) * CH, :] for h in range(n_heads)], axis=1)


def _head_masks(n_heads, d):
    hc = n_heads * CH
    ri = lax.broadcasted_iota(jnp.int32, (hc, hc), 0)
    rj = lax.broadcasted_iota(jnp.int32, (hc, hc), 1)
    same = (ri // CH) == (rj // CH)
    return (same & (ri >= rj), same & (ri > rj)) if d == 0 else (same & (ri <= rj), same & (ri < rj))


def _chunk_tri(d):
    ci = lax.broadcasted_iota(jnp.int32, (CH, CH), 0)
    cj = lax.broadcasted_iota(jnp.int32, (CH, CH), 1)
    return jnp.where(ci >= cj if d == 0 else ci <= cj, 1.0, 0.0).astype(F32)


def _rwkv_kernel(T, has_state, *refs):
    if has_state:
        (xb_ref, mu_ref, w0_ref, w2_ref, a0_ref, a2_ref, g2_ref, kk_ref, ka_ref, rk_ref, lnw_ref, lnb_ref, s0_ref,
         ob_ref, sf_ref, r_s, v_s, av_s, lw_s, kd_s, bv_s, y_s, bon_s, gate_s, st_s, mbd_s, mst_s, min_s) = refs
    else:
        (xb_ref, mu_ref, w0_ref, w2_ref, a0_ref, a2_ref, g2_ref, kk_ref, ka_ref, rk_ref, lnw_ref, lnb_ref,
         ob_ref, sf_ref, r_s, v_s, av_s, lw_s, kd_s, bv_s, y_s, bon_s, gate_s, st_s, mbd_s, mst_s, min_s) = refs
    n_chunks = T // CH
    n_blk = T // SEQ_BLK
    hc = B_HEADS * CH
    seg = _seg_matrix(B_WIDTH, B_DIM, 1.0)
    row = lax.broadcasted_iota(jnp.int32, (SEQ_BLK, 1), 0)
    for b in range(n_blk):
        r0 = b * SEQ_BLK
        rb = slice(r0, r0 + SEQ_BLK)
        x = xb_ref[rb, :]
        before = xb_ref[r0 - 1:r0, :] if b > 0 else jnp.zeros((1, B_COLS), F32)
        after = xb_ref[r0 + SEQ_BLK:r0 + SEQ_BLK + 1, :] if b < n_blk - 1 else jnp.zeros((1, B_COLS), F32)
        prev = jnp.where(row == 0, before, pltpu.roll(x, 1, 0))
        nxt = jnp.where(row == SEQ_BLK - 1, after, pltpu.roll(x, SEQ_BLK - 1, 0))
        xs = x + mu_ref[...] * (0.5 * (prev + nxt) - x)
        r = xs[:, 0:B_WIDTH]
        kb = xs[:, B_WIDTH:2 * B_WIDTH]
        vb = xs[:, 2 * B_WIDTH:3 * B_WIDTH]
        wd = xs[:, 3 * B_WIDTH:3 * B_WIDTH + DECAY_LORA]
        ad = xs[:, 3 * B_WIDTH + DECAY_LORA:3 * B_WIDTH + DECAY_LORA + ICLR_LORA]
        gd = xs[:, 3 * B_WIDTH + DECAY_LORA + ICLR_LORA:]
        kk = kb * kk_ref[...]
        kkn = kk * lax.rsqrt(_dot_hi(kk * kk, seg) + 1e-6)
        r_s[rb, :] = r
        v_s[rb, :] = vb
        av_s[rb, :] = -kkn
        bon_s[rb, :] = _dot_hi(r * kb * rk_ref[...], seg) * vb
        gate_s[rb, :] = _dot(jax.nn.sigmoid(gd), g2_ref[...])
        twd = jnp.tanh(wd)
        for d in range(2):
            wl = w0_ref[d:d + 1, :] + _dot(twd, w2_ref[d])
            w_log = -jax.nn.softplus(-wl) - 0.5
            lw_s[d, rb, :] = -jnp.exp(w_log)
            a = jax.nn.sigmoid(a0_ref[d:d + 1, :] + _dot(ad, a2_ref[d]))
            kd_s[d, rb, :] = kb * (1.0 + (a - 1.0) * ka_ref[...])
            bv_s[d, rb, :] = kkn * a
    if has_state:
        for d in range(2):
            st_s[d] = jnp.concatenate([s0_ref[0, d, h] for h in range(B_HEADS)], axis=1)
    else:
        st_s[...] = jnp.zeros_like(st_s)
    incl0, _ = _head_masks(B_HEADS, 0)
    incl1, _ = _head_masks(B_HEADS, 1)
    mbd_s[...] = jnp.where(incl0 | incl1, 1.0, 0.0).astype(F32)

    def expand(x):
        return jnp.concatenate([x] * B_HEADS, axis=1) * mbd_s[...]

    for d in range(2):
        incl, strict = _head_masks(B_HEADS, d)
        min_s[...] = jnp.where(incl, 1.0, 0.0).astype(F32)
        mst_s[...] = jnp.where(strict, 1.0, 0.0).astype(F32)
        tri = _chunk_tri(d)
        last = CH - 1 if d == 0 else 0

        def chunk_body(it, carry, d=d, tri=tri, last=last):
            c = it if d == 0 else n_chunks - 1 - it
            rows = pl.ds(pl.multiple_of(c * CH, CH), CH)
            lwc = lw_s[d, rows, :]
            cum = _dot_hi(tri, lwc)
            e_pos = jnp.exp(cum)
            e_neg = jnp.exp(-cum)
            rt = _head_stack(r_s[rows, :] * e_pos, B_HEADS, B_DIM)
            at = _head_stack(av_s[rows, :] * jnp.exp(cum - lwc), B_HEADS, B_DIM)
            bk = jnp.concatenate([_head_stack(bv_s[d, rows, :] * e_neg, B_HEADS, B_DIM),
                                  _head_stack(kd_s[d, rows, :] * e_neg, B_HEADS, B_DIM)], axis=0)
            vs = _head_stack(v_s[rows, :], B_HEADS, B_DIM)
            pc = e_pos[last:last + 1, :]
            g1 = _dot_nt(at, bk)
            g2 = _dot_nt(rt, bk)
            mst = mst_s[...]
            a_ab = g1[:, 0:hc] * mst
            a_ak = g1[:, hc:] * mst
            mi = min_s[...]
            a_r = jnp.concatenate([g2[:, 0:hc] * mi, g2[:, hc:] * mi], axis=1)
            xm = _unit_lower_solve(a_ab, jnp.concatenate([at, _dot(a_ak, vs)], axis=1), RWKV_SOLVE_BLK)
            s_old = st_s[d]
            ws = _dot_nt(jnp.concatenate([expand(xm[:, 0:B_DIM]), expand(rt)], axis=0), s_old)
            uv = jnp.concatenate([ws[0:hc, :] + xm[:, B_DIM:], vs], axis=0)
            ys = ws[hc:, :] + _dot(a_r, uv)
            bk_e = jnp.concatenate([expand(bk[0:hc, :]), expand(bk[hc:, :])], axis=0)
            st_s[d] = (s_old + _dot_tn(uv, bk_e)) * pc
            yc = _head_unstack(ys, B_HEADS)
            if d == 0:
                y_s[rows, :] = yc
            else:
                y_s[rows, :] = y_s[rows, :] + yc
            return carry

        lax.fori_loop(0, n_chunks, chunk_body, 0)
    for b in range(n_blk):
        rb = slice(b * SEQ_BLK, (b + 1) * SEQ_BLK)
        y = y_s[rb, :]
        yc = y - _dot_hi(y, seg) * (1.0 / B_DIM)
        var = _dot_hi(yc * yc, seg) * (1.0 / B_DIM)
        yn = yc * lax.rsqrt(var + LNX_EPS) * lnw_ref[...] + lnb_ref[...]
        ob_ref[rb, :] = (yn + bon_s[rb, :]) * gate_s[rb, :]
    for d in range(2):
        sd = st_s[d]
        for h in range(B_HEADS):
            sf_ref[0, d, h] = sd[:, h * B_DIM:(h + 1) * B_DIM]


def _seq_spec(T, cols, row_off_blocks):
    extra = {'pipeline_mode': pl.Buffered(1)} if T > SEQ_BLK else {}
    return pl.BlockSpec((T, cols), lambda i: (row_off_blocks + i, 0), **extra)


def _rwkv_call(xb, row_off_blocks, n_seq, T, p, s0=None):
    has_state = s0 is not None
    st_shape = (2, B_HEADS, B_DIM, B_DIM)
    hc = B_HEADS * CH
    in_specs = [_seq_spec(T, B_COLS, row_off_blocks),
                _full((1, B_COLS)), _full((2, B_WIDTH)), _full((2, DECAY_LORA, B_WIDTH)), _full((2, B_WIDTH)),
                _full((2, ICLR_LORA, B_WIDTH)), _full((GATE_LORA, B_WIDTH))] + [_full((1, B_WIDTH))] * 5
    args = [xb, p['b_mu'].reshape(1, -1), p['b_w0'], p['b_w2'], p['b_a0'], p['b_a2'], p['b_g2'],
            p['b_kk'].reshape(1, -1), p['b_ka'].reshape(1, -1), p['b_rk'].reshape(1, -1),
            p['b_lnx_w'].reshape(1, -1), p['b_lnx_b'].reshape(1, -1)]
    if has_state:
        in_specs.append(pl.BlockSpec((1,) + st_shape, lambda i: (i, 0, 0, 0, 0)))
        args.append(s0)
    scr = [pltpu.VMEM((T, B_WIDTH), F32)] * 3 + [pltpu.VMEM((2, T, B_WIDTH), F32)] * 3 + \
          [pltpu.VMEM((T, B_WIDTH), F32)] * 3 + [pltpu.VMEM((2, B_DIM, B_WIDTH), F32)] + \
          [pltpu.VMEM((hc, hc), F32)] * 3
    return pl.pallas_call(
        functools.partial(_rwkv_kernel, T, has_state),
        grid=(n_seq,),
        in_specs=in_specs,
        out_specs=[pl.BlockSpec((T, B_WIDTH), lambda i: (i, 0)),
                   pl.BlockSpec((1,) + st_shape, lambda i: (i, 0, 0, 0, 0))],
        out_shape=[jax.ShapeDtypeStruct((n_seq * T, B_WIDTH), F32),
                   jax.ShapeDtypeStruct((n_seq,) + st_shape, F32)],
        scratch_shapes=scr,
        compiler_params=pltpu.CompilerParams(vmem_limit_bytes=VMEM_LIMIT),
    )(*args)


def _delta_kernel(T, has_state, *refs):
    if has_state:
        (c_ref, ab_ref, conv_ref, arow_ref, dtrow_ref, on_ref, s0_ref, oc_ref, sf_ref,
         q_s, k_s, v_s, g_s, b_s, o_s, st_s) = refs
    else:
        (c_ref, ab_ref, conv_ref, arow_ref, dtrow_ref, on_ref, oc_ref, sf_ref,
         q_s, k_s, v_s, g_s, b_s, o_s, st_s) = refs
    n_chunks = T // CH
    n_blk = T // SEQ_BLK
    hc = C_HEADS * CH
    row = lax.broadcasted_iota(jnp.int32, (SEQ_BLK, 1), 0)
    for b in range(n_blk):
        r0 = b * SEQ_BLK
        rb = slice(r0, r0 + SEQ_BLK)
        x = c_ref[rb, 0:3 * C_WIDTH]
        before = c_ref[r0 - 1:r0, 0:3 * C_WIDTH] if b > 0 else jnp.zeros((1, 3 * C_WIDTH), F32)
        after = c_ref[r0 + SEQ_BLK:r0 + SEQ_BLK + 1, 0:3 * C_WIDTH] if b < n_blk - 1 else jnp.zeros((1, 3 * C_WIDTH), F32)
        prev = jnp.where(row == 0, before, pltpu.roll(x, 1, 0))
        nxt = jnp.where(row == SEQ_BLK - 1, after, pltpu.roll(x, SEQ_BLK - 1, 0))
        xc = _silu(conv_ref[0:1, :] * prev + conv_ref[1:2, :] * x + conv_ref[2:3, :] * nxt)
        for h in range(C_HEADS):
            sl = slice(h * C_DIM, (h + 1) * C_DIM)
            qh = xc[:, h * C_DIM:(h + 1) * C_DIM]
            kh = xc[:, C_WIDTH + h * C_DIM:C_WIDTH + (h + 1) * C_DIM]
            q_s[rb, sl] = qh * lax.rsqrt(jnp.sum(qh * qh, axis=-1, keepdims=True) + 1e-6) * (C_DIM ** -0.5)
            k_s[rb, sl] = kh * lax.rsqrt(jnp.sum(kh * kh, axis=-1, keepdims=True) + 1e-6)
        v_s[rb, :] = xc[:, 2 * C_WIDTH:]
        ab = ab_ref[rb, :]
        g_s[rb, :] = arow_ref[...] * jax.nn.softplus(ab + dtrow_ref[...])
        b_s[rb, :] = jax.nn.sigmoid(ab)
    if has_state:
        for d in range(2):
            st_s[d] = jnp.concatenate([s0_ref[0, d, h] for h in range(C_HEADS)], axis=0)
    else:
        st_s[...] = jnp.zeros_like(st_s)
    ei = lax.broadcasted_iota(jnp.int32, (hc, C_WIDTH), 0) // CH
    ej = lax.broadcasted_iota(jnp.int32, (hc, C_WIDTH), 1) // C_DIM
    own = ei == ej

    def expand(x):
        return jnp.where(own, jnp.concatenate([x] * C_HEADS, axis=1), 0.0)

    ri = lax.broadcasted_iota(jnp.int32, (hc, hc), 0)
    rj = lax.broadcasted_iota(jnp.int32, (hc, hc), 1)
    eye = jnp.where(ri == rj, 1.0, 0.0).astype(F32)
    for d in range(2):
        incl, strict = _head_masks(C_HEADS, d)
        tri = _chunk_tri(d)
        last = CH - 1 if d == 0 else 0
        j0 = d * C_HEADS

        def chunk_body(it, carry, d=d, incl=incl, strict=strict, tri=tri, last=last, j0=j0):
            c = it if d == 0 else n_chunks - 1 - it
            rows = pl.ds(pl.multiple_of(c * CH, CH), CH)
            gcum = _dot_hi(tri, g_s[rows, :])
            bet = b_s[rows, :]
            heads = range(C_HEADS)
            g_col = jnp.concatenate([gcum[:, j0 + h:j0 + h + 1] for h in heads], axis=0)
            b_col = jnp.concatenate([bet[:, 2 * C_HEADS + j0 + h:2 * C_HEADS + j0 + h + 1] for h in heads], axis=0)
            g_end = [gcum[last:last + 1, j0 + h:j0 + h + 1] for h in heads]
            g_last = jnp.concatenate([jnp.broadcast_to(g, (CH, 1)) for g in g_end], axis=0)
            e_last = jnp.concatenate([jnp.broadcast_to(jnp.exp(g), (C_DIM, 1)) for g in g_end], axis=0)
            g_row = jnp.sum(eye * g_col, axis=0, keepdims=True)
            decay = jnp.where(incl, jnp.exp(jnp.where(incl, g_col - g_row, 0.0)), 0.0)
            qs = _head_stack(q_s[rows, :], C_HEADS, C_DIM)
            ks = _head_stack(k_s[rows, :], C_HEADS, C_DIM)
            vs = _head_stack(v_s[rows, :], C_HEADS, C_DIM)
            kbeta = ks * b_col
            gm = _dot_nt(jnp.concatenate([kbeta, qs], axis=0), ks)
            attn = gm[hc:, :] * decay
            eg = jnp.exp(g_col)
            xm = _unit_lower_solve(-jnp.where(strict, gm[0:hc, :] * decay, 0.0),
                                   jnp.concatenate([vs * b_col, kbeta * eg], axis=1), hc)
            s_old = st_s[d]
            ws = _dot(jnp.concatenate([expand(xm[:, C_DIM:]), expand(qs * eg)], axis=0), s_old)
            v_new = xm[:, 0:C_DIM] - ws[0:hc, :]
            o = ws[hc:, :] + _dot(attn, v_new)
            st_s[d] = s_old * e_last + _dot_tn(expand(ks * jnp.exp(g_last - g_col)), v_new)
            oc = _head_unstack(o, C_HEADS)
            if d == 0:
                o_s[rows, :] = oc
            else:
                o_s[rows, :] = o_s[rows, :] + oc
            return carry

        lax.fori_loop(0, n_chunks, chunk_body, 0)
    for b in range(n_blk):
        rb = slice(b * SEQ_BLK, (b + 1) * SEQ_BLK)
        z = c_ref[rb, 3 * C_WIDTH:]
        for h in range(C_HEADS):
            sl = slice(h * C_DIM, (h + 1) * C_DIM)
            oh = o_s[rb, sl]
            on = oh * lax.rsqrt(jnp.mean(oh * oh, axis=-1, keepdims=True) + NORM_EPS) * on_ref[...]
            oc_ref[rb, sl] = on * _silu(z[:, sl])
    for d in range(2):
        sd = st_s[d]
        for h in range(C_HEADS):
            sf_ref[0, d, h] = sd[h * C_DIM:(h + 1) * C_DIM, :]


def _delta_call(cmat, abmat, row_off_blocks, n_seq, T, p, s0=None):
    has_state = s0 is not None
    st_shape = (2, C_HEADS, C_DIM, C_DIM)
    arow = jnp.zeros((1, LANE), F32).at[0, 0:2 * C_HEADS].set(-jnp.exp(p['c_A_log'].reshape(-1)))
    dtrow = jnp.zeros((1, LANE), F32).at[0, 0:2 * C_HEADS].set(p['c_dt_bias'].reshape(-1))
    in_specs = [pl.BlockSpec((T, 4 * C_WIDTH), lambda i: (row_off_blocks + i, 0)),
                pl.BlockSpec((T, LANE), lambda i: (row_off_blocks + i, 0)),
                _full((3, 3 * C_WIDTH)), _full((1, LANE)), _full((1, LANE)), _full((1, C_DIM))]
    args = [cmat, abmat, p['c_conv'], arow, dtrow, p['c_onorm'].reshape(1, -1)]
    if has_state:
        in_specs.append(pl.BlockSpec((1,) + st_shape, lambda i: (i, 0, 0, 0, 0)))
        args.append(s0)
    scr = [pltpu.VMEM((T, C_WIDTH), F32)] * 3 + [pltpu.VMEM((T, LANE), F32)] * 2 + \
          [pltpu.VMEM((T, C_WIDTH), F32), pltpu.VMEM((2, C_HEADS * C_DIM, C_DIM), F32)]
    return pl.pallas_call(
        functools.partial(_delta_kernel, T, has_state),
        grid=(n_seq,),
        in_specs=in_specs,
        out_specs=[pl.BlockSpec((T, C_WIDTH), lambda i: (i, 0)),
                   pl.BlockSpec((1,) + st_shape, lambda i: (i, 0, 0, 0, 0))],
        out_shape=[jax.ShapeDtypeStruct((n_seq * T, C_WIDTH), F32),
                   jax.ShapeDtypeStruct((n_seq,) + st_shape, F32)],
        scratch_shapes=scr,
        compiler_params=pltpu.CompilerParams(vmem_limit_bytes=VMEM_LIMIT),
    )(*args)


def _first_max(x, iota, size):
    m = jnp.max(x, axis=0, keepdims=True)
    idx = jnp.min(jnp.where(x == m, iota, size), axis=0, keepdims=True)
    return m, idx


def _route_kernel(lg_ref, bias_ref, idx_ref, wts_ref):
    n = lg_ref.shape[1]
    scores = jax.nn.sigmoid(lg_ref[...])
    biased = scores + bias_ref[...]
    e_iota = lax.broadcasted_iota(jnp.int32, (N_EXPERTS, n), 0)
    g_iota = lax.broadcasted_iota(jnp.int32, (PER_GROUP, n), 0)
    gs = []
    for g in range(N_GROUPS):
        xg = biased[g * PER_GROUP:(g + 1) * PER_GROUP, :]
        m1, i1 = _first_max(xg, g_iota, PER_GROUP)
        m2 = jnp.max(jnp.where(g_iota == i1, NEG, xg), axis=0, keepdims=True)
        gs.append(m1 + m2)
    gscore = jnp.concatenate(gs, axis=0)
    gi = lax.broadcasted_iota(jnp.int32, (N_GROUPS, n), 0)
    gsel = jnp.zeros((N_GROUPS, n), F32)
    for _ in range(TOPK_GROUPS):
        _, ig = _first_max(gscore, gi, N_GROUPS)
        hit = gi == ig
        gsel = jnp.where(hit, 1.0, gsel)
        gscore = jnp.where(hit, NEG, gscore)
    masked = jnp.concatenate(
        [jnp.where(gsel[g:g + 1, :] > 0.0, biased[g * PER_GROUP:(g + 1) * PER_GROUP, :], NEG)
         for g in range(N_GROUPS)], axis=0)
    ids, ws = [], []
    for _ in range(TOP_K):
        _, ie = _first_max(masked, e_iota, N_EXPERTS)
        hit = e_iota == ie
        ids.append(ie)
        ws.append(jnp.sum(jnp.where(hit, scores, 0.0), axis=0, keepdims=True))
        masked = jnp.where(hit, NEG, masked)
    wsum = ws[0]
    for w in ws[1:]:
        wsum = wsum + w
    inv = ROUTED_SCALE / (wsum + 1e-20)
    idx_ref[...] = jnp.concatenate(ids + [jnp.zeros((8 - TOP_K, n), jnp.int32)], axis=0)
    wts_ref[...] = jnp.concatenate([w * inv for w in ws] + [jnp.zeros((8 - TOP_K, n), F32)], axis=0)


def _route_call(logits_t, bias):
    m = logits_t.shape[1]
    return pl.pallas_call(
        _route_kernel,
        grid=(m // TM,),
        in_specs=[pl.BlockSpec((N_EXPERTS, TM), lambda i: (0, i)), _full((N_EXPERTS, 1))],
        out_specs=[pl.BlockSpec((8, TM), lambda i: (0, i))] * 2,
        out_shape=[jax.ShapeDtypeStruct((8, m), jnp.int32), jax.ShapeDtypeStruct((8, m), F32)],
    )(logits_t, bias.reshape(N_EXPERTS, 1))


def _dispatch_tables(idx, m):
    n_asg = m * TOP_K
    nb = -(-n_asg // BM) + N_EXPERTS
    n_pad = nb * BM - n_asg
    flat_e = idx[0:TOP_K, :].reshape(-1)
    e_iota = jnp.arange(N_EXPERTS, dtype=jnp.int32)
    counts = jnp.sum((flat_e[:, None] == e_iota[None, :]).astype(jnp.int32), axis=0)
    padded = (counts + BM - 1) // BM * BM
    pad_end = jnp.cumsum(padded)
    pad_cum = jnp.cumsum(padded - counts)
    pad_e = jnp.sum((pad_cum[None, :] <= jnp.arange(n_pad, dtype=jnp.int32)[:, None]).astype(jnp.int32), axis=1)
    keys = jnp.concatenate([flat_e * 2, pad_e * 2 + 1])
    vals = jnp.concatenate([jnp.arange(n_asg, dtype=jnp.int32), jnp.full((n_pad,), -1, jnp.int32)])
    _, slot_asg = lax.sort((keys, vals), num_keys=1, is_stable=True)
    valid = slot_asg >= 0
    spare = TOP_K * m + jnp.arange(nb * BM, dtype=jnp.int32) % BM
    slot_dst = jnp.where(valid, slot_asg, spare)
    slot_tok = jnp.where(valid, slot_asg % m, 0)
    blk0 = jnp.arange(nb, dtype=jnp.int32) * BM
    block_expert = jnp.minimum(jnp.sum((pad_end[None, :] <= blk0[:, None]).astype(jnp.int32), axis=1), N_EXPERTS - 1)
    n_used = (pad_end[-1] // BM).astype(jnp.int32).reshape(1)
    return slot_tok.reshape(nb, 1, BM), slot_dst.reshape(nb, 1, BM), block_expert.astype(jnp.int32), n_used


def _expert_kernel(be_ref, nu_ref, st_ref, stn_ref, sd_ref, tok_hbm, wg_ref, wu_ref, wd_ref, y_hbm,
                   xbuf, ybuf, sem_in, sem_out):
    i = pl.program_id(0)
    n_used = nu_ref[0]
    slot = i % 2

    def gather(tab_ref, b):
        def body(r, c):
            pltpu.make_async_copy(tok_hbm.at[pl.ds(tab_ref[0, 0, r], 1), :], xbuf.at[b, pl.ds(r, 1), :],
                                  sem_in.at[b]).start()
            return c
        lax.fori_loop(0, BM, body, 0, unroll=DMA_UNROLL)

    def wait_gather(b):
        pltpu.make_async_copy(tok_hbm.at[pl.ds(0, BM), :], xbuf.at[b], sem_in.at[b]).wait()

    def wait_scatter(b):
        pltpu.make_async_copy(ybuf.at[b], y_hbm.at[pl.ds(0, BM), :], sem_out.at[b]).wait()

    @pl.when(i == 0)
    def _():
        ybuf[0] = jnp.zeros((BM, D_MODEL), F32)
        spare = pltpu.make_async_copy(ybuf.at[0], y_hbm.at[pl.ds(y_hbm.shape[0] - BM, BM), :], sem_out.at[0])
        spare.start()
        spare.wait()
        gather(st_ref, 0)

    @pl.when(i + 1 < n_used)
    def _():
        gather(stn_ref, 1 - slot)

    @pl.when(i < n_used)
    def _():
        wait_gather(slot)

        @pl.when(i >= 2)
        def _():
            wait_scatter(slot)

        x = xbuf[slot].astype(BF16)
        g = jnp.dot(x, wg_ref[0, 0].astype(BF16), preferred_element_type=F32)
        u = jnp.dot(x, wu_ref[0, 0].astype(BF16), preferred_element_type=F32)
        ybuf[slot] = jnp.dot((_silu(g) * u).astype(BF16), wd_ref[0, 0].astype(BF16), preferred_element_type=F32)

        def body(r, c):
            pltpu.make_async_copy(ybuf.at[slot, pl.ds(r, 1), :], y_hbm.at[pl.ds(sd_ref[0, 0, r], 1), :],
                                  sem_out.at[slot]).start()
            return c
        lax.fori_loop(0, BM, body, 0, unroll=DMA_UNROLL)

        @pl.when(i == n_used - 1)
        def _():
            wait_scatter(slot)

            @pl.when(i >= 1)
            def _():
                wait_scatter(1 - slot)


def _expert_call(tok, tables, layer, wg, wu, wd):
    m = tok.shape[0]
    slot_tok, slot_dst, block_expert, n_used = tables
    nb = slot_tok.shape[0]
    smem_blk = lambda f: pl.BlockSpec((1, 1, BM), lambda i, be, nu: (f(i), 0, 0), memory_space=pltpu.SMEM)
    grid_spec = pltpu.PrefetchScalarGridSpec(
        num_scalar_prefetch=2,
        grid=(nb,),
        in_specs=[smem_blk(lambda i: i), smem_blk(lambda i: jnp.minimum(i + 1, nb - 1)), smem_blk(lambda i: i),
                  pl.BlockSpec(memory_space=pl.ANY),
                  pl.BlockSpec((1, 1, D_MODEL, EXPERT_FF), lambda i, be, nu: (layer, be[i], 0, 0)),
                  pl.BlockSpec((1, 1, D_MODEL, EXPERT_FF), lambda i, be, nu: (layer, be[i], 0, 0)),
                  pl.BlockSpec((1, 1, EXPERT_FF, D_MODEL), lambda i, be, nu: (layer, be[i], 0, 0))],
        out_specs=pl.BlockSpec(memory_space=pl.ANY),
        scratch_shapes=[pltpu.VMEM((2, BM, D_MODEL), F32), pltpu.VMEM((2, BM, D_MODEL), F32),
                        pltpu.SemaphoreType.DMA((2,)), pltpu.SemaphoreType.DMA((2,))])
    return pl.pallas_call(
        _expert_kernel, grid_spec=grid_spec,
        out_shape=jax.ShapeDtypeStruct((TOP_K * m + BM, D_MODEL), F32),
        compiler_params=pltpu.CompilerParams(vmem_limit_bytes=VMEM_LIMIT),
    )(block_expert, n_used, slot_tok, slot_tok, slot_dst, tok, wg, wu, wd)


def _combine_kernel(x_ref, sh_ref, w_ref, mod_ref, *rest):
    y_refs, o_ref = rest[:TOP_K], rest[TOP_K]
    w = w_ref[...]
    acc = sh_ref[...]
    for k, y_ref in enumerate(y_refs):
        acc = acc + w[:, k:k + 1] * y_ref[...]
    o_ref[...] = x_ref[...] + mod_ref[0, 0, 5:6, :] * acc


def _combine_call(x, sh, wts_rows, mod, layer, yrows):
    nt = M_TOK // TM
    row = pl.BlockSpec((TM, D_MODEL), lambda i: (i, 0))
    ysp = [pl.BlockSpec((TM, D_MODEL), (lambda i, k=k: (k * nt + i, 0))) for k in range(TOP_K)]
    return pl.pallas_call(
        _combine_kernel,
        grid=(nt,),
        in_specs=[row, row, pl.BlockSpec((TM, 8), lambda i: (i, 0)),
                  pl.BlockSpec((1, 1, 6, D_MODEL), lambda i: (layer, _mod_group(i), 0, 0))] + ysp,
        out_specs=row,
        out_shape=jax.ShapeDtypeStruct((M_TOK, D_MODEL), F32),
        compiler_params=pltpu.CompilerParams(vmem_limit_bytes=VMEM_LIMIT),
    )(x, sh, wts_rows, mod, *([yrows] * TOP_K))


def kernel(x_prompt, x_sample, cache_attn_k, cache_attn_v, state_rwkv, state_delta, cache_diff_k,
           cache_diff_v, c, c_ctx, mod_w, mod_b, norm1_g, norm2_g, ev_w_in, ev_w_out, a_qn, a_kn, b_mu,
           b_w0, b_w2, b_a0, b_a2, b_g2, b_kk, b_ka, b_rk, b_lnx_w, b_lnx_b, od_w_in, od_w_out, c_conv,
           c_A_log, c_dt_bias, c_onorm, d_qn, d_kn, d_lambda, d_subln, router_w, router_bias, exp_w_gate,
           exp_w_up, exp_w_down, sh_w_gate, sh_w_up, sh_w_down):
    x = jnp.concatenate([x_prompt.reshape(N_CTX, D_MODEL), x_sample.reshape(N_LAT, D_MODEL)], axis=0)
    cond = jnp.concatenate([c_ctx[None], c, jnp.zeros((8 - N_MOD, D_MODEL), F32)], axis=0)
    mod = _adaln_call(cond, mod_w, mod_b)[:, 0:N_MOD].reshape(DEPTH, N_MOD, 6, D_MODEL)
    lat_blk = N_CTX // DEC_SEQ
    new_ak, new_av, new_sr, new_sd, new_dk, new_dv = [], [], [], [], [], []
    for l in range(DEPTH):
        j = l // 2
        if l % 2 == 0:
            w = ev_w_in[j]
            q, k, v = w[:, 0:512], w[:, 512:640], w[:, 640:768]
            w_a = jnp.concatenate([q[:, 0:256], k[:, 0:64], v[:, 0:64], q[:, 256:], k[:, 64:], v[:, 64:]], axis=1)
            amat, bmat = _proj_in_call(x, mod, l, norm1_g[l], [w_a.astype(BF16), w[:, 768:].astype(BF16)])
            gain = jnp.concatenate([jnp.tile(a_qn[j], A_GROUP), a_kn[j], jnp.ones((HD,), F32)]).reshape(1, GW)
            oa_c, nk, nv = _attn_a_call(amat, 0, BATCH, SEQ, gain)
            oa_l, = _attn_a_call(amat, lat_blk, DEC_BATCH, DEC_SEQ, gain, (cache_attn_k, cache_attn_v, j))
            p = {'b_mu': b_mu[j], 'b_w0': b_w0[j], 'b_w2': b_w2[j], 'b_a0': b_a0[j], 'b_a2': b_a2[j],
                 'b_g2': b_g2[j], 'b_kk': b_kk[j], 'b_ka': b_ka[j], 'b_rk': b_rk[j], 'b_lnx_w': b_lnx_w[j],
                 'b_lnx_b': b_lnx_b[j]}
            ob_c, sr = _rwkv_call(bmat, 0, BATCH, SEQ, p)
            ob_l, _ = _rwkv_call(bmat, lat_blk, DEC_BATCH, DEC_SEQ, p, state_rwkv[:, j])
            o1 = jnp.concatenate([oa_c, oa_l], axis=0)
            o2 = jnp.concatenate([ob_c, ob_l], axis=0)
            w_out = ev_w_out[j]
            new_ak.append(nk)
            new_av.append(nv)
            new_sr.append(sr)
        else:
            lam_init = 0.8 - 0.6 * math.exp(-0.3 * l)
            w = od_w_in[j]
            s0 = 4 * C_WIDTH + 4 * C_HEADS
            dq, dk, dv = w[:, s0:s0 + 512], w[:, s0 + 512:s0 + 1024], w[:, s0 + 1024:]
            w_ab = jnp.pad(w[:, 4 * C_WIDTH:s0], ((0, 0), (0, LANE - 4 * C_HEADS)))
            w_d = jnp.concatenate([jnp.concatenate([dq[:, 128 * h:128 * (h + 1)], dk[:, 128 * h:128 * (h + 1)],
                                                    dv[:, 128 * h:128 * (h + 1)]], axis=1) for h in range(D_HEADS)],
                                  axis=1)
            cmat, abmat, dmat = _proj_in_call(x, mod, l, norm1_g[l],
                                              [w[:, 0:4 * C_WIDTH].astype(BF16), w_ab.astype(BF16), w_d.astype(BF16)])
            p = {'c_conv': c_conv[j], 'c_A_log': c_A_log[j], 'c_dt_bias': c_dt_bias[j], 'c_onorm': c_onorm[j]}
            oc_c, sd_ = _delta_call(cmat, abmat, 0, BATCH, SEQ, p)
            oc_l, _ = _delta_call(cmat, abmat, lat_blk, DEC_BATCH, DEC_SEQ, p, state_delta[:, j])
            gain = jnp.concatenate([jnp.tile(d_qn[j], 2), jnp.tile(d_kn[j], 2), jnp.ones((D_VDIM,), F32)]).reshape(1, GW)
            od_c, ndk, ndv = _attn_d_call(dmat, 0, BATCH, SEQ, gain, d_lambda[j], d_subln[j], lam_init)
            od_l, = _attn_d_call(dmat, lat_blk, DEC_BATCH, DEC_SEQ, gain, d_lambda[j], d_subln[j], lam_init,
                                 (cache_diff_k, cache_diff_v, j))
            o1 = jnp.concatenate([oc_c, oc_l], axis=0)
            o2 = jnp.concatenate([od_c, od_l], axis=0)
            w_out = od_w_out[j]
            new_dk.append(ndk)
            new_dv.append(ndv)
            new_sd.append(sd_)
        x, tok, sh, logits_t = _proj_out_call(
            o1, o2, x, mod, l, norm2_g[l], w_out.astype(BF16), router_w[l].T.astype(BF16),
            sh_w_gate[l].astype(BF16), sh_w_up[l].astype(BF16), sh_w_down[l].astype(BF16))
        idx, wts = _route_call(logits_t, router_bias[l])
        tables = _dispatch_tables(idx, M_TOK)
        yrows = _expert_call(tok, tables, l, exp_w_gate, exp_w_up, exp_w_down)
        x = _combine_call(x, sh, wts.T, mod, l, yrows)
    return (x[0:N_CTX].reshape(BATCH, SEQ, D_MODEL), x[N_CTX:].reshape(DEC_BATCH, DEC_SEQ, D_MODEL),
            jnp.stack(new_ak, axis=1), jnp.stack(new_av, axis=1), jnp.stack(new_sr, axis=1),
            jnp.stack(new_sd, axis=1), jnp.stack(new_dk, axis=1), jnp.stack(new_dv, axis=1))
```

```python
import functools
import math
import jax
import jax.numpy as jnp
from jax import lax
from jax.experimental import pallas as pl
from jax.experimental.pallas import tpu as pltpu

D_MODEL = 1024
BATCH = 32
SEQ = 256
DEPTH = 4
DEC_BATCH = 2
DEC_SEQ = 1024
PAST_LEN = 512
GRID_W = 64
NORM_EPS = 1e-6
ROPE_THETA = 10000.0
A_HEADS = 8
A_KV_HEADS = 2
A_GROUP = A_HEADS // A_KV_HEADS
B_HEADS = 8
B_DIM = 64
B_WIDTH = B_HEADS * B_DIM
DECAY_LORA = 64
ICLR_LORA = 64
GATE_LORA = 128
B_COLS = 3 * B_WIDTH + DECAY_LORA + ICLR_LORA + GATE_LORA
LNX_EPS = 64e-5
C_HEADS = 4
C_DIM = 128
C_WIDTH = C_HEADS * C_DIM
D_HEADS = 4
D_VDIM = 128
N_EXPERTS = 64
TOP_K = 6
N_GROUPS = 8
TOPK_GROUPS = 4
PER_GROUP = N_EXPERTS // N_GROUPS
EXPERT_FF = 256
SHARED_FF = 256
ROUTED_SCALE = 1.0

HD = 64
GW = 384
CH = 64
LANE = 128
BM = 128
DMA_UNROLL = 8
TM = 512
SEQ_BLK = 256
RWKV_SOLVE_BLK = 256
N_CTX = BATCH * SEQ
N_LAT = DEC_BATCH * DEC_SEQ
M_TOK = N_CTX + N_LAT
N_MOD = 1 + DEC_BATCH
VMEM_LIMIT = 56 * 1024 * 1024

F32 = jnp.float32
BF16 = jnp.bfloat16
HI = lax.Precision.HIGHEST
NEG = -jnp.inf


def _dot(a, b):
    return jnp.dot(a.astype(BF16), b.astype(BF16), preferred_element_type=F32)


def _dot_nt(a, b):
    return lax.dot_general(a.astype(BF16), b.astype(BF16), (((1,), (1,)), ((), ())), preferred_element_type=F32)


def _dot_tn(a, b):
    return lax.dot_general(a.astype(BF16), b.astype(BF16), (((0,), (0,)), ((), ())), preferred_element_type=F32)


def _dot_hi(a, b):
    return jnp.dot(a, b, preferred_element_type=F32, precision=HI)


def _silu(x):
    return x * jax.nn.sigmoid(x)


def _mod_group(i):
    n_ctx_tiles = N_CTX // TM
    return jnp.where(i < n_ctx_tiles, 0, 1 + (i - n_ctx_tiles) // (DEC_SEQ // TM))


def _full(shape):
    return pl.BlockSpec(shape, lambda *_: (0,) * len(shape))


def _adaln_kernel(c_ref, w_ref, b_ref, o_ref):
    o_ref[0] = _dot(_silu(c_ref[...]), w_ref[0]) + b_ref[0]


def _adaln_call(cond, mod_w, mod_b):
    n = 6
    return pl.pallas_call(
        _adaln_kernel,
        grid=(DEPTH, n),
        in_specs=[_full((8, D_MODEL)),
                  pl.BlockSpec((1, D_MODEL, D_MODEL), lambda l, j: (l, 0, j)),
                  pl.BlockSpec((1, 1, D_MODEL), lambda l, j: (l, 0, j))],
        out_specs=pl.BlockSpec((1, 8, D_MODEL), lambda l, j: (l, 0, j)),
        out_shape=jax.ShapeDtypeStruct((DEPTH, 8, n * D_MODEL), F32),
    )(cond, mod_w, mod_b.reshape(DEPTH, 1, n * D_MODEL))


def _proj_in_kernel(n_out, x_ref, mod_ref, g_ref, *rest):
    x = x_ref[...]
    y = x * lax.rsqrt(jnp.mean(x * x, axis=-1, keepdims=True) + NORM_EPS) * g_ref[...]
    h = (y * (1.0 + mod_ref[0, 0, 1:2, :]) + mod_ref[0, 0, 0:1, :]).astype(BF16)
    for w_ref, o_ref in zip(rest[:n_out], rest[n_out:]):
        o_ref[...] = jnp.dot(h, w_ref[...], preferred_element_type=F32)


def _proj_in_call(x, mod, layer, g, weights):
    n_out = len(weights)
    return pl.pallas_call(
        functools.partial(_proj_in_kernel, n_out),
        grid=(M_TOK // TM,),
        in_specs=[pl.BlockSpec((TM, D_MODEL), lambda i: (i, 0)),
                  pl.BlockSpec((1, 1, 6, D_MODEL), lambda i: (layer, _mod_group(i), 0, 0)),
                  _full((1, D_MODEL))] + [_full(w.shape) for w in weights],
        out_specs=[pl.BlockSpec((TM, w.shape[1]), lambda i: (i, 0)) for w in weights],
        out_shape=[jax.ShapeDtypeStruct((M_TOK, w.shape[1]), F32) for w in weights],
        compiler_params=pltpu.CompilerParams(vmem_limit_bytes=VMEM_LIMIT),
    )(x, mod, g.reshape(1, D_MODEL), *weights)


def _proj_out_kernel(o1_ref, o2_ref, x_ref, mod_ref, g_ref, wo_ref, rwt_ref, sg_ref, su_ref, sd_ref,
                     xn_ref, tok_ref, sh_ref, lg_ref):
    o = jnp.concatenate([o1_ref[...], o2_ref[...]], axis=1).astype(BF16)
    xn = x_ref[...] + mod_ref[0, 0, 2:3, :] * jnp.dot(o, wo_ref[...], preferred_element_type=F32)
    xn_ref[...] = xn
    y = xn * lax.rsqrt(jnp.mean(xn * xn, axis=-1, keepdims=True) + NORM_EPS) * g_ref[...]
    h2 = y * (1.0 + mod_ref[0, 0, 4:5, :]) + mod_ref[0, 0, 3:4, :]
    tok_ref[...] = h2
    hb = h2.astype(BF16)
    lg_ref[...] = _dot_nt(rwt_ref[...], hb)
    hid = _silu(jnp.dot(hb, sg_ref[...], preferred_element_type=F32)) * jnp.dot(hb, su_ref[...], preferred_element_type=F32)
    sh_ref[...] = jnp.dot(hid.astype(BF16), sd_ref[...], preferred_element_type=F32)


def _proj_out_call(o1, o2, x, mod, layer, g2, w_out, rw_t, sg, su, sd):
    row = lambda n: pl.BlockSpec((TM, n), lambda i: (i, 0))
    return pl.pallas_call(
        _proj_out_kernel,
        grid=(M_TOK // TM,),
        in_specs=[row(o1.shape[1]), row(o2.shape[1]), row(D_MODEL),
                  pl.BlockSpec((1, 1, 6, D_MODEL), lambda i: (layer, _mod_group(i), 0, 0)),
                  _full((1, D_MODEL)), _full(w_out.shape), _full(rw_t.shape), _full(sg.shape), _full(su.shape),
                  _full(sd.shape)],
        out_specs=[row(D_MODEL), row(D_MODEL), row(D_MODEL), pl.BlockSpec((N_EXPERTS, TM), lambda i: (0, i))],
        out_shape=[jax.ShapeDtypeStruct((M_TOK, D_MODEL), F32)] * 3 + [jax.ShapeDtypeStruct((N_EXPERTS, M_TOK), F32)],
        compiler_params=pltpu.CompilerParams(vmem_limit_bytes=VMEM_LIMIT),
    )(o1, o2, x, mod, g2.reshape(1, D_MODEL), w_out, rw_t, sg, su, sd)


def _rope_lane_tables(n_tok):
    rows = n_tok // GRID_W
    row = jnp.repeat(jnp.arange(rows, dtype=F32), GRID_W)
    col = jnp.tile(jnp.arange(GRID_W, dtype=F32), rows)
    n_freq = HD // 4
    inv = ROPE_THETA ** (-jnp.arange(n_freq, dtype=F32) / n_freq)
    ang = jnp.concatenate([row[:, None] * inv, col[:, None] * inv], axis=-1)
    cos = jnp.repeat(jnp.cos(ang), 2, axis=-1)
    sin = jnp.repeat(jnp.sin(ang), 2, axis=-1) * jnp.tile(jnp.array([-1.0, 1.0], F32), HD // 2)
    return cos, sin


def _seg_matrix(width, seg, value):
    li = lax.broadcasted_iota(jnp.int32, (width, width), 0) // seg
    lj = lax.broadcasted_iota(jnp.int32, (width, width), 1) // seg
    return jnp.where(li == lj, value, 0.0).astype(F32)


def _norm_rope(x, gain, n_norm, cos, sin):
    lane = lax.broadcasted_iota(jnp.int32, x.shape, 1)
    ms = _dot_hi(x * x, _seg_matrix(x.shape[1], HD, 1.0 / HD))
    xn = jnp.where(lane < n_norm, x * lax.rsqrt(ms + NORM_EPS) * gain, x)
    if cos is None:
        return xn, xn
    w = x.shape[1]
    swapped = jnp.where(lane % 2 == 0, pltpu.roll(xn, w - 1, 1), pltpu.roll(xn, 1, 1))
    return xn, xn * cos + swapped * sin


def _softmax_pv(s, v):
    m = jnp.max(s, axis=-1, keepdims=True)
    p = jnp.exp(s - m)
    l = jnp.sum(p, axis=-1, keepdims=True)
    return _dot(p, v) / l


def _attn_a_kernel(T, P, TQ, *refs):
    if P:
        x_ref, gain_ref, cos_ref, sin_ref, ck_ref, cv_ref, o_ref, xr_s = refs
        xn, xr = _norm_rope(x_ref[...], gain_ref[...], 5 * HD, cos_ref[...], sin_ref[...])
    else:
        x_ref, gain_ref, o_ref, nk_ref, nv_ref, xr_s = refs
        xn, xr = _norm_rope(x_ref[...], gain_ref[...], 5 * HD, None, None)
        nk_ref[0, 0] = xn[:, 4 * HD:5 * HD]
        nv_ref[0, 0] = xn[:, 5 * HD:6 * HD]
    xr_s[...] = xr
    k_new = xr[:, 4 * HD:5 * HD]
    v_new = xr[:, 5 * HD:6 * HD]
    if P:
        k_all = jnp.concatenate([ck_ref[0, 0, 0], k_new], axis=0).astype(BF16)
        v_all = jnp.concatenate([cv_ref[0, 0, 0], v_new], axis=0).astype(BF16)
    else:
        k_all, v_all = k_new.astype(BF16), v_new.astype(BF16)

    def q_block(qb, carry):
        rows = pl.ds(pl.multiple_of(qb * TQ, TQ), TQ)
        qx = xr_s[rows, 0:A_GROUP * HD]
        qs = jnp.concatenate([qx[:, i * HD:(i + 1) * HD] for i in range(A_GROUP)], axis=0)
        o = _softmax_pv(_dot_nt(qs, k_all) * (HD ** -0.5), v_all)
        o_ref[rows, :] = jnp.concatenate([o[i * TQ:(i + 1) * TQ, :] for i in range(A_GROUP)], axis=1)
        return carry

    lax.fori_loop(0, T // TQ, q_block, 0)


def _attn_a_call(amat, row_off_blocks, n_seq, T, gain, cache=None):
    P = 0 if cache is None else cache[0].shape[3]
    TQ = T if P == 0 else 128
    in_specs = [pl.BlockSpec((T, GW), lambda i, g: (row_off_blocks + i, g)), _full((1, GW))]
    args = [amat, gain]
    out_specs = [pl.BlockSpec((T, A_GROUP * HD), lambda i, g: (i, g))]
    out_shape = [jax.ShapeDtypeStruct((n_seq * T, A_HEADS * HD), F32)]
    if P:
        ck, cv, j = cache
        cos, sin = _rope_lane_tables(T)
        cos_g = jnp.concatenate([cos] * 5 + [jnp.ones((T, HD), F32)], axis=1)
        sin_g = jnp.concatenate([sin] * 5 + [jnp.zeros((T, HD), F32)], axis=1)
        in_specs += [_full((T, GW)), _full((T, GW)),
                     pl.BlockSpec((1, 1, 1, P, HD), lambda i, g: (i, j, g, 0, 0)),
                     pl.BlockSpec((1, 1, 1, P, HD), lambda i, g: (i, j, g, 0, 0))]
        args += [cos_g, sin_g, ck, cv]
    else:
        out_specs += [pl.BlockSpec((1, 1, T, HD), lambda i, g: (i, g, 0, 0))] * 2
        out_shape += [jax.ShapeDtypeStruct((n_seq, A_KV_HEADS, T, HD), F32)] * 2
    return pl.pallas_call(
        functools.partial(_attn_a_kernel, T, P, TQ),
        grid=(n_seq, A_KV_HEADS),
        in_specs=in_specs, out_specs=out_specs, out_shape=out_shape,
        scratch_shapes=[pltpu.VMEM((T, GW), F32)],
        compiler_params=pltpu.CompilerParams(vmem_limit_bytes=VMEM_LIMIT),
    )(*args)


def _attn_d_kernel(T, P, TQ, lam_init, *refs):
    if P:
        x_ref, gain_ref, lam_ref, sub_ref, cos_ref, sin_ref, ck_ref, cv_ref, o_ref, xr_s = refs
        xn, xr = _norm_rope(x_ref[...], gain_ref[...], 4 * HD, cos_ref[...], sin_ref[...])
    else:
        x_ref, gain_ref, lam_ref, sub_ref, o_ref, nk_ref, nv_ref, xr_s = refs
        xn, xr = _norm_rope(x_ref[...], gain_ref[...], 4 * HD, None, None)
        nk_ref[0, 0, 0] = xn[:, 2 * HD:3 * HD]
        nk_ref[0, 0, 1] = xn[:, 3 * HD:4 * HD]
        nv_ref[0, 0] = xn[:, 4 * HD:]
    xr_s[...] = xr
    v_new = xr[:, 4 * HD:]
    ks = []
    for m in range(2):
        k_new = xr[:, (2 + m) * HD:(3 + m) * HD]
        if P:
            ks.append(jnp.concatenate([ck_ref[0, 0, 0, m], k_new], axis=0).astype(BF16))
        else:
            ks.append(k_new.astype(BF16))
    v_all = (jnp.concatenate([cv_ref[0, 0, 0], v_new], axis=0) if P else v_new).astype(BF16)
    lm = lam_ref[...]
    lam = (jnp.exp(jnp.sum(lm[0:1, :] * lm[1:2, :], axis=-1, keepdims=True))
           - jnp.exp(jnp.sum(lm[2:3, :] * lm[3:4, :], axis=-1, keepdims=True)) + lam_init)

    def q_block(qb, carry):
        rows = pl.ds(pl.multiple_of(qb * TQ, TQ), TQ)
        qx = xr_s[rows, 0:2 * HD]
        o1 = _softmax_pv(_dot_nt(qx[:, 0:HD], ks[0]) * (HD ** -0.5), v_all)
        o2 = _softmax_pv(_dot_nt(qx[:, HD:], ks[1]) * (HD ** -0.5), v_all)
        od = o1 - lam * o2
        od = od * lax.rsqrt(jnp.mean(od * od, axis=-1, keepdims=True) + NORM_EPS) * sub_ref[...]
        o_ref[rows, :] = od * (1.0 - lam_init)
        return carry

    lax.fori_loop(0, T // TQ, q_block, 0)


def _attn_d_call(dmat, row_off_blocks, n_seq, T, gain, lam, subln, lam_init, cache=None):
    P = 0 if cache is None else cache[0].shape[4]
    TQ = T if P == 0 else 256
    in_specs = [pl.BlockSpec((T, GW), lambda i, h: (row_off_blocks + i, h)),
                _full((1, GW)), _full((4, HD)), _full((1, D_VDIM))]
    args = [dmat, gain, lam, subln.reshape(1, -1)]
    out_specs = [pl.BlockSpec((T, D_VDIM), lambda i, h: (i, h))]
    out_shape = [jax.ShapeDtypeStruct((n_seq * T, D_HEADS * D_VDIM), F32)]
    if P:
        ck, cv, j = cache
        cos, sin = _rope_lane_tables(T)
        cos_g = jnp.concatenate([cos] * 4 + [jnp.ones((T, D_VDIM), F32)], axis=1)
        sin_g = jnp.concatenate([sin] * 4 + [jnp.zeros((T, D_VDIM), F32)], axis=1)
        in_specs += [_full((T, GW)), _full((T, GW)),
                     pl.BlockSpec((1, 1, 1, 2, P, HD), lambda i, h: (i, j, h, 0, 0, 0)),
                     pl.BlockSpec((1, 1, 1, P, D_VDIM), lambda i, h: (i, j, h, 0, 0))]
        args += [cos_g, sin_g, ck, cv]
    else:
        out_specs += [pl.BlockSpec((1, 1, 2, T, HD), lambda i, h: (i, h, 0, 0, 0)),
                      pl.BlockSpec((1, 1, T, D_VDIM), lambda i, h: (i, h, 0, 0))]
        out_shape += [jax.ShapeDtypeStruct((n_seq, D_HEADS, 2, T, HD), F32),
                      jax.ShapeDtypeStruct((n_seq, D_HEADS, T, D_VDIM), F32)]
    return pl.pallas_call(
        functools.partial(_attn_d_kernel, T, P, TQ, lam_init),
        grid=(n_seq, D_HEADS),
        in_specs=in_specs, out_specs=out_specs, out_shape=out_shape,
        scratch_shapes=[pltpu.VMEM((T, GW), F32)],
        compiler_params=pltpu.CompilerParams(vmem_limit_bytes=VMEM_LIMIT),
    )(*args)


def _dot_3pass(a, b):
    a_hi = a.astype(BF16)
    a_lo = (a - a_hi.astype(F32)).astype(BF16)
    b_hi = b.astype(BF16)
    b_lo = (b - b_hi.astype(F32)).astype(BF16)
    d = lambda x, y: jnp.dot(x, y, preferred_element_type=F32)
    return d(a_hi, b_hi) + (d(a_hi, b_lo) + d(a_lo, b_hi))


def _unit_lower_solve(a_bd, x, blk):
    n_stage = CH.bit_length() - 1
    outs = []
    for g in range(a_bd.shape[0] // blk):
        sl = slice(g * blk, (g + 1) * blk)
        am, xm = a_bd[sl, sl], x[sl, :]
        for s in range(n_stage):
            xm = xm + _dot_3pass(am, xm)
            if s < n_stage - 1:
                am = _dot_3pass(am, am)
        outs.append(xm)
    return jnp.concatenate(outs, axis=0)


def _head_stack(x, n_heads, width):
    return jnp.concatenate([x[:, h * width:(h + 1) * width] for h in range(n_heads)], axis=0)


def _head_unstack(x, n_heads):
    return jnp.concatenate([x[h * CH:(h + 1) * CH, :] for h in range(n_heads)], axis=1)


def _head_masks(n_heads, d):
    hc = n_heads * CH
    ri = lax.broadcasted_iota(jnp.int32, (hc, hc), 0)
    rj = lax.broadcasted_iota(jnp.int32, (hc, hc), 1)
    same = (ri // CH) == (rj // CH)
    return (same & (ri >= rj), same & (ri > rj)) if d == 0 else (same & (ri <= rj), same & (ri < rj))


def _chunk_tri(d):
    ci = lax.broadcasted_iota(jnp.int32, (CH, CH), 0)
    cj = lax.broadcasted_iota(jnp.int32, (CH, CH), 1)
    return jnp.where(ci >= cj if d == 0 else ci <= cj, 1.0, 0.0).astype(F32)


def _rwkv_kernel(T, has_state, *refs):
    if has_state:
        (xb_ref, mu_ref, w0_ref, w2_ref, a0_ref, a2_ref, g2_ref, kk_ref, ka_ref, rk_ref, lnw_ref, lnb_ref, s0_ref,
         ob_ref, sf_ref, r_s, v_s, av_s, lw_s, kd_s, bv_s, y_s, bon_s, gate_s, st_s, mbd_s, mst_s, min_s) = refs
    else:
        (xb_ref, mu_ref, w0_ref, w2_ref, a0_ref, a2_ref, g2_ref, kk_ref, ka_ref, rk_ref, lnw_ref, lnb_ref,
         ob_ref, sf_ref, r_s, v_s, av_s, lw_s, kd_s, bv_s, y_s, bon_s, gate_s, st_s, mbd_s, mst_s, min_s) = refs
    n_chunks = T // CH
    n_blk = T // SEQ_BLK
    hc = B_HEADS * CH
    seg = _seg_matrix(B_WIDTH, B_DIM, 1.0)
    row = lax.broadcasted_iota(jnp.int32, (SEQ_BLK, 1), 0)
    for b in range(n_blk):
        r0 = b * SEQ_BLK
        rb = slice(r0, r0 + SEQ_BLK)
        x = xb_ref[rb, :]
        before = xb_ref[r0 - 1:r0, :] if b > 0 else jnp.zeros((1, B_COLS), F32)
        after = xb_ref[r0 + SEQ_BLK:r0 + SEQ_BLK + 1, :] if b < n_blk - 1 else jnp.zeros((1, B_COLS), F32)
        prev = jnp.where(row == 0, before, pltpu.roll(x, 1, 0))
        nxt = jnp.where(row == SEQ_BLK - 1, after, pltpu.roll(x, SEQ_BLK - 1, 0))
        xs = x + mu_ref[...] * (0.5 * (prev + nxt) - x)
        r = xs[:, 0:B_WIDTH]
        kb = xs[:, B_WIDTH:2 * B_WIDTH]
        vb = xs[:, 2 * B_WIDTH:3 * B_WIDTH]
        wd = xs[:, 3 * B_WIDTH:3 * B_WIDTH + DECAY_LORA]
        ad = xs[:, 3 * B_WIDTH + DECAY_LORA:3 * B_WIDTH + DECAY_LORA + ICLR_LORA]
        gd = xs[:, 3 * B_WIDTH + DECAY_LORA + ICLR_LORA:]
        kk = kb * kk_ref[...]
        kkn = kk * lax.rsqrt(_dot_hi(kk * kk, seg) + 1e-6)
        r_s[rb, :] = r
        v_s[rb, :] = vb
        av_s[rb, :] = -kkn
        bon_s[rb, :] = _dot_hi(r * kb * rk_ref[...], seg) * vb
        gate_s[rb, :] = _dot(jax.nn.sigmoid(gd), g2_ref[...])
        twd = jnp.tanh(wd)
        for d in range(2):
            wl = w0_ref[d:d + 1, :] + _dot(twd, w2_ref[d])
            w_log = -jax.nn.softplus(-wl) - 0.5
            lw_s[d, rb, :] = -jnp.exp(w_log)
            a = jax.nn.sigmoid(a0_ref[d:d + 1, :] + _dot(ad, a2_ref[d]))
            kd_s[d, rb, :] = kb * (1.0 + (a - 1.0) * ka_ref[...])
            bv_s[d, rb, :] = kkn * a
    if has_state:
        for d in range(2):
            st_s[d] = jnp.concatenate([s0_ref[0, d, h] for h in range(B_HEADS)], axis=1)
    else:
        st_s[...] = jnp.zeros_like(st_s)
    incl0, _ = _head_masks(B_HEADS, 0)
    incl1, _ = _head_masks(B_HEADS, 1)
    mbd_s[...] = jnp.where(incl0 | incl1, 1.0, 0.0).astype(F32)

    def expand(x):
        return jnp.concatenate([x] * B_HEADS, axis=1) * mbd_s[...]

    for d in range(2):
        incl, strict = _head_masks(B_HEADS, d)
        min_s[...] = jnp.where(incl, 1.0, 0.0).astype(F32)
        mst_s[...] = jnp.where(strict, 1.0, 0.0).astype(F32)
        tri = _chunk_tri(d)
        last = CH - 1 if d == 0 else 0

        def chunk_body(it, carry, d=d, tri=tri, last=last):
            c = it if d == 0 else n_chunks - 1 - it
            rows = pl.ds(pl.multiple_of(c * CH, CH), CH)
            lwc = lw_s[d, rows, :]
            cum = _dot_hi(tri, lwc)
            e_pos = jnp.exp(cum)
            e_neg = jnp.exp(-cum)
            rt = _head_stack(r_s[rows, :] * e_pos, B_HEADS, B_DIM)
            at = _head_stack(av_s[rows, :] * jnp.exp(cum - lwc), B_HEADS, B_DIM)
            bk = jnp.concatenate([_head_stack(bv_s[d, rows, :] * e_neg, B_HEADS, B_DIM),
                                  _head_stack(kd_s[d, rows, :] * e_neg, B_HEADS, B_DIM)], axis=0)
            vs = _head_stack(v_s[rows, :], B_HEADS, B_DIM)
            pc = e_pos[last:last + 1, :]
            g1 = _dot_nt(at, bk)
            g2 = _dot_nt(rt, bk)
            mst = mst_s[...]
            a_ab = g1[:, 0:hc] * mst
            a_ak = g1[:, hc:] * mst
            mi = min_s[...]
            a_r = jnp.concatenate([g2[:, 0:hc] * mi, g2[:, hc:] * mi], axis=1)
            xm = _unit_lower_solve(a_ab, jnp.concatenate([at, _dot(a_ak, vs)], axis=1), RWKV_SOLVE_BLK)
            s_old = st_s[d]
            ws = _dot_nt(jnp.concatenate([expand(xm[:, 0:B_DIM]), expand(rt)], axis=0), s_old)
            uv = jnp.concatenate([ws[0:hc, :] + xm[:, B_DIM:], vs], axis=0)
            ys = ws[hc:, :] + _dot(a_r, uv)
            bk_e = jnp.concatenate([expand(bk[0:hc, :]), expand(bk[hc:, :])], axis=0)
            st_s[d] = (s_old + _dot_tn(uv, bk_e)) * pc
            yc = _head_unstack(ys, B_HEADS)
            if d == 0:
                y_s[rows, :] = yc
            else:
                y_s[rows, :] = y_s[rows, :] + yc
            return carry

        lax.fori_loop(0, n_chunks, chunk_body, 0)
    for b in range(n_blk):
        rb = slice(b * SEQ_BLK, (b + 1) * SEQ_BLK)
        y = y_s[rb, :]
        yc = y - _dot_hi(y, seg) * (1.0 / B_DIM)
        var = _dot_hi(yc * yc, seg) * (1.0 / B_DIM)
        yn = yc * lax.rsqrt(var + LNX_EPS) * lnw_ref[...] + lnb_ref[...]
        ob_ref[rb, :] = (yn + bon_s[rb, :]) * gate_s[rb, :]
    for d in range(2):
        sd = st_s[d]
        for h in range(B_HEADS):
            sf_ref[0, d, h] = sd[:, h * B_DIM:(h + 1) * B_DIM]


def _seq_spec(T, cols, row_off_blocks):
    extra = {'pipeline_mode': pl.Buffered(1)} if T > SEQ_BLK else {}
    return pl.BlockSpec((T, cols), lambda i: (row_off_blocks + i, 0), **extra)


def _rwkv_call(xb, row_off_blocks, n_seq, T, p, s0=None):
    has_state = s0 is not None
    st_shape = (2, B_HEADS, B_DIM, B_DIM)
    hc = B_HEADS * CH
    in_specs = [_seq_spec(T, B_COLS, row_off_blocks),
                _full((1, B_COLS)), _full((2, B_WIDTH)), _full((2, DECAY_LORA, B_WIDTH)), _full((2, B_WIDTH)),
                _full((2, ICLR_LORA, B_WIDTH)), _full((GATE_LORA, B_WIDTH))] + [_full((1, B_WIDTH))] * 5
    args = [xb, p['b_mu'].reshape(1, -1), p['b_w0'], p['b_w2'], p['b_a0'], p['b_a2'], p['b_g2'],
            p['b_kk'].reshape(1, -1), p['b_ka'].reshape(1, -1), p['b_rk'].reshape(1, -1),
            p['b_lnx_w'].reshape(1, -1), p['b_lnx_b'].reshape(1, -1)]
    if has_state:
        in_specs.append(pl.BlockSpec((1,) + st_shape, lambda i: (i, 0, 0, 0, 0)))
        args.append(s0)
    scr = [pltpu.VMEM((T, B_WIDTH), F32)] * 3 + [pltpu.VMEM((2, T, B_WIDTH), F32)] * 3 + \
          [pltpu.VMEM((T, B_WIDTH), F32)] * 3 + [pltpu.VMEM((2, B_DIM, B_WIDTH), F32)] + \
          [pltpu.VMEM((hc, hc), F32)] * 3
    return pl.pallas_call(
        functools.partial(_rwkv_kernel, T, has_state),
        grid=(n_seq,),
        in_specs=in_specs,
        out_specs=[pl.BlockSpec((T, B_WIDTH), lambda i: (i, 0)),
                   pl.BlockSpec((1,) + st_shape, lambda i: (i, 0, 0, 0, 0))],
        out_shape=[jax.ShapeDtypeStruct((n_seq * T, B_WIDTH), F32),
                   jax.ShapeDtypeStruct((n_seq,) + st_shape, F32)],
        scratch_shapes=scr,
        compiler_params=pltpu.CompilerParams(vmem_limit_bytes=VMEM_LIMIT),
    )(*args)


def _delta_kernel(T, has_state, *refs):
    if has_state:
        (c_ref, ab_ref, conv_ref, arow_ref, dtrow_ref, on_ref, s0_ref, oc_ref, sf_ref,
         q_s, k_s, v_s, g_s, b_s, o_s, st_s) = refs
    else:
        (c_ref, ab_ref, conv_ref, arow_ref, dtrow_ref, on_ref, oc_ref, sf_ref,
         q_s, k_s, v_s, g_s, b_s, o_s, st_s) = refs
    n_chunks = T // CH
    n_blk = T // SEQ_BLK
    hc = C_HEADS * CH
    row = lax.broadcasted_iota(jnp.int32, (SEQ_BLK, 1), 0)
    for b in range(n_blk):
        r0 = b * SEQ_BLK
        rb = slice(r0, r0 + SEQ_BLK)
        x = c_ref[rb, 0:3 * C_WIDTH]
        before = c_ref[r0 - 1:r0, 0:3 * C_WIDTH] if b > 0 else jnp.zeros((1, 3 * C_WIDTH), F32)
        after = c_ref[r0 + SEQ_BLK:r0 + SEQ_BLK + 1, 0:3 * C_WIDTH] if b < n_blk - 1 else jnp.zeros((1, 3 * C_WIDTH), F32)
        prev = jnp.where(row == 0, before, pltpu.roll(x, 1, 0))
        nxt = jnp.where(row == SEQ_BLK - 1, after, pltpu.roll(x, SEQ_BLK - 1, 0))
        xc = _silu(conv_ref[0:1, :] * prev + conv_ref[1:2, :] * x + conv_ref[2:3, :] * nxt)
        for h in range(C_HEADS):
            sl = slice(h * C_DIM, (h + 1) * C_DIM)
            qh = xc[:, h * C_DIM:(h + 1) * C_DIM]
            kh = xc[:, C_WIDTH + h * C_DIM:C_WIDTH + (h + 1) * C_DIM]
            q_s[rb, sl] = qh * lax.rsqrt(jnp.sum(qh * qh, axis=-1, keepdims=True) + 1e-6) * (C_DIM ** -0.5)
            k_s[rb, sl] = kh * lax.rsqrt(jnp.sum(kh * kh, axis=-1, keepdims=True) + 1e-6)
        v_s[rb, :] = xc[:, 2 * C_WIDTH:]
        ab = ab_ref[rb, :]
        g_s[rb, :] = arow_ref[...] * jax.nn.softplus(ab + dtrow_ref[...])
        b_s[rb, :] = jax.nn.sigmoid(ab)
    if has_state:
        for d in range(2):
            st_s[d] = jnp.concatenate([s0_ref[0, d, h] for h in range(C_HEADS)], axis=0)
    else:
        st_s[...] = jnp.zeros_like(st_s)
    ei = lax.broadcasted_iota(jnp.int32, (hc, C_WIDTH), 0) // CH
    ej = lax.broadcasted_iota(jnp.int32, (hc, C_WIDTH), 1) // C_DIM
    own = ei == ej

    def expand(x):
        return jnp.where(own, jnp.concatenate([x] * C_HEADS, axis=1), 0.0)

    ri = lax.broadcasted_iota(jnp.int32, (hc, hc), 0)
    rj = lax.broadcasted_iota(jnp.int32, (hc, hc), 1)
    eye = jnp.where(ri == rj, 1.0, 0.0).astype(F32)
    for d in range(2):
        incl, strict = _head_masks(C_HEADS, d)
        tri = _chunk_tri(d)
        last = CH - 1 if d == 0 else 0
        j0 = d * C_HEADS

        def chunk_body(it, carry, d=d, incl=incl, strict=strict, tri=tri, last=last, j0=j0):
            c = it if d == 0 else n_chunks - 1 - it
            rows = pl.ds(pl.multiple_of(c * CH, CH), CH)
            gcum = _dot_hi(tri, g_s[rows, :])
            bet = b_s[rows, :]
            heads = range(C_HEADS)
            g_col = jnp.concatenate([gcum[:, j0 + h:j0 + h + 1] for h in heads], axis=0)
            b_col = jnp.concatenate([bet[:, 2 * C_HEADS + j0 + h:2 * C_HEADS + j0 + h + 1] for h in heads], axis=0)
            g_end = [gcum[last:last + 1, j0 + h:j0 + h + 1] for h in heads]
            g_last = jnp.concatenate([jnp.broadcast_to(g, (CH, 1)) for g in g_end], axis=0)
            e_last = jnp.concatenate([jnp.broadcast_to(jnp.exp(g), (C_DIM, 1)) for g in g_end], axis=0)
            g_row = jnp.sum(eye * g_col, axis=0, keepdims=True)
            decay = jnp.where(incl, jnp.exp(jnp.where(incl, g_col - g_row, 0.0)), 0.0)
            qs = _head_stack(q_s[rows, :], C_HEADS, C_DIM)
            ks = _head_stack(k_s[rows, :], C_HEADS, C_DIM)
            vs = _head_stack(v_s[rows, :], C_HEADS, C_DIM)
            kbeta = ks * b_col
            gm = _dot_nt(jnp.concatenate([kbeta, qs], axis=0), ks)
            attn = gm[hc:, :] * decay
            eg = jnp.exp(g_col)
            xm = _unit_lower_solve(-jnp.where(strict, gm[0:hc, :] * decay, 0.0),
                                   jnp.concatenate([vs * b_col, kbeta * eg], axis=1), hc)
            s_old = st_s[d]
            ws = _dot(jnp.concatenate([expand(xm[:, C_DIM:]), expand(qs * eg)], axis=0), s_old)
            v_new = xm[:, 0:C_DIM] - ws[0:hc, :]
            o = ws[hc:, :] + _dot(attn, v_new)
            st_s[d] = s_old * e_last + _dot_tn(expand(ks * jnp.exp(g_last - g_col)), v_new)
            oc = _head_unstack(o, C_HEADS)
            if d == 0:
                o_s[rows, :] = oc
            else:
                o_s[rows, :] = o_s[rows, :] + oc
            return carry

        lax.fori_loop(0, n_chunks, chunk_body, 0)
    for b in range(n_blk):
        rb = slice(b * SEQ_BLK, (b + 1) * SEQ_BLK)
        z = c_ref[rb, 3 * C_WIDTH:]
        for h in range(C_HEADS):
            sl = slice(h * C_DIM, (h + 1) * C_DIM)
            oh = o_s[rb, sl]
            on = oh * lax.rsqrt(jnp.mean(oh * oh, axis=-1, keepdims=True) + NORM_EPS) * on_ref[...]
            oc_ref[rb, sl] = on * _silu(z[:, sl])
    for d in range(2):
        sd = st_s[d]
        for h in range(C_HEADS):
            sf_ref[0, d, h] = sd[h * C_DIM:(h + 1) * C_DIM, :]


def _delta_call(cmat, abmat, row_off_blocks, n_seq, T, p, s0=None):
    has_state = s0 is not None
    st_shape = (2, C_HEADS, C_DIM, C_DIM)
    arow = jnp.zeros((1, LANE), F32).at[0, 0:2 * C_HEADS].set(-jnp.exp(p['c_A_log'].reshape(-1)))
    dtrow = jnp.zeros((1, LANE), F32).at[0, 0:2 * C_HEADS].set(p['c_dt_bias'].reshape(-1))
    in_specs = [pl.BlockSpec((T, 4 * C_WIDTH), lambda i: (row_off_blocks + i, 0)),
                pl.BlockSpec((T, LANE), lambda i: (row_off_blocks + i, 0)),
                _full((3, 3 * C_WIDTH)), _full((1, LANE)), _full((1, LANE)), _full((1, C_DIM))]
    args = [cmat, abmat, p['c_conv'], arow, dtrow, p['c_onorm'].reshape(1, -1)]
    if has_state:
        in_specs.append(pl.BlockSpec((1,) + st_shape, lambda i: (i, 0, 0, 0, 0)))
        args.append(s0)
    scr = [pltpu.VMEM((T, C_WIDTH), F32)] * 3 + [pltpu.VMEM((T, LANE), F32)] * 2 + \
          [pltpu.VMEM((T, C_WIDTH), F32), pltpu.VMEM((2, C_HEADS * C_DIM, C_DIM), F32)]
    return pl.pallas_call(
        functools.partial(_delta_kernel, T, has_state),
        grid=(n_seq,),
        in_specs=in_specs,
        out_specs=[pl.BlockSpec((T, C_WIDTH), lambda i: (i, 0)),
                   pl.BlockSpec((1,) + st_shape, lambda i: (i, 0, 0, 0, 0))],
        out_shape=[jax.ShapeDtypeStruct((n_seq * T, C_WIDTH), F32),
                   jax.ShapeDtypeStruct((n_seq,) + st_shape, F32)],
        scratch_shapes=scr,
        compiler_params=pltpu.CompilerParams(vmem_limit_bytes=VMEM_LIMIT),
    )(*args)


def _first_max(x, iota, size):
    m = jnp.max(x, axis=0, keepdims=True)
    idx = jnp.min(jnp.where(x == m, iota, size), axis=0, keepdims=True)
    return m, idx


def _route_kernel(lg_ref, bias_ref, idx_ref, wts_ref):
    n = lg_ref.shape[1]
    scores = jax.nn.sigmoid(lg_ref[...])
    biased = scores + bias_ref[...]
    e_iota = lax.broadcasted_iota(jnp.int32, (N_EXPERTS, n), 0)
    g_iota = lax.broadcasted_iota(jnp.int32, (PER_GROUP, n), 0)
    gs = []
    for g in range(N_GROUPS):
        xg = biased[g * PER_GROUP:(g + 1) * PER_GROUP, :]
        m1, i1 = _first_max(xg, g_iota, PER_GROUP)
        m2 = jnp.max(jnp.where(g_iota == i1, NEG, xg), axis=0, keepdims=True)
        gs.append(m1 + m2)
    gscore = jnp.concatenate(gs, axis=0)
    gi = lax.broadcasted_iota(jnp.int32, (N_GROUPS, n), 0)
    gsel = jnp.zeros((N_GROUPS, n), F32)
    for _ in range(TOPK_GROUPS):
        _, ig = _first_max(gscore, gi, N_GROUPS)
        hit = gi == ig
        gsel = jnp.where(hit, 1.0, gsel)
        gscore = jnp.where(hit, NEG, gscore)
    masked = jnp.concatenate(
        [jnp.where(gsel[g:g + 1, :] > 0.0, biased[g * PER_GROUP:(g + 1) * PER_GROUP, :], NEG)
         for g in range(N_GROUPS)], axis=0)
    ids, ws = [], []
    for _ in range(TOP_K):
        _, ie = _first_max(masked, e_iota, N_EXPERTS)
        hit = e_iota == ie
        ids.append(ie)
        ws.append(jnp.sum(jnp.where(hit, scores, 0.0), axis=0, keepdims=True))
        masked = jnp.where(hit, NEG, masked)
    wsum = ws[0]
    for w in ws[1:]:
        wsum = wsum + w
    inv = ROUTED_SCALE / (wsum + 1e-20)
    idx_ref[...] = jnp.concatenate(ids + [jnp.zeros((8 - TOP_K, n), jnp.int32)], axis=0)
    wts_ref[...] = jnp.concatenate([w * inv for w in ws] + [jnp.zeros((8 - TOP_K, n), F32)], axis=0)


def _route_call(logits_t, bias):
    m = logits_t.shape[1]
    return pl.pallas_call(
        _route_kernel,
        grid=(m // TM,),
        in_specs=[pl.BlockSpec((N_EXPERTS, TM), lambda i: (0, i)), _full((N_EXPERTS, 1))],
        out_specs=[pl.BlockSpec((8, TM), lambda i: (0, i))] * 2,
        out_shape=[jax.ShapeDtypeStruct((8, m), jnp.int32), jax.ShapeDtypeStruct((8, m), F32)],
    )(logits_t, bias.reshape(N_EXPERTS, 1))


def _dispatch_tables(idx, m):
    n_asg = m * TOP_K
    nb = -(-n_asg // BM) + N_EXPERTS
    n_pad = nb * BM - n_asg
    flat_e = idx[0:TOP_K, :].reshape(-1)
    e_iota = jnp.arange(N_EXPERTS, dtype=jnp.int32)
    counts = jnp.sum((flat_e[:, None] == e_iota[None, :]).astype(jnp.int32), axis=0)
    padded = (counts + BM - 1) // BM * BM
    pad_end = jnp.cumsum(padded)
    pad_cum = jnp.cumsum(padded - counts)
    pad_e = jnp.sum((pad_cum[None, :] <= jnp.arange(n_pad, dtype=jnp.int32)[:, None]).astype(jnp.int32), axis=1)
    keys = jnp.concatenate([flat_e * 2, pad_e * 2 + 1])
    vals = jnp.concatenate([jnp.arange(n_asg, dtype=jnp.int32), jnp.full((n_pad,), -1, jnp.int32)])
    _, slot_asg = lax.sort((keys, vals), num_keys=1, is_stable=True)
    valid = slot_asg >= 0
    spare = TOP_K * m + jnp.arange(nb * BM, dtype=jnp.int32) % BM
    slot_dst = jnp.where(valid, slot_asg, spare)
    slot_tok = jnp.where(valid, slot_asg % m, 0)
    blk0 = jnp.arange(nb, dtype=jnp.int32) * BM
    block_expert = jnp.minimum(jnp.sum((pad_end[None, :] <= blk0[:, None]).astype(jnp.int32), axis=1), N_EXPERTS - 1)
    n_used = (pad_end[-1] // BM).astype(jnp.int32).reshape(1)
    return slot_tok.reshape(nb, 1, BM), slot_dst.reshape(nb, 1, BM), block_expert.astype(jnp.int32), n_used


def _expert_kernel(be_ref, nu_ref, st_ref, stn_ref, sd_ref, tok_hbm, wg_ref, wu_ref, wd_ref, y_hbm,
                   xbuf, ybuf, sem_in, sem_out):
    i = pl.program_id(0)
    n_used = nu_ref[0]
    slot = i % 2

    def gather(tab_ref, b):
        def body(r, c):
            pltpu.make_async_copy(tok_hbm.at[pl.ds(tab_ref[0, 0, r], 1), :], xbuf.at[b, pl.ds(r, 1), :],
                                  sem_in.at[b]).start()
            return c
        lax.fori_loop(0, BM, body, 0, unroll=DMA_UNROLL)

    def wait_gather(b):
        pltpu.make_async_copy(tok_hbm.at[pl.ds(0, BM), :], xbuf.at[b], sem_in.at[b]).wait()

    def wait_scatter(b):
        pltpu.make_async_copy(ybuf.at[b], y_hbm.at[pl.ds(0, BM), :], sem_out.at[b]).wait()

    @pl.when(i == 0)
    def _():
        ybuf[0] = jnp.zeros((BM, D_MODEL), F32)
        spare = pltpu.make_async_copy(ybuf.at[0], y_hbm.at[pl.ds(y_hbm.shape[0] - BM, BM), :], sem_out.at[0])
        spare.start()
        spare.wait()
        gather(st_ref, 0)

    @pl.when(i + 1 < n_used)
    def _():
        gather(stn_ref, 1 - slot)

    @pl.when(i < n_used)
    def _():
        wait_gather(slot)

        @pl.when(i >= 2)
        def _():
            wait_scatter(slot)

        x = xbuf[slot].astype(BF16)
        g = jnp.dot(x, wg_ref[0, 0].astype(BF16), preferred_element_type=F32)
        u = jnp.dot(x, wu_ref[0, 0].astype(BF16), preferred_element_type=F32)
        ybuf[slot] = jnp.dot((_silu(g) * u).astype(BF16), wd_ref[0, 0].astype(BF16), preferred_element_type=F32)

        def body(r, c):
            pltpu.make_async_copy(ybuf.at[slot, pl.ds(r, 1), :], y_hbm.at[pl.ds(sd_ref[0, 0, r], 1), :],
                                  sem_out.at[slot]).start()
            return c
        lax.fori_loop(0, BM, body, 0, unroll=DMA_UNROLL)

        @pl.when(i == n_used - 1)
        def _():
            wait_scatter(slot)

            @pl.when(i >= 1)
            def _():
                wait_scatter(1 - slot)


def _expert_call(tok, tables, layer, wg, wu, wd):
    m = tok.shape[0]
    slot_tok, slot_dst, block_expert, n_used = tables
    nb = slot_tok.shape[0]
    smem_blk = lambda f: pl.BlockSpec((1, 1, BM), lambda i, be, nu: (f(i), 0, 0), memory_space=pltpu.SMEM)
    grid_spec = pltpu.PrefetchScalarGridSpec(
        num_scalar_prefetch=2,
        grid=(nb,),
        in_specs=[smem_blk(lambda i: i), smem_blk(lambda i: jnp.minimum(i + 1, nb - 1)), smem_blk(lambda i: i),
                  pl.BlockSpec(memory_space=pl.ANY),
                  pl.BlockSpec((1, 1, D_MODEL, EXPERT_FF), lambda i, be, nu: (layer, be[i], 0, 0)),
                  pl.BlockSpec((1, 1, D_MODEL, EXPERT_FF), lambda i, be, nu: (layer, be[i], 0, 0)),
                  pl.BlockSpec((1, 1, EXPERT_FF, D_MODEL), lambda i, be, nu: (layer, be[i], 0, 0))],
        out_specs=pl.BlockSpec(memory_space=pl.ANY),
        scratch_shapes=[pltpu.VMEM((2, BM, D_MODEL), F32), pltpu.VMEM((2, BM, D_MODEL), F32),
                        pltpu.SemaphoreType.DMA((2,)), pltpu.SemaphoreType.DMA((2,))])
    return pl.pallas_call(
        _expert_kernel, grid_spec=grid_spec,
        out_shape=jax.ShapeDtypeStruct((TOP_K * m + BM, D_MODEL), F32),
        compiler_params=pltpu.CompilerParams(vmem_limit_bytes=VMEM_LIMIT),
    )(block_expert, n_used, slot_tok, slot_tok, slot_dst, tok, wg, wu, wd)


def _combine_kernel(x_ref, sh_ref, w_ref, mod_ref, *rest):
    y_refs, o_ref = rest[:TOP_K], rest[TOP_K]
    w = w_ref[...]
    acc = sh_ref[...]
    for k, y_ref in enumerate(y_refs):
        acc = acc + w[:, k:k + 1] * y_ref[...]
    o_ref[...] = x_ref[...] + mod_ref[0, 0, 5:6, :] * acc


def _combine_call(x, sh, wts_rows, mod, layer, yrows):
    nt = M_TOK // TM
    row = pl.BlockSpec((TM, D_MODEL), lambda i: (i, 0))
    ysp = [pl.BlockSpec((TM, D_MODEL), (lambda i, k=k: (k * nt + i, 0))) for k in range(TOP_K)]
    return pl.pallas_call(
        _combine_kernel,
        grid=(nt,),
        in_specs=[row, row, pl.BlockSpec((TM, 8), lambda i: (i, 0)),
                  pl.BlockSpec((1, 1, 6, D_MODEL), lambda i: (layer, _mod_group(i), 0, 0))] + ysp,
        out_specs=row,
        out_shape=jax.ShapeDtypeStruct((M_TOK, D_MODEL), F32),
        compiler_params=pltpu.CompilerParams(vmem_limit_bytes=VMEM_LIMIT),
    )(x, sh, wts_rows, mod, *([yrows] * TOP_K))


def kernel(x_prompt, x_sample, cache_attn_k, cache_attn_v, state_rwkv, state_delta, cache_diff_k,
           cache_diff_v, c, c_ctx, mod_w, mod_b, norm1_g, norm2_g, ev_w_in, ev_w_out, a_qn, a_kn, b_mu,
           b_w0, b_w2, b_a0, b_a2, b_g2, b_kk, b_ka, b_rk, b_lnx_w, b_lnx_b, od_w_in, od_w_out, c_conv,
           c_A_log, c_dt_bias, c_onorm, d_qn, d_kn, d_lambda, d_subln, router_w, router_bias, exp_w_gate,
           exp_w_up, exp_w_down, sh_w_gate, sh_w_up, sh_w_down):
    x = jnp.concatenate([x_prompt.reshape(N_CTX, D_MODEL), x_sample.reshape(N_LAT, D_MODEL)], axis=0)
    cond = jnp.concatenate([c_ctx[None], c, jnp.zeros((8 - N_MOD, D_MODEL), F32)], axis=0)
    mod = _adaln_call(cond, mod_w, mod_b)[:, 0:N_MOD].reshape(DEPTH, N_MOD, 6, D_MODEL)
    lat_blk = N_CTX // DEC_SEQ
    new_ak, new_av, new_sr, new_sd, new_dk, new_dv = [], [], [], [], [], []
    for l in range(DEPTH):
        j = l // 2
        if l % 2 == 0:
            w = ev_w_in[j]
            q, k, v = w[:, 0:512], w[:, 512:640], w[:, 640:768]
            w_a = jnp.concatenate([q[:, 0:256], k[:, 0:64], v[:, 0:64], q[:, 256:], k[:, 64:], v[:, 64:]], axis=1)
            amat, bmat = _proj_in_call(x, mod, l, norm1_g[l], [w_a.astype(BF16), w[:, 768:].astype(BF16)])
            gain = jnp.concatenate([jnp.tile(a_qn[j], A_GROUP), a_kn[j], jnp.ones((HD,), F32)]).reshape(1, GW)
            oa_c, nk, nv = _attn_a_call(amat, 0, BATCH, SEQ, gain)
            oa_l, = _attn_a_call(amat, lat_blk, DEC_BATCH, DEC_SEQ, gain, (cache_attn_k, cache_attn_v, j))
            p = {'b_mu': b_mu[j], 'b_w0': b_w0[j], 'b_w2': b_w2[j], 'b_a0': b_a0[j], 'b_a2': b_a2[j],
                 'b_g2': b_g2[j], 'b_kk': b_kk[j], 'b_ka': b_ka[j], 'b_rk': b_rk[j], 'b_lnx_w': b_lnx_w[j],
                 'b_lnx_b': b_lnx_b[j]}
            ob_c, sr = _rwkv_call(bmat, 0, BATCH, SEQ, p)
            ob_l, _ = _rwkv_call(bmat, lat_blk, DEC_BATCH, DEC_SEQ, p, state_rwkv[:, j])
            o1 = jnp.concatenate([oa_c, oa_l], axis=0)
            o2 = jnp.concatenate([ob_c, ob_l], axis=0)
            w_out = ev_w_out[j]
            new_ak.append(nk)
            new_av.append(nv)
            new_sr.append(sr)
        else:
            lam_init = 0.8 - 0.6 * math.exp(-0.3 * l)
            w = od_w_in[j]
            s0 = 4 * C_WIDTH + 4 * C_HEADS
            dq, dk, dv = w[:, s0:s0 + 512], w[:, s0 + 512:s0 + 1024], w[:, s0 + 1024:]
            w_ab = jnp.pad(w[:, 4 * C_WIDTH:s0], ((0, 0), (0, LANE - 4 * C_HEADS)))
            w_d = jnp.concatenate([jnp.concatenate([dq[:, 128 * h:128 * (h + 1)], dk[:, 128 * h:128 * (h + 1)],
                                                    dv[:, 128 * h:128 * (h + 1)]], axis=1) for h in range(D_HEADS)],
                                  axis=1)
            cmat, abmat, dmat = _proj_in_call(x, mod, l, norm1_g[l],
                                              [w[:, 0:4 * C_WIDTH].astype(BF16), w_ab.astype(BF16), w_d.astype(BF16)])
            p = {'c_conv': c_conv[j], 'c_A_log': c_A_log[j], 'c_dt_bias': c_dt_bias[j], 'c_onorm': c_onorm[j]}
            oc_c, sd_ = _delta_call(cmat, abmat, 0, BATCH, SEQ, p)
            oc_l, _ = _delta_call(cmat, abmat, lat_blk, DEC_BATCH, DEC_SEQ, p, state_delta[:, j])
            gain = jnp.concatenate([jnp.tile(d_qn[j], 2), jnp.tile(d_kn[j], 2), jnp.ones((D_VDIM,), F32)]).reshape(1, GW)
            od_c, ndk, ndv = _attn_d_call(dmat, 0, BATCH, SEQ, gain, d_lambda[j], d_subln[j], lam_init)
            od_l, = _attn_d_call(dmat, lat_blk, DEC_BATCH, DEC_SEQ, gain, d_lambda[j], d_subln[j], lam_init,
                                 (cache_diff_k, cache_diff_v, j))
            o1 = jnp.concatenate([oc_c, oc_l], axis=0)
            o2 = jnp.concatenate([od_c, od_l], axis=0)
            w_out = od_w_out[j]
            new_dk.append(ndk)
            new_dv.append(ndv)
            new_sd.append(sd_)
        x, tok, sh, logits_t = _proj_out_call(
            o1, o2, x, mod, l, norm2_g[l], w_out.astype(BF16), router_w[l].T.astype(BF16),
            sh_w_gate[l].astype(BF16), sh_w_up[l].astype(BF16), sh_w_down[l].astype(BF16))
        idx, wts = _route_call(logits_t, router_bias[l])
        tables = _dispatch_tables(idx, M_TOK)
        yrows = _expert_call(tok, tables, l, exp_w_gate, exp_w_up, exp_w_down)
        x = _combine_call(x, sh, wts.T, mod, l, yrows)
    return (x[0:N_CTX].reshape(BATCH, SEQ, D_MODEL), x[N_CTX:].reshape(DEC_BATCH, DEC_SEQ, D_MODEL),
            jnp.stack(new_ak, axis=1), jnp.stack(new_av, axis=1), jnp.stack(new_sr, axis=1),
            jnp.stack(new_sd, axis=1), jnp.stack(new_dk, axis=1), jnp.stack(new_dv, axis=1))
```

```python
import functools
import math
import jax
import jax.numpy as jnp
from jax import lax
from jax.experimental import pallas as pl
from jax.experimental.pallas import tpu as pltpu

D_MODEL = 1024
BATCH = 32
SEQ = 256
DEPTH = 4
DEC_BATCH = 2
DEC_SEQ = 1024
PAST_LEN = 512
GRID_W = 64
NORM_EPS = 1e-6
ROPE_THETA = 10000.0
A_HEADS = 8
A_KV_HEADS = 2
A_GROUP = A_HEADS // A_KV_HEADS
B_HEADS = 8
B_DIM = 64
B_WIDTH = B_HEADS * B_DIM
DECAY_LORA = 64
ICLR_LORA = 64
GATE_LORA = 128
B_COLS = 3 * B_WIDTH + DECAY_LORA + ICLR_LORA + GATE_LORA
LNX_EPS = 64e-5
C_HEADS = 4
C_DIM = 128
C_WIDTH = C_HEADS * C_DIM
D_HEADS = 4
D_VDIM = 128
N_EXPERTS = 64
TOP_K = 6
N_GROUPS = 8
TOPK_GROUPS = 4
PER_GROUP = N_EXPERTS // N_GROUPS
EXPERT_FF = 256
SHARED_FF = 256
ROUTED_SCALE = 1.0

HD = 64
GW = 384
CH = 64
LANE = 128
BM = 128
DMA_UNROLL = 8
TM = 512
SEQ_BLK = 256
RWKV_SOLVE_BLK = 128
N_CTX = BATCH * SEQ
N_LAT = DEC_BATCH * DEC_SEQ
M_TOK = N_CTX + N_LAT
N_MOD = 1 + DEC_BATCH
VMEM_LIMIT = 56 * 1024 * 1024

F32 = jnp.float32
BF16 = jnp.bfloat16
HI = lax.Precision.HIGHEST
NEG = -jnp.inf


def _dot(a, b):
    return jnp.dot(a.astype(BF16), b.astype(BF16), preferred_element_type=F32)


def _dot_nt(a, b):
    return lax.dot_general(a.astype(BF16), b.astype(BF16), (((1,), (1,)), ((), ())), preferred_element_type=F32)


def _dot_tn(a, b):
    return lax.dot_general(a.astype(BF16), b.astype(BF16), (((0,), (0,)), ((), ())), preferred_element_type=F32)


def _split3(x):
    x1 = x.astype(BF16)
    r = x - x1.astype(F32)
    x2 = r.astype(BF16)
    return x1, x2, (r - x2.astype(F32)).astype(BF16)


def _dot_data_const(a, c):
    cb = c.astype(BF16)
    a1, a2, a3 = _split3(a)
    d = lambda x: jnp.dot(x, cb, preferred_element_type=F32)
    return d(a1) + (d(a2) + d(a3))


def _dot_const_data(c, b):
    cb = c.astype(BF16)
    b1, b2, b3 = _split3(b)
    d = lambda x: jnp.dot(cb, x, preferred_element_type=F32)
    return d(b1) + (d(b2) + d(b3))


def _silu(x):
    return x * jax.nn.sigmoid(x)


def _mod_group(i):
    n_ctx_tiles = N_CTX // TM
    return jnp.where(i < n_ctx_tiles, 0, 1 + (i - n_ctx_tiles) // (DEC_SEQ // TM))


def _full(shape):
    return pl.BlockSpec(shape, lambda *_: (0,) * len(shape))


def _adaln_kernel(c_ref, w_ref, b_ref, o_ref):
    o_ref[0] = _dot(_silu(c_ref[...]), w_ref[0]) + b_ref[0]


def _adaln_call(cond, mod_w, mod_b):
    n = 6
    return pl.pallas_call(
        _adaln_kernel,
        grid=(DEPTH, n),
        in_specs=[_full((8, D_MODEL)),
                  pl.BlockSpec((1, D_MODEL, D_MODEL), lambda l, j: (l, 0, j)),
                  pl.BlockSpec((1, 1, D_MODEL), lambda l, j: (l, 0, j))],
        out_specs=pl.BlockSpec((1, 8, D_MODEL), lambda l, j: (l, 0, j)),
        out_shape=jax.ShapeDtypeStruct((DEPTH, 8, n * D_MODEL), F32),
    )(cond, mod_w, mod_b.reshape(DEPTH, 1, n * D_MODEL))


def _proj_in_kernel(n_out, x_ref, mod_ref, g_ref, *rest):
    x = x_ref[...]
    y = x * lax.rsqrt(jnp.mean(x * x, axis=-1, keepdims=True) + NORM_EPS) * g_ref[...]
    h = (y * (1.0 + mod_ref[0, 0, 1:2, :]) + mod_ref[0, 0, 0:1, :]).astype(BF16)
    for w_ref, o_ref in zip(rest[:n_out], rest[n_out:]):
        o_ref[...] = jnp.dot(h, w_ref[...], preferred_element_type=F32)


def _proj_in_call(x, mod, layer, g, weights):
    n_out = len(weights)
    return pl.pallas_call(
        functools.partial(_proj_in_kernel, n_out),
        grid=(M_TOK // TM,),
        in_specs=[pl.BlockSpec((TM, D_MODEL), lambda i: (i, 0)),
                  pl.BlockSpec((1, 1, 6, D_MODEL), lambda i: (layer, _mod_group(i), 0, 0)),
                  _full((1, D_MODEL))] + [_full(w.shape) for w in weights],
        out_specs=[pl.BlockSpec((TM, w.shape[1]), lambda i: (i, 0)) for w in weights],
        out_shape=[jax.ShapeDtypeStruct((M_TOK, w.shape[1]), F32) for w in weights],
        compiler_params=pltpu.CompilerParams(vmem_limit_bytes=VMEM_LIMIT),
    )(x, mod, g.reshape(1, D_MODEL), *weights)


def _proj_out_kernel(o1_ref, o2_ref, x_ref, mod_ref, g_ref, wo_ref, rwt_ref, sg_ref, su_ref, sd_ref,
                     xn_ref, tok_ref, sh_ref, lg_ref):
    o = jnp.concatenate([o1_ref[...], o2_ref[...]], axis=1).astype(BF16)
    xn = x_ref[...] + mod_ref[0, 0, 2:3, :] * jnp.dot(o, wo_ref[...], preferred_element_type=F32)
    xn_ref[...] = xn
    y = xn * lax.rsqrt(jnp.mean(xn * xn, axis=-1, keepdims=True) + NORM_EPS) * g_ref[...]
    h2 = y * (1.0 + mod_ref[0, 0, 4:5, :]) + mod_ref[0, 0, 3:4, :]
    tok_ref[...] = h2
    hb = h2.astype(BF16)
    lg_ref[...] = _dot_nt(rwt_ref[...], hb)
    hid = _silu(jnp.dot(hb, sg_ref[...], preferred_element_type=F32)) * jnp.dot(hb, su_ref[...], preferred_element_type=F32)
    sh_ref[...] = jnp.dot(hid.astype(BF16), sd_ref[...], preferred_element_type=F32)


def _proj_out_call(o1, o2, x, mod, layer, g2, w_out, rw_t, sg, su, sd):
    row = lambda n: pl.BlockSpec((TM, n), lambda i: (i, 0))
    return pl.pallas_call(
        _proj_out_kernel,
        grid=(M_TOK // TM,),
        in_specs=[row(o1.shape[1]), row(o2.shape[1]), row(D_MODEL),
                  pl.BlockSpec((1, 1, 6, D_MODEL), lambda i: (layer, _mod_group(i), 0, 0)),
                  _full((1, D_MODEL)), _full(w_out.shape), _full(rw_t.shape), _full(sg.shape), _full(su.shape),
                  _full(sd.shape)],
        out_specs=[row(D_MODEL), row(D_MODEL), row(D_MODEL), pl.BlockSpec((N_EXPERTS, TM), lambda i: (0, i))],
        out_shape=[jax.ShapeDtypeStruct((M_TOK, D_MODEL), F32)] * 3 + [jax.ShapeDtypeStruct((N_EXPERTS, M_TOK), F32)],
        compiler_params=pltpu.CompilerParams(vmem_limit_bytes=VMEM_LIMIT),
    )(o1, o2, x, mod, g2.reshape(1, D_MODEL), w_out, rw_t, sg, su, sd)


def _rope_lane_tables(n_tok):
    rows = n_tok // GRID_W
    row = jnp.repeat(jnp.arange(rows, dtype=F32), GRID_W)
    col = jnp.tile(jnp.arange(GRID_W, dtype=F32), rows)
    n_freq = HD // 4
    inv = ROPE_THETA ** (-jnp.arange(n_freq, dtype=F32) / n_freq)
    ang = jnp.concatenate([row[:, None] * inv, col[:, None] * inv], axis=-1)
    cos = jnp.repeat(jnp.cos(ang), 2, axis=-1)
    sin = jnp.repeat(jnp.sin(ang), 2, axis=-1) * jnp.tile(jnp.array([-1.0, 1.0], F32), HD // 2)
    return cos, sin


def _seg_matrix(width, seg, value):
    li = lax.broadcasted_iota(jnp.int32, (width, width), 0) // seg
    lj = lax.broadcasted_iota(jnp.int32, (width, width), 1) // seg
    return jnp.where(li == lj, value, 0.0).astype(F32)


def _norm_rope(x, gain, n_norm, cos, sin):
    lane = lax.broadcasted_iota(jnp.int32, x.shape, 1)
    ms = _dot_data_const(x * x, _seg_matrix(x.shape[1], HD, 1.0 / HD))
    xn = jnp.where(lane < n_norm, x * lax.rsqrt(ms + NORM_EPS) * gain, x)
    if cos is None:
        return xn, xn
    w = x.shape[1]
    swapped = jnp.where(lane % 2 == 0, pltpu.roll(xn, w - 1, 1), pltpu.roll(xn, 1, 1))
    return xn, xn * cos + swapped * sin


def _softmax_pv(s, v):
    m = jnp.max(s, axis=-1, keepdims=True)
    p = jnp.exp(s - m)
    l = jnp.sum(p, axis=-1, keepdims=True)
    return _dot(p, v) / l


def _attn_a_kernel(T, P, TQ, *refs):
    if P:
        x_ref, gain_ref, cos_ref, sin_ref, ck_ref, cv_ref, o_ref, xr_s = refs
        xn, xr = _norm_rope(x_ref[...], gain_ref[...], 5 * HD, cos_ref[...], sin_ref[...])
    else:
        x_ref, gain_ref, o_ref, nk_ref, nv_ref, xr_s = refs
        xn, xr = _norm_rope(x_ref[...], gain_ref[...], 5 * HD, None, None)
        nk_ref[0, 0] = xn[:, 4 * HD:5 * HD]
        nv_ref[0, 0] = xn[:, 5 * HD:6 * HD]
    xr_s[...] = xr
    k_new = xr[:, 4 * HD:5 * HD]
    v_new = xr[:, 5 * HD:6 * HD]
    if P:
        k_all = jnp.concatenate([ck_ref[0, 0, 0], k_new], axis=0).astype(BF16)
        v_all = jnp.concatenate([cv_ref[0, 0, 0], v_new], axis=0).astype(BF16)
    else:
        k_all, v_all = k_new.astype(BF16), v_new.astype(BF16)

    def q_block(qb, carry):
        rows = pl.ds(pl.multiple_of(qb * TQ, TQ), TQ)
        qx = xr_s[rows, 0:A_GROUP * HD]
        qs = jnp.concatenate([qx[:, i * HD:(i + 1) * HD] for i in range(A_GROUP)], axis=0)
        o = _softmax_pv(_dot_nt(qs, k_all) * (HD ** -0.5), v_all)
        o_ref[rows, :] = jnp.concatenate([o[i * TQ:(i + 1) * TQ, :] for i in range(A_GROUP)], axis=1)
        return carry

    lax.fori_loop(0, T // TQ, q_block, 0)


def _attn_a_call(amat, row_off_blocks, n_seq, T, gain, cache=None):
    P = 0 if cache is None else cache[0].shape[3]
    TQ = T if P == 0 else 128
    in_specs = [pl.BlockSpec((T, GW), lambda i, g: (row_off_blocks + i, g)), _full((1, GW))]
    args = [amat, gain]
    out_specs = [pl.BlockSpec((T, A_GROUP * HD), lambda i, g: (i, g))]
    out_shape = [jax.ShapeDtypeStruct((n_seq * T, A_HEADS * HD), F32)]
    if P:
        ck, cv, j = cache
        cos, sin = _rope_lane_tables(T)
        cos_g = jnp.concatenate([cos] * 5 + [jnp.ones((T, HD), F32)], axis=1)
        sin_g = jnp.concatenate([sin] * 5 + [jnp.zeros((T, HD), F32)], axis=1)
        in_specs += [_full((T, GW)), _full((T, GW)),
                     pl.BlockSpec((1, 1, 1, P, HD), lambda i, g: (i, j, g, 0, 0)),
                     pl.BlockSpec((1, 1, 1, P, HD), lambda i, g: (i, j, g, 0, 0))]
        args += [cos_g, sin_g, ck, cv]
    else:
        out_specs += [pl.BlockSpec((1, 1, T, HD), lambda i, g: (i, g, 0, 0))] * 2
        out_shape += [jax.ShapeDtypeStruct((n_seq, A_KV_HEADS, T, HD), F32)] * 2
    return pl.pallas_call(
        functools.partial(_attn_a_kernel, T, P, TQ),
        grid=(n_seq, A_KV_HEADS),
        in_specs=in_specs, out_specs=out_specs, out_shape=out_shape,
        scratch_shapes=[pltpu.VMEM((T, GW), F32)],
        compiler_params=pltpu.CompilerParams(vmem_limit_bytes=VMEM_LIMIT),
    )(*args)


def _attn_d_kernel(T, P, TQ, lam_init, *refs):
    if P:
        x_ref, gain_ref, lam_ref, sub_ref, cos_ref, sin_ref, ck_ref, cv_ref, o_ref, xr_s = refs
        xn, xr = _norm_rope(x_ref[...], gain_ref[...], 4 * HD, cos_ref[...], sin_ref[...])
    else:
        x_ref, gain_ref, lam_ref, sub_ref, o_ref, nk_ref, nv_ref, xr_s = refs
        xn, xr = _norm_rope(x_ref[...], gain_ref[...], 4 * HD, None, None)
        nk_ref[0, 0, 0] = xn[:, 2 * HD:3 * HD]
        nk_ref[0, 0, 1] = xn[:, 3 * HD:4 * HD]
        nv_ref[0, 0] = xn[:, 4 * HD:]
    xr_s[...] = xr
    v_new = xr[:, 4 * HD:]
    ks = []
    for m in range(2):
        k_new = xr[:, (2 + m) * HD:(3 + m) * HD]
        if P:
            ks.append(jnp.concatenate([ck_ref[0, 0, 0, m], k_new], axis=0).astype(BF16))
        else:
            ks.append(k_new.astype(BF16))
    v_all = (jnp.concatenate([cv_ref[0, 0, 0], v_new], axis=0) if P else v_new).astype(BF16)
    lm = lam_ref[...]
    lam = (jnp.exp(jnp.sum(lm[0:1, :] * lm[1:2, :], axis=-1, keepdims=True))
           - jnp.exp(jnp.sum(lm[2:3, :] * lm[3:4, :], axis=-1, keepdims=True)) + lam_init)

    def q_block(qb, carry):
        rows = pl.ds(pl.multiple_of(qb * TQ, TQ), TQ)
        qx = xr_s[rows, 0:2 * HD]
        o1 = _softmax_pv(_dot_nt(qx[:, 0:HD], ks[0]) * (HD ** -0.5), v_all)
        o2 = _softmax_pv(_dot_nt(qx[:, HD:], ks[1]) * (HD ** -0.5), v_all)
        od = o1 - lam * o2
        od = od * lax.rsqrt(jnp.mean(od * od, axis=-1, keepdims=True) + NORM_EPS) * sub_ref[...]
        o_ref[rows, :] = od * (1.0 - lam_init)
        return carry

    lax.fori_loop(0, T // TQ, q_block, 0)


def _attn_d_call(dmat, row_off_blocks, n_seq, T, gain, lam, subln, lam_init, cache=None):
    P = 0 if cache is None else cache[0].shape[4]
    TQ = T if P == 0 else 256
    in_specs = [pl.BlockSpec((T, GW), lambda i, h: (row_off_blocks + i, h)),
                _full((1, GW)), _full((4, HD)), _full((1, D_VDIM))]
    args = [dmat, gain, lam, subln.reshape(1, -1)]
    out_specs = [pl.BlockSpec((T, D_VDIM), lambda i, h: (i, h))]
    out_shape = [jax.ShapeDtypeStruct((n_seq * T, D_HEADS * D_VDIM), F32)]
    if P:
        ck, cv, j = cache
        cos, sin = _rope_lane_tables(T)
        cos_g = jnp.concatenate([cos] * 4 + [jnp.ones((T, D_VDIM), F32)], axis=1)
        sin_g = jnp.concatenate([sin] * 4 + [jnp.zeros((T, D_VDIM), F32)], axis=1)
        in_specs += [_full((T, GW)), _full((T, GW)),
                     pl.BlockSpec((1, 1, 1, 2, P, HD), lambda i, h: (i, j, h, 0, 0, 0)),
                     pl.BlockSpec((1, 1, 1, P, D_VDIM), lambda i, h: (i, j, h, 0, 0))]
        args += [cos_g, sin_g, ck, cv]
    else:
        out_specs += [pl.BlockSpec((1, 1, 2, T, HD), lambda i, h: (i, h, 0, 0, 0)),
                      pl.BlockSpec((1, 1, T, D_VDIM), lambda i, h: (i, h, 0, 0))]
        out_shape += [jax.ShapeDtypeStruct((n_seq, D_HEADS, 2, T, HD), F32),
                      jax.ShapeDtypeStruct((n_seq, D_HEADS, T, D_VDIM), F32)]
    return pl.pallas_call(
        functools.partial(_attn_d_kernel, T, P, TQ, lam_init),
        grid=(n_seq, D_HEADS),
        in_specs=in_specs, out_specs=out_specs, out_shape=out_shape,
        scratch_shapes=[pltpu.VMEM((T, GW), F32)],
        compiler_params=pltpu.CompilerParams(vmem_limit_bytes=VMEM_LIMIT),
    )(*args)


def _dot_3pass(a, b):
    a_hi = a.astype(BF16)
    a_lo = (a - a_hi.astype(F32)).astype(BF16)
    b_hi = b.astype(BF16)
    b_lo = (b - b_hi.astype(F32)).astype(BF16)
    d = lambda x, y: jnp.dot(x, y, preferred_element_type=F32)
    return d(a_hi, b_hi) + (d(a_hi, b_lo) + d(a_lo, b_hi))


def _unit_lower_solve(a_bd, x, blk):
    n_stage = CH.bit_length() - 1
    outs = []
    for g in range(a_bd.shape[0] // blk):
        sl = slice(g * blk, (g + 1) * blk)
        am, xm = a_bd[sl, sl], x[sl, :]
        for s in range(n_stage):
            xm = xm + _dot_3pass(am, xm)
            if s < n_stage - 1:
                am = _dot_3pass(am, am)
        outs.append(xm)
    return jnp.concatenate(outs, axis=0)


def _head_stack(x, n_heads, width):
    return jnp.concatenate([x[:, h * width:(h + 1) * width] for h in range(n_heads)], axis=0)


def _head_unstack(x, n_heads):
    return jnp.concatenate([x[h * CH:(h + 1) * CH, :] for h in range(n_heads)], axis=1)


def _head_masks(n_heads, d):
    hc = n_heads * CH
    ri = lax.broadcasted_iota(jnp.int32, (hc, hc), 0)
    rj = lax.broadcasted_iota(jnp.int32, (hc, hc), 1)
    same = (ri // CH) == (rj // CH)
    return (same & (ri >= rj), same & (ri > rj)) if d == 0 else (same & (ri <= rj), same & (ri < rj))


def _chunk_tri(d):
    ci = lax.broadcasted_iota(jnp.int32, (CH, CH), 0)
    cj = lax.broadcasted_iota(jnp.int32, (CH, CH), 1)
    return jnp.where(ci >= cj if d == 0 else ci <= cj, 1.0, 0.0).astype(F32)


def _rwkv_kernel(T, has_state, *refs):
    if has_state:
        (xb_ref, mu_ref, w0_ref, w2_ref, a0_ref, a2_ref, g2_ref, kk_ref, ka_ref, rk_ref, lnw_ref, lnb_ref, s0_ref,
         ob_ref, sf_ref, r_s, v_s, av_s, lw_s, kd_s, bv_s, y_s, bon_s, gate_s, st_s, mbd_s, mst_s, min_s) = refs
    else:
        (xb_ref, mu_ref, w0_ref, w2_ref, a0_ref, a2_ref, g2_ref, kk_ref, ka_ref, rk_ref, lnw_ref, lnb_ref,
         ob_ref, sf_ref, r_s, v_s, av_s, lw_s, kd_s, bv_s, y_s, bon_s, gate_s, st_s, mbd_s, mst_s, min_s) = refs
    n_chunks = T // CH
    n_blk = T // SEQ_BLK
    hc = B_HEADS * CH
    seg = _seg_matrix(B_WIDTH, B_DIM, 1.0)
    row = lax.broadcasted_iota(jnp.int32, (SEQ_BLK, 1), 0)
    for b in range(n_blk):
        r0 = b * SEQ_BLK
        rb = slice(r0, r0 + SEQ_BLK)
        x = xb_ref[rb, :]
        before = xb_ref[r0 - 1:r0, :] if b > 0 else jnp.zeros((1, B_COLS), F32)
        after = xb_ref[r0 + SEQ_BLK:r0 + SEQ_BLK + 1, :] if b < n_blk - 1 else jnp.zeros((1, B_COLS), F32)
        prev = jnp.where(row == 0, before, pltpu.roll(x, 1, 0))
        nxt = jnp.where(row == SEQ_BLK - 1, after, pltpu.roll(x, SEQ_BLK - 1, 0))
        xs = x + mu_ref[...] * (0.5 * (prev + nxt) - x)
        r = xs[:, 0:B_WIDTH]
        kb = xs[:, B_WIDTH:2 * B_WIDTH]
        vb = xs[:, 2 * B_WIDTH:3 * B_WIDTH]
        wd = xs[:, 3 * B_WIDTH:3 * B_WIDTH + DECAY_LORA]
        ad = xs[:, 3 * B_WIDTH + DECAY_LORA:3 * B_WIDTH + DECAY_LORA + ICLR_LORA]
        gd = xs[:, 3 * B_WIDTH + DECAY_LORA + ICLR_LORA:]
        kk = kb * kk_ref[...]
        kkn = kk * lax.rsqrt(_dot_data_const(kk * kk, seg) + 1e-6)
        r_s[rb, :] = r
        v_s[rb, :] = vb
        av_s[rb, :] = -kkn
        bon_s[rb, :] = _dot_data_const(r * kb * rk_ref[...], seg) * vb
        gate_s[rb, :] = _dot(jax.nn.sigmoid(gd), g2_ref[...])
        twd = jnp.tanh(wd)
        for d in range(2):
            wl = w0_ref[d:d + 1, :] + _dot(twd, w2_ref[d])
            w_log = -jax.nn.softplus(-wl) - 0.5
            lw_s[d, rb, :] = -jnp.exp(w_log)
            a = jax.nn.sigmoid(a0_ref[d:d + 1, :] + _dot(ad, a2_ref[d]))
            kd_s[d, rb, :] = kb * (1.0 + (a - 1.0) * ka_ref[...])
            bv_s[d, rb, :] = kkn * a
    if has_state:
        for d in range(2):
            st_s[d] = jnp.concatenate([s0_ref[0, d, h] for h in range(B_HEADS)], axis=1)
    else:
        st_s[...] = jnp.zeros_like(st_s)
    incl0, _ = _head_masks(B_HEADS, 0)
    incl1, _ = _head_masks(B_HEADS, 1)
    mbd_s[...] = jnp.where(incl0 | incl1, 1.0, 0.0).astype(F32)

    def expand(x):
        return jnp.concatenate([x] * B_HEADS, axis=1) * mbd_s[...]

    for d in range(2):
        incl, strict = _head_masks(B_HEADS, d)
        min_s[...] = jnp.where(incl, 1.0, 0.0).astype(F32)
        mst_s[...] = jnp.where(strict, 1.0, 0.0).astype(F32)
        tri = _chunk_tri(d)
        last = CH - 1 if d == 0 else 0

        def chunk_body(it, carry, d=d, tri=tri, last=last):
            c = it if d == 0 else n_chunks - 1 - it
            rows = pl.ds(pl.multiple_of(c * CH, CH), CH)
            lwc = lw_s[d, rows, :]
            cum = _dot_const_data(tri,lwc)
            e_pos = jnp.exp(cum)
            e_neg = jnp.exp(-cum)
            rt = _head_stack(r_s[rows, :] * e_pos, B_HEADS, B_DIM)
            at = _head_stack(av_s[rows, :] * jnp.exp(cum - lwc), B_HEADS, B_DIM)
            bk = jnp.concatenate([_head_stack(bv_s[d, rows, :] * e_neg, B_HEADS, B_DIM),
                                  _head_stack(kd_s[d, rows, :] * e_neg, B_HEADS, B_DIM)], axis=0)
            vs = _head_stack(v_s[rows, :], B_HEADS, B_DIM)
            pc = e_pos[last:last + 1, :]
            g1 = _dot_nt(at, bk)
            g2 = _dot_nt(rt, bk)
            mst = mst_s[...]
            a_ab = g1[:, 0:hc] * mst
            a_ak = g1[:, hc:] * mst
            mi = min_s[...]
            a_r = jnp.concatenate([g2[:, 0:hc] * mi, g2[:, hc:] * mi], axis=1)
            xm = _unit_lower_solve(a_ab, jnp.concatenate([at, _dot(a_ak, vs)], axis=1), RWKV_SOLVE_BLK)
            s_old = st_s[d]
            ws = _dot_nt(jnp.concatenate([expand(xm[:, 0:B_DIM]), expand(rt)], axis=0), s_old)
            uv = jnp.concatenate([ws[0:hc, :] + xm[:, B_DIM:], vs], axis=0)
            ys = ws[hc:, :] + _dot(a_r, uv)
            bk_e = jnp.concatenate([expand(bk[0:hc, :]), expand(bk[hc:, :])], axis=0)
            st_s[d] = (s_old + _dot_tn(uv, bk_e)) * pc
            yc = _head_unstack(ys, B_HEADS)
            if d == 0:
                y_s[rows, :] = yc
            else:
                y_s[rows, :] = y_s[rows, :] + yc
            return carry

        lax.fori_loop(0, n_chunks, chunk_body, 0)
    for b in range(n_blk):
        rb = slice(b * SEQ_BLK, (b + 1) * SEQ_BLK)
        y = y_s[rb, :]
        yc = y - _dot_data_const(y, seg) * (1.0 / B_DIM)
        var = _dot_data_const(yc * yc, seg) * (1.0 / B_DIM)
        yn = yc * lax.rsqrt(var + LNX_EPS) * lnw_ref[...] + lnb_ref[...]
        ob_ref[rb, :] = (yn + bon_s[rb, :]) * gate_s[rb, :]
    for d in range(2):
        sd = st_s[d]
        for h in range(B_HEADS):
            sf_ref[0, d, h] = sd[:, h * B_DIM:(h + 1) * B_DIM]


def _seq_spec(T, cols, row_off_blocks):
    extra = {'pipeline_mode': pl.Buffered(1)} if T > SEQ_BLK else {}
    return pl.BlockSpec((T, cols), lambda i: (row_off_blocks + i, 0), **extra)


def _rwkv_call(xb, row_off_blocks, n_seq, T, p, s0=None):
    has_state = s0 is not None
    st_shape = (2, B_HEADS, B_DIM, B_DIM)
    hc = B_HEADS * CH
    in_specs = [_seq_spec(T, B_COLS, row_off_blocks),
                _full((1, B_COLS)), _full((2, B_WIDTH)), _full((2, DECAY_LORA, B_WIDTH)), _full((2, B_WIDTH)),
                _full((2, ICLR_LORA, B_WIDTH)), _full((GATE_LORA, B_WIDTH))] + [_full((1, B_WIDTH))] * 5
    args = [xb, p['b_mu'].reshape(1, -1), p['b_w0'], p['b_w2'], p['b_a0'], p['b_a2'], p['b_g2'],
            p['b_kk'].reshape(1, -1), p['b_ka'].reshape(1, -1), p['b_rk'].reshape(1, -1),
            p['b_lnx_w'].reshape(1, -1), p['b_lnx_b'].reshape(1, -1)]
    if has_state:
        in_specs.append(pl.BlockSpec((1,) + st_shape, lambda i: (i, 0, 0, 0, 0)))
        args.append(s0)
    scr = [pltpu.VMEM((T, B_WIDTH), F32)] * 3 + [pltpu.VMEM((2, T, B_WIDTH), F32)] * 3 + \
          [pltpu.VMEM((T, B_WIDTH), F32)] * 3 + [pltpu.VMEM((2, B_DIM, B_WIDTH), F32)] + \
          [pltpu.VMEM((hc, hc), F32)] * 3
    return pl.pallas_call(
        functools.partial(_rwkv_kernel, T, has_state),
        grid=(n_seq,),
        in_specs=in_specs,
        out_specs=[pl.BlockSpec((T, B_WIDTH), lambda i: (i, 0)),
                   pl.BlockSpec((1,) + st_shape, lambda i: (i, 0, 0, 0, 0))],
        out_shape=[jax.ShapeDtypeStruct((n_seq * T, B_WIDTH), F32),
                   jax.ShapeDtypeStruct((n_seq,) + st_shape, F32)],
        scratch_shapes=scr,
        compiler_params=pltpu.CompilerParams(vmem_limit_bytes=VMEM_LIMIT),
    )(*args)


def _delta_kernel(T, has_state, *refs):
    if has_state:
        (c_ref, ab_ref, conv_ref, arow_ref, dtrow_ref, on_ref, s0_ref, oc_ref, sf_ref,
         q_s, k_s, v_s, g_s, b_s, o_s, st_s) = refs
    else:
        (c_ref, ab_ref, conv_ref, arow_ref, dtrow_ref, on_ref, oc_ref, sf_ref,
         q_s, k_s, v_s, g_s, b_s, o_s, st_s) = refs
    n_chunks = T // CH
    n_blk = T // SEQ_BLK
    hc = C_HEADS * CH
    row = lax.broadcasted_iota(jnp.int32, (SEQ_BLK, 1), 0)
    for b in range(n_blk):
        r0 = b * SEQ_BLK
        rb = slice(r0, r0 + SEQ_BLK)
        x = c_ref[rb, 0:3 * C_WIDTH]
        before = c_ref[r0 - 1:r0, 0:3 * C_WIDTH] if b > 0 else jnp.zeros((1, 3 * C_WIDTH), F32)
        after = c_ref[r0 + SEQ_BLK:r0 + SEQ_BLK + 1, 0:3 * C_WIDTH] if b < n_blk - 1 else jnp.zeros((1, 3 * C_WIDTH), F32)
        prev = jnp.where(row == 0, before, pltpu.roll(x, 1, 0))
        nxt = jnp.where(row == SEQ_BLK - 1, after, pltpu.roll(x, SEQ_BLK - 1, 0))
        xc = _silu(conv_ref[0:1, :] * prev + conv_ref[1:2, :] * x + conv_ref[2:3, :] * nxt)
        for h in range(C_HEADS):
            sl = slice(h * C_DIM, (h + 1) * C_DIM)
            qh = xc[:, h * C_DIM:(h + 1) * C_DIM]
            kh = xc[:, C_WIDTH + h * C_DIM:C_WIDTH + (h + 1) * C_DIM]
            q_s[rb, sl] = qh * lax.rsqrt(jnp.sum(qh * qh, axis=-1, keepdims=True) + 1e-6) * (C_DIM ** -0.5)
            k_s[rb, sl] = kh * lax.rsqrt(jnp.sum(kh * kh, axis=-1, keepdims=True) + 1e-6)
        v_s[rb, :] = xc[:, 2 * C_WIDTH:]
        ab = ab_ref[rb, :]
        g_s[rb, :] = arow_ref[...] * jax.nn.softplus(ab + dtrow_ref[...])
        b_s[rb, :] = jax.nn.sigmoid(ab)
    if has_state:
        for d in range(2):
            st_s[d] = jnp.concatenate([s0_ref[0, d, h] for h in range(C_HEADS)], axis=0)
    else:
        st_s[...] = jnp.zeros_like(st_s)
    ei = lax.broadcasted_iota(jnp.int32, (hc, C_WIDTH), 0) // CH
    ej = lax.broadcasted_iota(jnp.int32, (hc, C_WIDTH), 1) // C_DIM
    own = ei == ej

    def expand(x):
        return jnp.where(own, jnp.concatenate([x] * C_HEADS, axis=1), 0.0)

    ri = lax.broadcasted_iota(jnp.int32, (hc, hc), 0)
    rj = lax.broadcasted_iota(jnp.int32, (hc, hc), 1)
    eye = jnp.where(ri == rj, 1.0, 0.0).astype(F32)
    for d in range(2):
        incl, strict = _head_masks(C_HEADS, d)
        tri = _chunk_tri(d)
        last = CH - 1 if d == 0 else 0
        j0 = d * C_HEADS

        def chunk_body(it, carry, d=d, incl=incl, strict=strict, tri=tri, last=last, j0=j0):
            c = it if d == 0 else n_chunks - 1 - it
            rows = pl.ds(pl.multiple_of(c * CH, CH), CH)
            gcum = _dot_const_data(tri,g_s[rows, :])
            bet = b_s[rows, :]
            heads = range(C_HEADS)
            g_col = jnp.concatenate([gcum[:, j0 + h:j0 + h + 1] for h in heads], axis=0)
            b_col = jnp.concatenate([bet[:, 2 * C_HEADS + j0 + h:2 * C_HEADS + j0 + h + 1] for h in heads], axis=0)
            g_end = [gcum[last:last + 1, j0 + h:j0 + h + 1] for h in heads]
            g_last = jnp.concatenate([jnp.broadcast_to(g, (CH, 1)) for g in g_end], axis=0)
            e_last = jnp.concatenate([jnp.broadcast_to(jnp.exp(g), (C_DIM, 1)) for g in g_end], axis=0)
            g_row = jnp.sum(eye * g_col, axis=0, keepdims=True)
            decay = jnp.where(incl, jnp.exp(jnp.where(incl, g_col - g_row, 0.0)), 0.0)
            qs = _head_stack(q_s[rows, :], C_HEADS, C_DIM)
            ks = _head_stack(k_s[rows, :], C_HEADS, C_DIM)
            vs = _head_stack(v_s[rows, :], C_HEADS, C_DIM)
            kbeta = ks * b_col
            gm = _dot_nt(jnp.concatenate([kbeta, qs], axis=0), ks)
            attn = gm[hc:, :] * decay
            eg = jnp.exp(g_col)
            xm = _unit_lower_solve(-jnp.where(strict, gm[0:hc, :] * decay, 0.0),
                                   jnp.concatenate([vs * b_col, kbeta * eg], axis=1), hc)
            s_old = st_s[d]
            ws = _dot(jnp.concatenate([expand(xm[:, C_DIM:]), expand(qs * eg)], axis=0), s_old)
            v_new = xm[:, 0:C_DIM] - ws[0:hc, :]
            o = ws[hc:, :] + _dot(attn, v_new)
            st_s[d] = s_old * e_last + _dot_tn(expand(ks * jnp.exp(g_last - g_col)), v_new)
            oc = _head_unstack(o, C_HEADS)
            if d == 0:
                o_s[rows, :] = oc
            else:
                o_s[rows, :] = o_s[rows, :] + oc
            return carry

        lax.fori_loop(0, n_chunks, chunk_body, 0)
    for b in range(n_blk):
        rb = slice(b * SEQ_BLK, (b + 1) * SEQ_BLK)
        z = c_ref[rb, 3 * C_WIDTH:]
        for h in range(C_HEADS):
            sl = slice(h * C_DIM, (h + 1) * C_DIM)
            oh = o_s[rb, sl]
            on = oh * lax.rsqrt(jnp.mean(oh * oh, axis=-1, keepdims=True) + NORM_EPS) * on_ref[...]
            oc_ref[rb, sl] = on * _silu(z[:, sl])
    for d in range(2):
        sd = st_s[d]
        for h in range(C_HEADS):
            sf_ref[0, d, h] = sd[h * C_DIM:(h + 1) * C_DIM, :]


def _delta_call(cmat, abmat, row_off_blocks, n_seq, T, p, s0=None):
    has_state = s0 is not None
    st_shape = (2, C_HEADS, C_DIM, C_DIM)
    arow = jnp.zeros((1, LANE), F32).at[0, 0:2 * C_HEADS].set(-jnp.exp(p['c_A_log'].reshape(-1)))
    dtrow = jnp.zeros((1, LANE), F32).at[0, 0:2 * C_HEADS].set(p['c_dt_bias'].reshape(-1))
    in_specs = [pl.BlockSpec((T, 4 * C_WIDTH), lambda i: (row_off_blocks + i, 0)),
                pl.BlockSpec((T, LANE), lambda i: (row_off_blocks + i, 0)),
                _full((3, 3 * C_WIDTH)), _full((1, LANE)), _full((1, LANE)), _full((1, C_DIM))]
    args = [cmat, abmat, p['c_conv'], arow, dtrow, p['c_onorm'].reshape(1, -1)]
    if has_state:
        in_specs.append(pl.BlockSpec((1,) + st_shape, lambda i: (i, 0, 0, 0, 0)))
        args.append(s0)
    scr = [pltpu.VMEM((T, C_WIDTH), F32)] * 3 + [pltpu.VMEM((T, LANE), F32)] * 2 + \
          [pltpu.VMEM((T, C_WIDTH), F32), pltpu.VMEM((2, C_HEADS * C_DIM, C_DIM), F32)]
    return pl.pallas_call(
        functools.partial(_delta_kernel, T, has_state),
        grid=(n_seq,),
        in_specs=in_specs,
        out_specs=[pl.BlockSpec((T, C_WIDTH), lambda i: (i, 0)),
                   pl.BlockSpec((1,) + st_shape, lambda i: (i, 0, 0, 0, 0))],
        out_shape=[jax.ShapeDtypeStruct((n_seq * T, C_WIDTH), F32),
                   jax.ShapeDtypeStruct((n_seq,) + st_shape, F32)],
        scratch_shapes=scr,
        compiler_params=pltpu.CompilerParams(vmem_limit_bytes=VMEM_LIMIT),
    )(*args)


def _first_max(x, iota, size):
    m = jnp.max(x, axis=0, keepdims=True)
    idx = jnp.min(jnp.where(x == m, iota, size), axis=0, keepdims=True)
    return m, idx


def _route_kernel(lg_ref, bias_ref, idx_ref, wts_ref):
    n = lg_ref.shape[1]
    scores = jax.nn.sigmoid(lg_ref[...])
    biased = scores + bias_ref[...]
    e_iota = lax.broadcasted_iota(jnp.int32, (N_EXPERTS, n), 0)
    g_iota = lax.broadcasted_iota(jnp.int32, (PER_GROUP, n), 0)
    gs = []
    for g in range(N_GROUPS):
        xg = biased[g * PER_GROUP:(g + 1) * PER_GROUP, :]
        m1, i1 = _first_max(xg, g_iota, PER_GROUP)
        m2 = jnp.max(jnp.where(g_iota == i1, NEG, xg), axis=0, keepdims=True)
        gs.append(m1 + m2)
    gscore = jnp.concatenate(gs, axis=0)
    gi = lax.broadcasted_iota(jnp.int32, (N_GROUPS, n), 0)
    gsel = jnp.zeros((N_GROUPS, n), F32)
    for _ in range(TOPK_GROUPS):
        _, ig = _first_max(gscore, gi, N_GROUPS)
        hit = gi == ig
        gsel = jnp.where(hit, 1.0, gsel)
        gscore = jnp.where(hit, NEG, gscore)
    masked = jnp.concatenate(
        [jnp.where(gsel[g:g + 1, :] > 0.0, biased[g * PER_GROUP:(g + 1) * PER_GROUP, :], NEG)
         for g in range(N_GROUPS)], axis=0)
    ids, ws = [], []
    for _ in range(TOP_K):
        _, ie = _first_max(masked, e_iota, N_EXPERTS)
        hit = e_iota == ie
        ids.append(ie)
        ws.append(jnp.sum(jnp.where(hit, scores, 0.0), axis=0, keepdims=True))
        masked = jnp.where(hit, NEG, masked)
    wsum = ws[0]
    for w in ws[1:]:
        wsum = wsum + w
    inv = ROUTED_SCALE / (wsum + 1e-20)
    idx_ref[...] = jnp.concatenate(ids + [jnp.zeros((8 - TOP_K, n), jnp.int32)], axis=0)
    wts_ref[...] = jnp.concatenate([w * inv for w in ws] + [jnp.zeros((8 - TOP_K, n), F32)], axis=0)


def _route_call(logits_t, bias):
    m = logits_t.shape[1]
    return pl.pallas_call(
        _route_kernel,
        grid=(m // TM,),
        in_specs=[pl.BlockSpec((N_EXPERTS, TM), lambda i: (0, i)), _full((N_EXPERTS, 1))],
        out_specs=[pl.BlockSpec((8, TM), lambda i: (0, i))] * 2,
        out_shape=[jax.ShapeDtypeStruct((8, m), jnp.int32), jax.ShapeDtypeStruct((8, m), F32)],
    )(logits_t, bias.reshape(N_EXPERTS, 1))


def _dispatch_tables(idx, m):
    n_asg = m * TOP_K
    nb = -(-n_asg // BM) + N_EXPERTS
    n_pad = nb * BM - n_asg
    flat_e = idx[0:TOP_K, :].reshape(-1)
    e_iota = jnp.arange(N_EXPERTS, dtype=jnp.int32)
    counts = jnp.sum((flat_e[:, None] == e_iota[None, :]).astype(jnp.int32), axis=0)
    padded = (counts + BM - 1) // BM * BM
    pad_end = jnp.cumsum(padded)
    pad_cum = jnp.cumsum(padded - counts)
    pad_e = jnp.sum((pad_cum[None, :] <= jnp.arange(n_pad, dtype=jnp.int32)[:, None]).astype(jnp.int32), axis=1)
    keys = jnp.concatenate([flat_e * 2, pad_e * 2 + 1])
    vals = jnp.concatenate([jnp.arange(n_asg, dtype=jnp.int32), jnp.full((n_pad,), -1, jnp.int32)])
    _, slot_asg = lax.sort((keys, vals), num_keys=1, is_stable=True)
    valid = slot_asg >= 0
    spare = TOP_K * m + jnp.arange(nb * BM, dtype=jnp.int32) % BM
    slot_dst = jnp.where(valid, slot_asg, spare)
    slot_tok = jnp.where(valid, slot_asg % m, 0)
    blk0 = jnp.arange(nb, dtype=jnp.int32) * BM
    block_expert = jnp.minimum(jnp.sum((pad_end[None, :] <= blk0[:, None]).astype(jnp.int32), axis=1), N_EXPERTS - 1)
    n_used = (pad_end[-1] // BM).astype(jnp.int32).reshape(1)
    return slot_tok.reshape(nb, 1, BM), slot_dst.reshape(nb, 1, BM), block_expert.astype(jnp.int32), n_used


def _expert_kernel(be_ref, nu_ref, st_ref, stn_ref, sd_ref, tok_hbm, wg_ref, wu_ref, wd_ref, y_hbm,
                   xbuf, ybuf, sem_in, sem_out):
    i = pl.program_id(0)
    n_used = nu_ref[0]
    slot = i % 2

    def gather(tab_ref, b):
        def body(r, c):
            pltpu.make_async_copy(tok_hbm.at[pl.ds(tab_ref[0, 0, r], 1), :], xbuf.at[b, pl.ds(r, 1), :],
                                  sem_in.at[b]).start()
            return c
        lax.fori_loop(0, BM, body, 0, unroll=DMA_UNROLL)

    def wait_gather(b):
        pltpu.make_async_copy(tok_hbm.at[pl.ds(0, BM), :], xbuf.at[b], sem_in.at[b]).wait()

    def wait_scatter(b):
        pltpu.make_async_copy(ybuf.at[b], y_hbm.at[pl.ds(0, BM), :], sem_out.at[b]).wait()

    @pl.when(i == 0)
    def _():
        ybuf[0] = jnp.zeros((BM, D_MODEL), F32)
        spare = pltpu.make_async_copy(ybuf.at[0], y_hbm.at[pl.ds(y_hbm.shape[0] - BM, BM), :], sem_out.at[0])
        spare.start()
        spare.wait()
        gather(st_ref, 0)

    @pl.when(i + 1 < n_used)
    def _():
        gather(stn_ref, 1 - slot)

    @pl.when(i < n_used)
    def _():
        wait_gather(slot)

        @pl.when(i >= 2)
        def _():
            wait_scatter(slot)

        x = xbuf[slot].astype(BF16)
        g = jnp.dot(x, wg_ref[0, 0].astype(BF16), preferred_element_type=F32)
        u = jnp.dot(x, wu_ref[0, 0].astype(BF16), preferred_element_type=F32)
        ybuf[slot] = jnp.dot((_silu(g) * u).astype(BF16), wd_ref[0, 0].astype(BF16), preferred_element_type=F32)

        def body(r, c):
            pltpu.make_async_copy(ybuf.at[slot, pl.ds(r, 1), :], y_hbm.at[pl.ds(sd_ref[0, 0, r], 1), :],
                                  sem_out.at[slot]).start()
            return c
        lax.fori_loop(0, BM, body, 0, unroll=DMA_UNROLL)

        @pl.when(i == n_used - 1)
        def _():
            wait_scatter(slot)

            @pl.when(i >= 1)
            def _():
                wait_scatter(1 - slot)


def _expert_call(tok, tables, layer, wg, wu, wd):
    m = tok.shape[0]
    slot_tok, slot_dst, block_expert, n_used = tables
    nb = slot_tok.shape[0]
    smem_blk = lambda f: pl.BlockSpec((1, 1, BM), lambda i, be, nu: (f(i), 0, 0), memory_space=pltpu.SMEM)
    grid_spec = pltpu.PrefetchScalarGridSpec(
        num_scalar_prefetch=2,
        grid=(nb,),
        in_specs=[smem_blk(lambda i: i), smem_blk(lambda i: jnp.minimum(i + 1, nb - 1)), smem_blk(lambda i: i),
                  pl.BlockSpec(memory_space=pl.ANY),
                  pl.BlockSpec((1, 1, D_MODEL, EXPERT_FF), lambda i, be, nu: (layer, be[i], 0, 0)),
                  pl.BlockSpec((1, 1, D_MODEL, EXPERT_FF), lambda i, be, nu: (layer, be[i], 0, 0)),
                  pl.BlockSpec((1, 1, EXPERT_FF, D_MODEL), lambda i, be, nu: (layer, be[i], 0, 0))],
        out_specs=pl.BlockSpec(memory_space=pl.ANY),
        scratch_shapes=[pltpu.VMEM((2, BM, D_MODEL), F32), pltpu.VMEM((2, BM, D_MODEL), F32),
                        pltpu.SemaphoreType.DMA((2,)), pltpu.SemaphoreType.DMA((2,))])
    return pl.pallas_call(
        _expert_kernel, grid_spec=grid_spec,
        out_shape=jax.ShapeDtypeStruct((TOP_K * m + BM, D_MODEL), F32),
        compiler_params=pltpu.CompilerParams(vmem_limit_bytes=VMEM_LIMIT),
    )(block_expert, n_used, slot_tok, slot_tok, slot_dst, tok, wg, wu, wd)


def _combine_kernel(x_ref, sh_ref, w_ref, mod_ref, *rest):
    y_refs, o_ref = rest[:TOP_K], rest[TOP_K]
    w = w_ref[...]
    acc = sh_ref[...]
    for k, y_ref in enumerate(y_refs):
        acc = acc + w[:, k:k + 1] * y_ref[...]
    o_ref[...] = x_ref[...] + mod_ref[0, 0, 5:6, :] * acc


def _combine_call(x, sh, wts_rows, mod, layer, yrows):
    nt = M_TOK // TM
    row = pl.BlockSpec((TM, D_MODEL), lambda i: (i, 0))
    ysp = [pl.BlockSpec((TM, D_MODEL), (lambda i, k=k: (k * nt + i, 0))) for k in range(TOP_K)]
    return pl.pallas_call(
        _combine_kernel,
        grid=(nt,),
        in_specs=[row, row, pl.BlockSpec((TM, 8), lambda i: (i, 0)),
                  pl.BlockSpec((1, 1, 6, D_MODEL), lambda i: (layer, _mod_group(i), 0, 0))] + ysp,
        out_specs=row,
        out_shape=jax.ShapeDtypeStruct((M_TOK, D_MODEL), F32),
        compiler_params=pltpu.CompilerParams(vmem_limit_bytes=VMEM_LIMIT),
    )(x, sh, wts_rows, mod, *([yrows] * TOP_K))


def kernel(x_prompt, x_sample, cache_attn_k, cache_attn_v, state_rwkv, state_delta, cache_diff_k,
           cache_diff_v, c, c_ctx, mod_w, mod_b, norm1_g, norm2_g, ev_w_in, ev_w_out, a_qn, a_kn, b_mu,
           b_w0, b_w2, b_a0, b_a2, b_g2, b_kk, b_ka, b_rk, b_lnx_w, b_lnx_b, od_w_in, od_w_out, c_conv,
           c_A_log, c_dt_bias, c_onorm, d_qn, d_kn, d_lambda, d_subln, router_w, router_bias, exp_w_gate,
           exp_w_up, exp_w_down, sh_w_gate, sh_w_up, sh_w_down):
    x = jnp.concatenate([x_prompt.reshape(N_CTX, D_MODEL), x_sample.reshape(N_LAT, D_MODEL)], axis=0)
    cond = jnp.concatenate([c_ctx[None], c, jnp.zeros((8 - N_MOD, D_MODEL), F32)], axis=0)
    mod = _adaln_call(cond, mod_w, mod_b)[:, 0:N_MOD].reshape(DEPTH, N_MOD, 6, D_MODEL)
    lat_blk = N_CTX // DEC_SEQ
    new_ak, new_av, new_sr, new_sd, new_dk, new_dv = [], [], [], [], [], []
    for l in range(DEPTH):
        j = l // 2
        if l % 2 == 0:
            w = ev_w_in[j]
            q, k, v = w[:, 0:512], w[:, 512:640], w[:, 640:768]
            w_a = jnp.concatenate([q[:, 0:256], k[:, 0:64], v[:, 0:64], q[:, 256:], k[:, 64:], v[:, 64:]], axis=1)
            amat, bmat = _proj_in_call(x, mod, l, norm1_g[l], [w_a.astype(BF16), w[:, 768:].astype(BF16)])
            gain = jnp.concatenate([jnp.tile(a_qn[j], A_GROUP), a_kn[j], jnp.ones((HD,), F32)]).reshape(1, GW)
            oa_c, nk, nv = _attn_a_call(amat, 0, BATCH, SEQ, gain)
            oa_l, = _attn_a_call(amat, lat_blk, DEC_BATCH, DEC_SEQ, gain, (cache_attn_k, cache_attn_v, j))
            p = {'b_mu': b_mu[j], 'b_w0': b_w0[j], 'b_w2': b_w2[j], 'b_a0': b_a0[j], 'b_a2': b_a2[j],
                 'b_g2': b_g2[j], 'b_kk': b_kk[j], 'b_ka': b_ka[j], 'b_rk': b_rk[j], 'b_lnx_w': b_lnx_w[j],
                 'b_lnx_b': b_lnx_b[j]}
            ob_c, sr = _rwkv_call(bmat, 0, BATCH, SEQ, p)
            ob_l, _ = _rwkv_call(bmat, lat_blk, DEC_BATCH, DEC_SEQ, p, state_rwkv[:, j])
            o1 = jnp.concatenate([oa_c, oa_l], axis=0)
            o2 = jnp.concatenate([ob_c, ob_l], axis=0)
            w_out = ev_w_out[j]
            new_ak.append(nk)
            new_av.append(nv)
            new_sr.append(sr)
        else:
            lam_init = 0.8 - 0.6 * math.exp(-0.3 * l)
            w = od_w_in[j]
            s0 = 4 * C_WIDTH + 4 * C_HEADS
            dq, dk, dv = w[:, s0:s0 + 512], w[:, s0 + 512:s0 + 1024], w[:, s0 + 1024:]
            w_ab = jnp.pad(w[:, 4 * C_WIDTH:s0], ((0, 0), (0, LANE - 4 * C_HEADS)))
            w_d = jnp.concatenate([jnp.concatenate([dq[:, 128 * h:128 * (h + 1)], dk[:, 128 * h:128 * (h + 1)],
                                                    dv[:, 128 * h:128 * (h + 1)]], axis=1) for h in range(D_HEADS)],
                                  axis=1)
            cmat, abmat, dmat = _proj_in_call(x, mod, l, norm1_g[l],
                                              [w[:, 0:4 * C_WIDTH].astype(BF16), w_ab.astype(BF16), w_d.astype(BF16)])
            p = {'c_conv': c_conv[j], 'c_A_log': c_A_log[j], 'c_dt_bias': c_dt_bias[j], 'c_onorm': c_onorm[j]}
            oc_c, sd_ = _delta_call(cmat, abmat, 0, BATCH, SEQ, p)
            oc_l, _ = _delta_call(cmat, abmat, lat_blk, DEC_BATCH, DEC_SEQ, p, state_delta[:, j])
            gain = jnp.concatenate([jnp.tile(d_qn[j], 2), jnp.tile(d_kn[j], 2), jnp.ones((D_VDIM,), F32)]).reshape(1, GW)
            od_c, ndk, ndv = _attn_d_call(dmat, 0, BATCH, SEQ, gain, d_lambda[j], d_subln[j], lam_init)
            od_l, = _attn_d_call(dmat, lat_blk, DEC_BATCH, DEC_SEQ, gain, d_lambda[j], d_subln[j], lam_init,
                                 (cache_diff_k, cache_diff_v, j))
            o1 = jnp.concatenate([oc_c, oc_l], axis=0)
            o2 = jnp.concatenate([od_c, od_l], axis=0)
            w_out = od_w_out[j]
            new_dk.append(ndk)
            new_dv.append(ndv)
            new_sd.append(sd_)
        x, tok, sh, logits_t = _proj_out_call(
            o1, o2, x, mod, l, norm2_g[l], w_out.astype(BF16), router_w[l].T.astype(BF16),
            sh_w_gate[l].astype(BF16), sh_w_up[l].astype(BF16), sh_w_down[l].astype(BF16))
        idx, wts = _route_call(logits_t, router_bias[l])
        tables = _dispatch_tables(idx, M_TOK)
        yrows = _expert_call(tok, tables, l, exp_w_gate, exp_w_up, exp_w_down)
        x = _combine_call(x, sh, wts.T, mod, l, yrows)
    return (x[0:N_CTX].reshape(BATCH, SEQ, D_MODEL), x[N_CTX:].reshape(DEC_BATCH, DEC_SEQ, D_MODEL),
            jnp.stack(new_ak, axis=1), jnp.stack(new_av, axis=1), jnp.stack(new_sr, axis=1),
            jnp.stack(new_sd, axis=1), jnp.stack(new_dk, axis=1), jnp.stack(new_dv, axis=1))
```
